```python
import jax, jax.numpy as jnp
from jax import lax
import numpy as np

D_MODEL = 1024
BATCH = 32
SEQ = 256
DEPTH = 2
DEC_BATCH = 2
DEC_SEQ = 4096
PAST_LEN = 256

GRID_W = 64
N_MIXERS = 2
N_RET_LAYERS = (DEPTH + 1) // 2
N_CONV_LAYERS = DEPTH // 2
RET_HEADS = 4
RET_DK = D_MODEL // RET_HEADS
RET_DV = 2 * D_MODEL // RET_HEADS
RET_HK = RET_HEADS * RET_DK
RET_HV = RET_HEADS * RET_DV
RET_CHUNK = 128
ROPE_BASE = 10000.0
CONV_WIDTH = 31
N_EXPERTS = 32
TOP_K = 4
D_EXPERT = D_MODEL
SWIGLU_LIMIT = 7.0
SWIGLU_ALPHA = 1.702
MOE_BLOCK = 128
EPS = 1e-6

kernel_name = 'hybrid_retention_conformer_moe_diffusion_step'


def rmsnorm(x, g):
    xf = x.astype(jnp.float32)
    y = xf * lax.rsqrt(jnp.mean(xf * xf, axis=-1, keepdims=True) + EPS)
    return (y * g.astype(jnp.float32)).astype(x.dtype)


def axial_rope(z):
    n = z.shape[1]
    t = jnp.arange(n)
    row = (t // GRID_W).astype(jnp.float32)
    col = (t % GRID_W).astype(jnp.float32)
    nf = RET_DK // 4
    inv = ROPE_BASE ** (-jnp.arange(nf, dtype=jnp.float32) / nf)

    def rot(u, pos):
        ang = pos[:, None] * inv[None, :]
        cos = jnp.cos(ang)[None, :, None, :]
        sin = jnp.sin(ang)[None, :, None, :]
        u1, u2 = jnp.split(u, 2, axis=-1)
        return jnp.concatenate([u1 * cos - u2 * sin, u1 * sin + u2 * cos], axis=-1)

    half = RET_DK // 2
    return jnp.concatenate([rot(z[..., :half], row), rot(z[..., half:], col)], axis=-1)


def retention_scan(q, k, v, log_g, s0):
    b, n, h, _ = q.shape
    nc = n // RET_CHUNK

    def chunks(z):
        return z.reshape(b, nc, RET_CHUNK, h, z.shape[-1]).transpose(1, 0, 3, 2, 4)

    qc, kc, vc = chunks(q), chunks(k), chunks(v)
    pos = jnp.arange(RET_CHUNK, dtype=jnp.float32)
    diff = pos[:, None] - pos[None, :]
    causal = diff >= 0
    dmat = jnp.where(causal[None], jnp.exp(log_g[:, None, None] * jnp.maximum(diff, 0.0)[None]), 0.0)
    read_decay = jnp.exp(log_g[:, None] * (pos[None, :] + 1.0))[None, :, :, None]
    write_decay = jnp.exp(log_g[:, None] * (RET_CHUNK - 1.0 - pos[None, :]))[None, :, :, None]
    chunk_decay = jnp.exp(log_g * RET_CHUNK)[None, :, None, None]

    def step(s, inp):
        qi, ki, vi = inp
        scores = jnp.einsum('bhqd,bhkd->bhqk', qi, ki) * dmat[None]
        o = jnp.einsum('bhqk,bhkv->bhqv', scores, vi) + jnp.einsum('bhqd,bhdv->bhqv', qi, s) * read_decay
        s = s * chunk_decay + jnp.einsum('bhkd,bhkv->bhdv', ki * write_decay, vi)
        return s, o

    s_fin, o = lax.scan(step, s0, (qc, kc, vc))
    o = o.transpose(1, 0, 3, 2, 4).reshape(b, n, h, v.shape[-1])
    return o, s_fin


def retention_mixer(h, w_in, decay_logit, w_out, s0_f, s0_b, latent):
    b, n, _ = h.shape
    proj = h @ w_in
    q, k, v, g = jnp.split(proj, [RET_HK, 2 * RET_HK, 2 * RET_HK + RET_HV], axis=-1)
    q = q.reshape(b, n, RET_HEADS, RET_DK).astype(jnp.float32)
    k = k.reshape(b, n, RET_HEADS, RET_DK).astype(jnp.float32)
    v = v.reshape(b, n, RET_HEADS, RET_DV).astype(jnp.float32)
    if latent:
        q = axial_rope(q)
        k = axial_rope(k)
    k = k * (RET_DK ** -0.5)
    log_g = jax.nn.log_sigmoid(decay_logit.astype(jnp.float32))
    o_f, s_f = retention_scan(q, k, v, log_g[0], s0_f)
    o_b, s_b = retention_scan(jnp.flip(q, 1), jnp.flip(k, 1), jnp.flip(v, 1), log_g[1], s0_b)
    o = o_f + jnp.flip(o_b, 1)
    o = o * lax.rsqrt(jnp.mean(o * o, axis=-1, keepdims=True) + EPS)
    o = o.reshape(b, n, RET_HV).astype(h.dtype) * jax.nn.silu(g)
    return o @ w_out, s_f, s_b


def conv_mixer(h, w_pw1, b_pw1, w_dw, b_dw, ln_g, ln_b, w_pw2, b_pw2, latent):
    b, n, d = h.shape
    a, gt = jnp.split(h @ w_pw1 + b_pw1, 2, axis=-1)
    u = a * jax.nn.sigmoid(gt)
    if latent:
        rows = n // GRID_W
        u = u.reshape(b * rows, GRID_W, d)
    u = lax.conv_general_dilated(u, w_dw[:, None, :], (1,), ((CONV_WIDTH // 2, CONV_WIDTH // 2),),
                                 dimension_numbers=('NWC', 'WIO', 'NWC'), feature_group_count=d) + b_dw
    u = u.reshape(b, n, d)
    uf = u.astype(jnp.float32)
    mu = jnp.mean(uf, axis=-1, keepdims=True)
    var = jnp.mean(jnp.square(uf - mu), axis=-1, keepdims=True)
    u = ((uf - mu) * lax.rsqrt(var + EPS) * ln_g.astype(jnp.float32) + ln_b.astype(jnp.float32)).astype(h.dtype)
    return jax.nn.silu(u) @ w_pw2 + b_pw2


def moe_ffn(h, w_router, b_router, w_up, b_up, w_down, b_down):
    shape = h.shape
    x = h.reshape(-1, shape[-1])
    t = x.shape[0]
    logits = (x @ w_router + b_router).astype(jnp.float32)
    top_v, top_i = lax.top_k(logits, TOP_K)
    gates = jax.nn.softmax(top_v, axis=-1)
    a = t * TOP_K
    e_flat = top_i.reshape(-1)
    tok_flat = jnp.arange(a, dtype=jnp.int32) // TOP_K
    order = jnp.argsort(e_flat)
    e_s = e_flat[order]
    tok_s = tok_flat[order]
    g_s = gates.reshape(-1)[order]
    counts = jnp.bincount(e_flat, length=N_EXPERTS)
    padded = (counts + MOE_BLOCK - 1) // MOE_BLOCK * MOE_BLOCK
    ends = jnp.cumsum(counts)
    pends = jnp.cumsum(padded)
    dest = (pends - padded)[e_s] + jnp.arange(a) - (ends - counts)[e_s]
    n_blocks = -(-a // MOE_BLOCK) + N_EXPERTS
    n_slots = n_blocks * MOE_BLOCK
    slot_tok = jnp.full((n_slots,), t, jnp.int32).at[dest].set(tok_s)
    slot_gate = jnp.zeros((n_slots,), jnp.float32).at[dest].set(g_s)
    block_expert = jnp.minimum(jnp.searchsorted(pends, jnp.arange(n_blocks) * MOE_BLOCK, side='right'), N_EXPERTS - 1)
    x_pad = jnp.concatenate([x, jnp.zeros((1, x.shape[1]), x.dtype)], axis=0)

    def expert_block(args):
        toks, e = args
        xb = x_pad[toks]
        glu, lin = jnp.split(xb @ w_up[e] + b_up[e], 2, axis=-1)
        glu = jnp.minimum(glu, SWIGLU_LIMIT)
        lin = jnp.clip(lin, -SWIGLU_LIMIT, SWIGLU_LIMIT)
        act = glu * jax.nn.sigmoid(SWIGLU_ALPHA * glu) * (lin + 1.0)
        return act @ w_down[e] + b_down[e]

    out = lax.map(expert_block, (slot_tok.reshape(n_blocks, MOE_BLOCK), block_expert))
    out = out.reshape(n_slots, -1) * slot_gate[:, None].astype(out.dtype)
    y = jnp.zeros((t + 1, out.shape[-1]), out.dtype).at[slot_tok].add(out)[:t]
    return y.reshape(shape)


def trunk(x, cvec, latent, ret_f, ret_b, w_ada, b_ada, g_norm1, g_norm2, ret_w_in, ret_decay, ret_w_out,
          conv_w_pw1, conv_b_pw1, conv_w_dw, conv_b_dw, conv_ln_g, conv_ln_b, conv_w_pw2, conv_b_pw2,
          moe_w_router, moe_b_router, moe_w_up, moe_b_up, moe_w_down, moe_b_down, g_final):
    b = x.shape[0]
    new_f, new_b = [], []
    for l in range(DEPTH):
        mod = (jax.nn.silu(cvec) @ w_ada[l] + b_ada[l])[:, None, :]
        sh1, sc1, gt1, sh2, sc2, gt2 = jnp.split(mod, 6, axis=-1)
        h = rmsnorm(x, g_norm1[l]) * (1.0 + sc1) + sh1
        if l % N_MIXERS == 0:
            r = l // N_MIXERS
            if latent:
                s0_f = ret_f[:, r].astype(jnp.float32)
                s0_b = ret_b[:, r].astype(jnp.float32)
            else:
                s0_f = jnp.zeros((b, RET_HEADS, RET_DK, RET_DV), jnp.float32)
                s0_b = jnp.zeros((b, RET_HEADS, RET_DK, RET_DV), jnp.float32)
            y, s_f, s_b = retention_mixer(h, ret_w_in[r], ret_decay[r], ret_w_out[r], s0_f, s0_b, latent)
            if not latent:
                new_f.append(s_f.astype(x.dtype))
                new_b.append(s_b.astype(x.dtype))
        else:
            ci = l // N_MIXERS
            y = conv_mixer(h, conv_w_pw1[ci], conv_b_pw1[ci], conv_w_dw[ci], conv_b_dw[ci], conv_ln_g[ci],
                           conv_ln_b[ci], conv_w_pw2[ci], conv_b_pw2[ci], latent)
        x = x + gt1 * y
        h = rmsnorm(x, g_norm2[l]) * (1.0 + sc2) + sh2
        x = x + gt2 * moe_ffn(h, moe_w_router[l], moe_b_router[l], moe_w_up[l], moe_b_up[l],
                              moe_w_down[l], moe_b_down[l])
    x = rmsnorm(x, g_final)
    if latent:
        return x, None, None
    return x, jnp.stack(new_f, axis=1), jnp.stack(new_b, axis=1)


def setup_inputs(seed: int = 0) -> dict:
    key = jax.random.key(seed)
    ks = jax.random.split(key, 32)
    f32 = jnp.float32
    d = D_MODEL

    def nrm(k, shape, scale):
        return jax.random.normal(k, shape, f32) * scale

    gam = 1.0 - 2.0 ** (-5.0 - np.arange(RET_HEADS, dtype=np.float32))
    decay_init = jnp.asarray(np.log(gam / (1.0 - gam)), f32)
    return {
        'x_prompt': nrm(ks[0], (BATCH, SEQ, d), 1.0),
        'x_sample': nrm(ks[1], (DEC_BATCH, DEC_SEQ, d), 1.0),
        'c': nrm(ks[2], (DEC_BATCH, d), 1.0),
        'state_ret_fwd': nrm(ks[3], (DEC_BATCH, N_RET_LAYERS, RET_HEADS, RET_DK, RET_DV), 0.5),
        'state_ret_bwd': nrm(ks[4], (DEC_BATCH, N_RET_LAYERS, RET_HEADS, RET_DK, RET_DV), 0.5),
        'c_ctx': nrm(ks[5], (d,), 1.0),
        'w_ada': nrm(ks[6], (DEPTH, d, 6 * d), 0.5 * d ** -0.5),
        'b_ada': nrm(ks[7], (DEPTH, 6 * d), 0.02),
        'g_norm1': 1.0 + nrm(ks[8], (DEPTH, d), 0.02),
        'g_norm2': 1.0 + nrm(ks[9], (DEPTH, d), 0.02),
        'ret_w_in': nrm(ks[10], (N_RET_LAYERS, d, 2 * RET_HK + 2 * RET_HV), d ** -0.5),
        'ret_decay': decay_init[None, None, :] + nrm(ks[11], (N_RET_LAYERS, 2, RET_HEADS), 0.1),
        'ret_w_out': nrm(ks[12], (N_RET_LAYERS, RET_HV, d), RET_HV ** -0.5),
        'conv_w_pw1': nrm(ks[13], (N_CONV_LAYERS, d, 2 * d), d ** -0.5),
        'conv_b_pw1': nrm(ks[14], (N_CONV_LAYERS, 2 * d), 0.02),
        'conv_w_dw': nrm(ks[15], (N_CONV_LAYERS, CONV_WIDTH, d), CONV_WIDTH ** -0.5),
        'conv_b_dw': nrm(ks[16], (N_CONV_LAYERS, d), 0.02),
        'conv_ln_g': 1.0 + nrm(ks[17], (N_CONV_LAYERS, d), 0.02),
        'conv_ln_b': nrm(ks[18], (N_CONV_LAYERS, d), 0.02),
        'conv_w_pw2': nrm(ks[19], (N_CONV_LAYERS, d, d), d ** -0.5),
        'conv_b_pw2': nrm(ks[20], (N_CONV_LAYERS, d), 0.02),
        'moe_w_router': nrm(ks[21], (DEPTH, d, N_EXPERTS), d ** -0.5),
        'moe_b_router': nrm(ks[22], (DEPTH, N_EXPERTS), 0.01),
        'moe_w_up': nrm(ks[23], (DEPTH, N_EXPERTS, d, 2 * D_EXPERT), d ** -0.5),
        'moe_b_up': nrm(ks[24], (DEPTH, N_EXPERTS, 2 * D_EXPERT), 0.02),
        'moe_w_down': nrm(ks[25], (DEPTH, N_EXPERTS, D_EXPERT, d), D_EXPERT ** -0.5),
        'moe_b_down': nrm(ks[26], (DEPTH, N_EXPERTS, d), 0.02),
        'g_final': 1.0 + nrm(ks[27], (d,), 0.02),
    }


def reference(x_prompt, x_sample, c, state_ret_fwd, state_ret_bwd, c_ctx, w_ada, b_ada, g_norm1, g_norm2,
              ret_w_in, ret_decay, ret_w_out, conv_w_pw1, conv_b_pw1, conv_w_dw, conv_b_dw, conv_ln_g,
              conv_ln_b, conv_w_pw2, conv_b_pw2, moe_w_router, moe_b_router, moe_w_up, moe_b_up,
              moe_w_down, moe_b_down, g_final):
    y_prompt, new_state_ret_fwd, new_state_ret_bwd = trunk(
        x_prompt, c_ctx[None, :], False, None, None, w_ada, b_ada, g_norm1, g_norm2, ret_w_in, ret_decay,
        ret_w_out, conv_w_pw1, conv_b_pw1, conv_w_dw, conv_b_dw, conv_ln_g, conv_ln_b, conv_w_pw2, conv_b_pw2,
        moe_w_router, moe_b_router, moe_w_up, moe_b_up, moe_w_down, moe_b_down, g_final)
    y_sample, _, _ = trunk(
        x_sample, c, True, state_ret_fwd, state_ret_bwd, w_ada, b_ada, g_norm1, g_norm2, ret_w_in, ret_decay,
        ret_w_out, conv_w_pw1, conv_b_pw1, conv_w_dw, conv_b_dw, conv_ln_g, conv_ln_b, conv_w_pw2, conv_b_pw2,
        moe_w_router, moe_b_router, moe_w_up, moe_b_up, moe_w_down, moe_b_down, g_final)
    return (y_prompt, y_sample, new_state_ret_fwd, new_state_ret_bwd)
```

```python
import functools

import numpy as np
import jax
import jax.numpy as jnp
from jax import lax
from jax.experimental import pallas as pl
from jax.experimental.pallas import tpu as pltpu

F32 = jnp.float32
BF16 = jnp.bfloat16

D = 1024
N_PROMPT, L_PROMPT = 32, 256
N_SAMPLE, L_SAMPLE = 2, 4096
T_PROMPT = N_PROMPT * L_PROMPT
T_SAMPLE = N_SAMPLE * L_SAMPLE
T = T_PROMPT + T_SAMPLE
GRID_W = 64
HEADS, DK, DV = 4, 256, 512
HK, HV = HEADS * DK, HEADS * DV
N_PROJ = 2 * HK + 2 * HV
ROPE_BASE = 10000.0
CONV_W = 31
CONV_PAD = 16
CONV_CHUNK = 64
N_EXP, TOP_K, D_EXP = 32, 4, 1024
SWIGLU_LIMIT, SWIGLU_ALPHA = 7.0, 1.702
EPS = 1e-6

LANES = 128
SUBLANES = 8
ROW_CHUNKS = D // LANES
VMEM_LIMIT = 56 * 1024 * 1024

TM = 512
MOE_TILE = 2048
MOE_BLK = 128
MOE_NBLK = MOE_TILE * TOP_K // MOE_BLK + N_EXP
N_MOE_TILES = T // MOE_TILE
RET_CHUNK = 256

NT_DIMS = (((1,), (1,)), ((), ()))
TN_DIMS = (((0,), (0,)), ((), ()))


def _cparams(sem):
    return pltpu.CompilerParams(dimension_semantics=sem, vmem_limit_bytes=VMEM_LIMIT)


def _cond_of_tile(i, tm):
    return jnp.maximum((i * tm) // L_SAMPLE - 1, 0)


def _sigmoid(x):
    return 1.0 / (1.0 + jnp.exp(-x))


def _rms(x):
    return x * lax.rsqrt(jnp.mean(x * x, axis=-1, keepdims=True) + EPS)


def _pick_x(i, tm, xp_ref, xs_ref):
    return jnp.where(i < T_PROMPT // tm, xp_ref[...], xs_ref[...])


def _x_specs(tm):
    n_p = T_PROMPT // tm
    return [pl.BlockSpec((tm, D), lambda i, *_: (jnp.minimum(i, n_p - 1), 0)),
            pl.BlockSpec((tm, D), lambda i, *_: (jnp.maximum(i - n_p, 0), 0))]


def _ada_kernel(ct_ref, w_ref, b_ref, o_ref):
    ct = ct_ref[...]
    s = ct * _sigmoid(ct)
    w = w_ref[0]
    rows = [jnp.sum(w * s[:, r:r + 1], axis=0, keepdims=True) for r in range(3)]
    rows.append(jnp.zeros((SUBLANES - 3, w.shape[1]), F32))
    o_ref[0] = jnp.concatenate(rows, axis=0) + b_ref[0]


def _ada(ct, w_ada, b_ada):
    depth, _, n = w_ada.shape
    tn = 1536
    return pl.pallas_call(
        _ada_kernel,
        grid=(depth, n // tn),
        in_specs=[pl.BlockSpec((D, SUBLANES), lambda l, j: (0, 0)),
                  pl.BlockSpec((1, D, tn), lambda l, j: (l, 0, j)),
                  pl.BlockSpec((1, 1, tn), lambda l, j: (l, 0, j))],
        out_specs=pl.BlockSpec((1, SUBLANES, tn), lambda l, j: (l, 0, j)),
        out_shape=jax.ShapeDtypeStruct((depth, SUBLANES, n), F32),
        compiler_params=_cparams(("arbitrary", "arbitrary")),
        name="ada",
    )(ct, w_ada, b_ada.reshape(depth, 1, n))


def _rope_tables():
    t = np.arange(L_SAMPLE)
    row = (t // GRID_W).astype(np.float32)
    col = (t % GRID_W).astype(np.float32)
    nf = DK // 4
    inv = (np.float32(ROPE_BASE) ** (-np.arange(nf, dtype=np.float32) / np.float32(nf))).astype(np.float32)
    cos, sin = [], []
    for pos in (row, col):
        ang = (pos[:, None] * inv[None, :]).astype(np.float32)
        c, s = np.cos(ang).astype(np.float32), np.sin(ang).astype(np.float32)
        cos += [c, c]
        sin += [-s, s]
    return np.concatenate(cos, axis=1), np.concatenate(sin, axis=1)


def _proj_kernel(xp_ref, xs_ref, g_ref, mod_ref, w_ref, cos_ref, sin_ref, o_ref, h_scr):
    i, j = pl.program_id(0), pl.program_id(1)

    @pl.when(j == 0)
    def _():
        m = mod_ref[0, 0]
        h = _rms(_pick_x(i, TM, xp_ref, xs_ref)) * g_ref[...]
        h_scr[...] = (h * (1.0 + m[1:2]) + m[0:1]).astype(BF16)

    acc = jnp.dot(h_scr[...], w_ref[...], preferred_element_type=F32)
    acc = acc * jnp.where(j == 1, DK ** -0.5, 1.0)
    rope = jnp.logical_and(i >= T_PROMPT // TM, j < 2)

    @pl.when(rope)
    def _():
        for c in range(D // LANES):
            a = acc[:, c * LANES:(c + 1) * LANES]
            p = (c % 2) * LANES
            r = a * cos_ref[:, p:p + LANES] + pltpu.roll(a, LANES // 2, 1) * sin_ref[:, p:p + LANES]
            o_ref[:, c * LANES:(c + 1) * LANES] = r.astype(BF16)

    @pl.when(jnp.logical_not(rope))
    def _():
        o_ref[...] = acc.astype(BF16)


def _proj(xp, xs, g1, mod4, w_in):
    cos, sin = _rope_tables()
    n_p = T_PROMPT // TM
    n_s = L_SAMPLE // TM
    tab_spec = pl.BlockSpec((TM, DK), lambda i, j: (jnp.maximum(i - n_p, 0) % n_s, 0))
    return pl.pallas_call(
        _proj_kernel,
        grid=(T // TM, N_PROJ // D),
        in_specs=_x_specs(TM) + [
            pl.BlockSpec((1, D), lambda i, j: (0, 0)),
            pl.BlockSpec((1, 1, 6, D), lambda i, j: (0, _cond_of_tile(i, TM), 0, 0)),
            pl.BlockSpec((D, D), lambda i, j: (0, j)),
            tab_spec, tab_spec],
        out_specs=pl.BlockSpec((TM, D), lambda i, j: (i, j)),
        out_shape=jax.ShapeDtypeStruct((T, N_PROJ), BF16),
        scratch_shapes=[pltpu.VMEM((TM, D), BF16)],
        compiler_params=_cparams(("arbitrary", "arbitrary")),
        name="proj",
    )(xp, xs, g1, mod4, w_in, jnp.asarray(cos), jnp.asarray(sin))


def _log_decays(dec_ref, head):
    out = []
    for direction in range(2):
        d = jnp.full((1, 1), dec_ref[direction, head], F32)
        out.append(jnp.minimum(d, 0.0) - jnp.log(1.0 + jnp.exp(-jnp.abs(d))))
    return out


def _decay_mask(lgf, lgb, c):
    ii = lax.broadcasted_iota(jnp.int32, (c, c), 0)
    jj = lax.broadcasted_iota(jnp.int32, (c, c), 1)
    diff = (ii - jj).astype(F32)
    fwd = jnp.where(diff >= 0, jnp.exp(lgf * jnp.maximum(diff, 0.0)), 0.0)
    bwd = jnp.where(diff <= 0, jnp.exp(lgb * jnp.maximum(-diff, 0.0)), 0.0)
    return fwd + bwd


def _norm_gate(o, g):
    g = g.astype(F32)
    return (_rms(o) * (g * _sigmoid(g))).astype(BF16)


def _ret_prompt_kernel(dec_ref, q_ref, k_ref, v_ref, g_ref, o_ref, sf_ref, sb_ref):
    c = L_PROMPT
    lgf, lgb = _log_decays(dec_ref, pl.program_id(1))
    q, k, v = q_ref[...], k_ref[...], v_ref[...]
    s = lax.dot_general(q, k, NT_DIMS, preferred_element_type=F32) * _decay_mask(lgf, lgb, c)
    o = jnp.dot(s.astype(BF16), v, preferred_element_type=F32)
    o_ref[...] = _norm_gate(o, g_ref[...])
    pos = lax.broadcasted_iota(jnp.int32, (c, 1), 0).astype(F32)
    kf = k.astype(F32)
    k_fwd = (kf * jnp.exp(lgf * (c - 1.0 - pos))).astype(BF16)
    k_bwd = (kf * jnp.exp(lgb * pos)).astype(BF16)
    sf_ref[0, 0, 0] = lax.dot_general(k_fwd, v, TN_DIMS, preferred_element_type=F32)
    sb_ref[0, 0, 0] = lax.dot_general(k_bwd, v, TN_DIMS, preferred_element_type=F32)


def _ret_specs(seq_len, row0):
    r = row0 // seq_len
    return [pl.BlockSpec((seq_len, DK), lambda b, h: (r + b, h)),
            pl.BlockSpec((seq_len, DK), lambda b, h: (r + b, HK // DK + h)),
            pl.BlockSpec((seq_len, DV), lambda b, h: (r + b, 2 * HK // DV + h)),
            pl.BlockSpec((seq_len, DV), lambda b, h: (r + b, (2 * HK + HV) // DV + h))]


def _ret_prompt(decay, proj):
    state = jax.ShapeDtypeStruct((N_PROMPT, 1, HEADS, DK, DV), F32)
    state_spec = pl.BlockSpec((1, 1, 1, DK, DV), lambda b, h: (b, 0, h, 0, 0))
    return pl.pallas_call(
        _ret_prompt_kernel,
        grid=(N_PROMPT, HEADS),
        in_specs=[pl.BlockSpec(memory_space=pltpu.SMEM)] + _ret_specs(L_PROMPT, 0),
        out_specs=[pl.BlockSpec((L_PROMPT, DV), lambda b, h: (b, h)), state_spec, state_spec],
        out_shape=[jax.ShapeDtypeStruct((T_PROMPT, HV), BF16), state, state],
        compiler_params=_cparams(("arbitrary", "arbitrary")),
        name="ret_prompt",
    )(decay, proj, proj, proj, proj)


def _ret_sample_kernel(dec_ref, q_ref, k_ref, v_ref, g_ref, s0f_ref, s0b_ref, o_ref,
                       ob_scr, s_scr, dm_scr):
    c = RET_CHUNK
    nc = L_SAMPLE // c
    lgf, lgb = _log_decays(dec_ref, pl.program_id(1))
    dm_scr[...] = _decay_mask(lgf, lgb, c)
    pos = lax.broadcasted_iota(jnp.int32, (c, 1), 0).astype(F32)

    def chunk(ci):
        rows = pl.ds(pl.multiple_of(ci * c, c), c)
        return rows, q_ref[rows, :], k_ref[rows, :], v_ref[rows, :]

    def state_update(lg, write_pos, kc, vc):
        kw = (kc.astype(F32) * jnp.exp(lg * write_pos)).astype(BF16)
        s_scr[...] = s_scr[...] * jnp.exp(lg * c) + lax.dot_general(kw, vc, TN_DIMS, preferred_element_type=F32)

    def read_state(lg, read_pos, qc):
        qr = (qc.astype(F32) * jnp.exp(lg * read_pos)).astype(BF16)
        return jnp.dot(qr, s_scr[...].astype(BF16), preferred_element_type=F32)

    s_scr[...] = s0b_ref[0, 0, 0]

    def bwd(step, carry):
        rows, qc, kc, vc = chunk(nc - 1 - step)
        ob_scr[rows, :] = read_state(lgb, c - pos, qc)
        state_update(lgb, pos, kc, vc)
        return carry

    lax.fori_loop(0, nc, bwd, 0)

    s_scr[...] = s0f_ref[0, 0, 0]

    def fwd(ci, carry):
        rows, qc, kc, vc = chunk(ci)
        s = lax.dot_general(qc, kc, NT_DIMS, preferred_element_type=F32) * dm_scr[...]
        o = jnp.dot(s.astype(BF16), vc, preferred_element_type=F32)
        o = o + read_state(lgf, pos + 1.0, qc) + ob_scr[rows, :]
        state_update(lgf, c - 1.0 - pos, kc, vc)
        o_ref[rows, :] = _norm_gate(o, g_ref[rows, :])
        return carry

    lax.fori_loop(0, nc, fwd, 0)


def _ret_sample(decay, proj, s0f, s0b):
    state_spec = pl.BlockSpec((1, 1, 1, DK, DV), lambda b, h: (b, 0, h, 0, 0))
    return pl.pallas_call(
        _ret_sample_kernel,
        grid=(N_SAMPLE, HEADS),
        in_specs=[pl.BlockSpec(memory_space=pltpu.SMEM)] + _ret_specs(L_SAMPLE, T_PROMPT)
        + [state_spec, state_spec],
        out_specs=pl.BlockSpec((L_SAMPLE, DV), lambda b, h: (b, h)),
        out_shape=jax.ShapeDtypeStruct((T_SAMPLE, HV), BF16),
        scratch_shapes=[pltpu.VMEM((L_SAMPLE, DV), F32), pltpu.VMEM((DK, DV), F32),
                        pltpu.VMEM((RET_CHUNK, RET_CHUNK), F32)],
        compiler_params=_cparams(("arbitrary", "arbitrary")),
        name="ret_sample",
    )(decay, proj, proj, proj, proj, s0f, s0b)


def _store_token_major(ref, val, tm):
    for s in range(ROW_CHUNKS):
        ref[pl.ds(s, tm, stride=ROW_CHUNKS), :] = val[:, s * LANES:(s + 1) * LANES]


def _load_token_major(ref, tm, lead=()):
    return jnp.concatenate([ref[lead + (pl.ds(s, tm, stride=ROW_CHUNKS), slice(None))]
                            for s in range(ROW_CHUNKS)], axis=1)


def _post_mixer(x, y, m, g2_ref, wr_ref, br_ref, x_out_ref, h_out_ref, ti_ref, tg_ref, tm):
    x1 = x + m[2:3] * y
    h = _rms(x1) * g2_ref[0] * (1.0 + m[4:5]) + m[3:4]
    x_out_ref[...] = x1
    _store_token_major(h_out_ref, h, tm)
    w = wr_ref[...]
    w_hi = w.astype(BF16)
    w_lo = (w - w_hi.astype(F32)).astype(BF16)
    h_hi = h.astype(BF16)
    h_lo = (h - h_hi.astype(F32)).astype(BF16)
    dot = functools.partial(lax.dot_general, dimension_numbers=NT_DIMS, preferred_element_type=F32)
    cur = dot(w_hi, h_hi) + dot(w_hi, h_lo) + dot(w_lo, h_hi) + br_ref[...]
    ie = lax.broadcasted_iota(jnp.int32, (N_EXP, tm), 0).astype(F32)
    vals, idxs = [], []
    for _ in range(TOP_K):
        top = jnp.max(cur, axis=0, keepdims=True)
        idx = jnp.min(jnp.where(cur == top, ie, float(N_EXP)), axis=0, keepdims=True)
        vals.append(top)
        idxs.append(idx)
        cur = jnp.where(ie == idx, -jnp.inf, cur)
    ex = [jnp.exp(v - vals[0]) for v in vals]
    den = ex[0] + ex[1] + ex[2] + ex[3]
    pad = jnp.zeros((SUBLANES - TOP_K, tm), F32)
    ti_ref[...] = jnp.concatenate(idxs + [pad], axis=0).astype(jnp.int32)
    tg_ref[...] = jnp.concatenate([e / den for e in ex] + [pad], axis=0)


def _post_in_specs(layer):
    return [pl.BlockSpec((1, 1, 6, D), lambda i: (layer, _cond_of_tile(i, TM), 0, 0)),
            pl.BlockSpec((1, 1, D), lambda i: (layer, 0, 0)),
            pl.BlockSpec((1, N_EXP, D), lambda i: (layer, 0, 0)),
            pl.BlockSpec((1, N_EXP, 1), lambda i: (layer, 0, 0))]


_POST_OUT_SPECS = [pl.BlockSpec((TM, D), lambda i: (i, 0)),
                   pl.BlockSpec((TM * ROW_CHUNKS, LANES), lambda i: (i, 0)),
                   pl.BlockSpec((SUBLANES, TM), lambda i: (0, i)),
                   pl.BlockSpec((SUBLANES, TM), lambda i: (0, i))]
_POST_OUT_SHAPES = [jax.ShapeDtypeStruct((T, D), F32),
                    jax.ShapeDtypeStruct((T * ROW_CHUNKS, LANES), F32),
                    jax.ShapeDtypeStruct((SUBLANES, T), jnp.int32),
                    jax.ShapeDtypeStruct((SUBLANES, T), F32)]


def _ret_out_kernel(ogp_ref, ogs_ref, w_ref, xp_ref, xs_ref, mod_ref, g2_ref, wr_ref, br_ref,
                    x_out_ref, h_out_ref, ti_ref, tg_ref):
    i = pl.program_id(0)
    og = jnp.where(i < T_PROMPT // TM, ogp_ref[...], ogs_ref[...])
    y = jnp.dot(og, w_ref[...], preferred_element_type=F32)
    _post_mixer(_pick_x(i, TM, xp_ref, xs_ref), y, mod_ref[0, 0], g2_ref, wr_ref[0], br_ref[0],
                x_out_ref, h_out_ref, ti_ref, tg_ref, TM)


def _ret_out(og_p, og_s, w_out, xp, xs, mod4, g_norm2, wr_t, br_c):
    n_p = T_PROMPT // TM
    return pl.pallas_call(
        _ret_out_kernel,
        grid=(T // TM,),
        in_specs=[pl.BlockSpec((TM, HV), lambda i: (jnp.minimum(i, n_p - 1), 0)),
                  pl.BlockSpec((TM, HV), lambda i: (jnp.maximum(i - n_p, 0), 0)),
                  pl.BlockSpec((HV, D), lambda i: (0, 0))]
        + _x_specs(TM) + _post_in_specs(0),
        out_specs=_POST_OUT_SPECS,
        out_shape=_POST_OUT_SHAPES,
        compiler_params=_cparams(("arbitrary",)),
        name="ret_out",
    )(og_p, og_s, w_out, xp, xs, mod4, g_norm2, wr_t, br_c)


def _route_a_kernel(ti_ref, dest_ref, meta_ref):
    tt, tm = MOE_TILE, MOE_BLK
    ti = ti_ref[...]
    ie = lax.broadcasted_iota(jnp.int32, (N_EXP, tt), 0)
    onehots = [(ie == ti[k:k + 1]).astype(F32) for k in range(TOP_K)]
    oh = onehots[0] + onehots[1] + onehots[2] + onehots[3]
    ch = 512
    upper = (lax.broadcasted_iota(jnp.int32, (ch, ch), 0)
             < lax.broadcasted_iota(jnp.int32, (ch, ch), 1)).astype(BF16)
    carry = jnp.zeros((N_EXP, 1), F32)
    cums = []
    for c in range(tt // ch):
        blk = oh[:, c * ch:(c + 1) * ch]
        cums.append(jnp.dot(blk.astype(BF16), upper, preferred_element_type=F32) + carry)
        carry = carry + jnp.sum(blk, axis=1, keepdims=True)
    cum = jnp.concatenate(cums, axis=1)
    cnt = carry
    nb = jnp.floor((cnt + (tm - 1.0)) * (1.0 / tm))
    lower = (lax.broadcasted_iota(jnp.int32, (N_EXP, N_EXP), 1)
             < lax.broadcasted_iota(jnp.int32, (N_EXP, N_EXP), 0)).astype(BF16)
    offb = jnp.dot(lower, jnp.broadcast_to(nb, (N_EXP, LANES)).astype(BF16),
                   preferred_element_type=F32)[:, :1]
    off = offb * tm
    base = off + cum
    dests = [jnp.sum(onehots[k] * base, axis=0, keepdims=True) for k in range(TOP_K)]
    dests.append(jnp.zeros((SUBLANES - TOP_K, tt), F32))
    dest_ref[0] = jnp.concatenate(dests, axis=0).astype(jnp.int32)
    nused = jnp.sum(nb, axis=0, keepdims=True)
    jl = lax.broadcasted_iota(jnp.int32, (N_EXP, LANES), 1).astype(F32)
    jc = jnp.minimum(jl, nused - 1.0)
    be = jnp.minimum(jnp.sum(((offb + nb) <= jc).astype(F32), axis=0, keepdims=True), N_EXP - 1.0)
    ief = lax.broadcasted_iota(jnp.int32, (N_EXP, LANES), 0).astype(F32)
    end_row = jnp.sum(jnp.where(ief == be, off + cnt, 0.0), axis=0, keepdims=True)
    nvalid = jnp.clip(end_row - jl[:1] * tm, 0.0, float(tm))
    nvalid = jnp.where(jl[:1] < nused, nvalid, 0.0)
    meta = jnp.concatenate([be, nvalid, jnp.zeros((SUBLANES - 2, LANES), F32)], axis=0)
    meta_ref[0] = meta.astype(jnp.int32)


def _route_a(ti):
    return pl.pallas_call(
        _route_a_kernel,
        grid=(N_MOE_TILES,),
        in_specs=[pl.BlockSpec((SUBLANES, MOE_TILE), lambda i: (0, i))],
        out_specs=[pl.BlockSpec((1, SUBLANES, MOE_TILE), lambda i: (i, 0, 0)),
                   pl.BlockSpec((1, SUBLANES, LANES), lambda i: (i, 0, 0))],
        out_shape=[jax.ShapeDtypeStruct((N_MOE_TILES, SUBLANES, MOE_TILE), jnp.int32),
                   jax.ShapeDtypeStruct((N_MOE_TILES, SUBLANES, LANES), jnp.int32)],
        compiler_params=_cparams(("arbitrary",)),
        name="route_a",
    )(ti)


def _route_b_kernel(dest_ref, meta_ref, slot_ref):
    shift = MOE_BLK.bit_length() - 1

    def per_token(t, carry):
        for k in range(TOP_K):
            d = dest_ref[0, k, t]
            slot_ref[d >> shift, d & (MOE_BLK - 1)] = t * TOP_K + k
        return carry

    lax.fori_loop(0, MOE_TILE, per_token, 0)

    def per_block(j, carry):
        def pad(s, c2):
            slot_ref[j, s] = 0
            return c2
        lax.fori_loop(meta_ref[0, 1, j], MOE_BLK, pad, 0)
        return carry

    lax.fori_loop(0, MOE_NBLK, per_block, 0)


def _route_b(dest, meta):
    return pl.pallas_call(
        _route_b_kernel,
        grid=(N_MOE_TILES,),
        in_specs=[pl.BlockSpec((1, SUBLANES, MOE_TILE), lambda i: (i, 0, 0), memory_space=pltpu.SMEM),
                  pl.BlockSpec((1, SUBLANES, LANES), lambda i: (i, 0, 0), memory_space=pltpu.SMEM)],
        out_specs=pl.BlockSpec((MOE_NBLK, MOE_BLK), lambda i: (i, 0), memory_space=pltpu.SMEM),
        out_shape=jax.ShapeDtypeStruct((N_MOE_TILES * MOE_NBLK, MOE_BLK), jnp.int32),
        compiler_params=_cparams(("arbitrary",)),
        name="route_b",
    )(dest, meta)


def _moe_kernel(be_ref, nv_ref, slot_ref, gate_ref, h_ref, wu_ref, bu_ref, wd_ref, bd_ref,
                out_ref, g_scr, y_scr):
    i, j = pl.program_id(0), pl.program_id(1)
    tm = MOE_BLK

    @pl.when(j == 0)
    def _():
        out_ref[...] = jnp.zeros(out_ref.shape, F32)

    nv = nv_ref[i * MOE_NBLK + j]

    @pl.when(nv > 0)
    def _():
        r = j % SUBLANES
        for ii in range(tm):
            tok = slot_ref[r, ii] >> 2
            g_scr[ii * ROW_CHUNKS:(ii + 1) * ROW_CHUNKS, :] = h_ref[
                pl.ds(pl.multiple_of(tok * ROW_CHUNKS, ROW_CHUNKS), ROW_CHUNKS), :]
        a = _load_token_major(g_scr, tm).astype(BF16)
        up = jnp.dot(a, wu_ref[0], preferred_element_type=F32) + bu_ref[0]
        glu = jnp.minimum(up[:, :D_EXP], SWIGLU_LIMIT)
        lin = jnp.clip(up[:, D_EXP:], -SWIGLU_LIMIT, SWIGLU_LIMIT)
        act = glu * _sigmoid(SWIGLU_ALPHA * glu) * (lin + 1.0)
        y = jnp.dot(act.astype(BF16), wd_ref[0], preferred_element_type=F32) + bd_ref[0]
        _store_token_major(y_scr, y, tm)
        batch = 4
        for i0 in range(0, tm, batch):
            pending = []
            for ii in range(i0, i0 + batch):
                code = slot_ref[r, ii]
                tok = code >> 2
                valid = ii < nv
                gate = jnp.where(valid, gate_ref[code & (TOP_K - 1), tok], 0.0)
                row = pl.multiple_of(jnp.where(valid, tok, MOE_TILE) * ROW_CHUNKS, ROW_CHUNKS)
                rows = pl.ds(row, ROW_CHUNKS)
                pending.append((rows, out_ref[0, rows, :]
                                + gate * y_scr[ii * ROW_CHUNKS:(ii + 1) * ROW_CHUNKS, :]))
            for rows, val in pending:
                out_ref[0, rows, :] = val


def _moe(be, nv, slot, gates, h_tm, w_up, b_up, w_down, b_down):
    def blk(i, j, be_ref, nv_ref):
        return be_ref[i * MOE_NBLK + j]

    grid_spec = pltpu.PrefetchScalarGridSpec(
        num_scalar_prefetch=2,
        grid=(N_MOE_TILES, MOE_NBLK),
        in_specs=[
            pl.BlockSpec((SUBLANES, MOE_BLK), lambda i, j, *_: (i * (MOE_NBLK // SUBLANES) + j // SUBLANES, 0),
                         memory_space=pltpu.SMEM),
            pl.BlockSpec((SUBLANES, MOE_TILE), lambda i, j, *_: (0, i), memory_space=pltpu.SMEM),
            pl.BlockSpec((MOE_TILE * ROW_CHUNKS, LANES), lambda i, j, *_: (i, 0)),
            pl.BlockSpec((1, D, 2 * D_EXP), lambda i, j, *s: (blk(i, j, *s), 0, 0)),
            pl.BlockSpec((1, 1, 2 * D_EXP), lambda i, j, *s: (blk(i, j, *s), 0, 0)),
            pl.BlockSpec((1, D_EXP, D), lambda i, j, *s: (blk(i, j, *s), 0, 0)),
            pl.BlockSpec((1, 1, D), lambda i, j, *s: (blk(i, j, *s), 0, 0)),
        ],
        out_specs=pl.BlockSpec((1, (MOE_TILE + 1) * ROW_CHUNKS, LANES), lambda i, j, *_: (i, 0, 0)),
        scratch_shapes=[pltpu.VMEM((MOE_BLK * ROW_CHUNKS, LANES), F32),
                        pltpu.VMEM((MOE_BLK * ROW_CHUNKS, LANES), F32)],
    )
    return pl.pallas_call(
        _moe_kernel,
        grid_spec=grid_spec,
        out_shape=jax.ShapeDtypeStruct((N_MOE_TILES, (MOE_TILE + 1) * ROW_CHUNKS, LANES), F32),
        compiler_params=_cparams(("arbitrary", "arbitrary")),
        name="moe",
    )(be, nv, slot, gates, h_tm, w_up, b_up.reshape(N_EXP, 1, 2 * D_EXP), w_down,
      b_down.reshape(N_EXP, 1, D))


def _moe_layer(ti, tg, h_tm, w_up, b_up, w_down, b_down):
    dest, meta = _route_a(ti)
    slot = _route_b(dest, meta)
    be = meta[:, 0, :MOE_NBLK].reshape(-1)
    nv = meta[:, 1, :MOE_NBLK].reshape(-1)
    return _moe(be, nv, slot, tg, h_tm, w_up, b_up, w_down, b_down)


_MOE_OUT_SPEC = pl.BlockSpec((1, TM * ROW_CHUNKS, LANES),
                             lambda i: (i // (MOE_TILE // TM), i % (MOE_TILE // TM), 0))


def _conv_kernel(x_ref, moe_ref, modp_ref, g1_ref, w1_ref, b1_ref, wdw_ref, bdw_ref, lng_ref, lnb_ref,
                 w2_ref, b2_ref, mod_ref, g2_ref, wr_ref, br_ref,
                 x_out_ref, h_out_ref, ti_ref, tg_ref, pad_scr, conv_scr):
    i = pl.program_id(0)
    mp = modp_ref[0, 0]
    m = mod_ref[0, 0]
    x = x_ref[...] + mp[5:6] * _load_token_major(moe_ref, TM, (0,))
    h = (_rms(x) * g1_ref[0] * (1.0 + m[1:2]) + m[0:1]).astype(BF16)
    ag = jnp.dot(h, w1_ref[0], preferred_element_type=F32) + b1_ref[0]
    u = ag[:, :D] * _sigmoid(ag[:, D:])
    is_prompt = i < T_PROMPT // TM

    def fill(seg):
        pitch = seg + CONV_PAD
        for s in range(TM // seg):
            pad_scr[s * pitch:s * pitch + CONV_PAD, :] = jnp.zeros((CONV_PAD, D), F32)
            pad_scr[s * pitch + CONV_PAD:(s + 1) * pitch, :] = u[s * seg:(s + 1) * seg, :]
        end = (TM // seg) * pitch
        pad_scr[end:end + CONV_PAD, :] = jnp.zeros((CONV_PAD, D), F32)

    @pl.when(is_prompt)
    def _():
        fill(L_PROMPT)

    @pl.when(jnp.logical_not(is_prompt))
    def _():
        fill(GRID_W)

    per_seq = L_PROMPT // CONV_CHUNK
    halo = CONV_CHUNK + 2 * CONV_PAD

    def conv_chunk(c, carry):
        base = jnp.where(is_prompt, (c // per_seq) * (L_PROMPT + CONV_PAD) + (c % per_seq) * CONV_CHUNK,
                         c * (GRID_W + CONV_PAD))
        base = pl.multiple_of(base, SUBLANES)
        out_rows = pl.ds(pl.multiple_of(c * CONV_CHUNK, CONV_CHUNK), CONV_CHUNK)
        for gl in range(D // LANES):
            lanes = slice(gl * LANES, (gl + 1) * LANES)
            blk = pad_scr[pl.ds(base, halo), lanes]
            acc = jnp.zeros((CONV_CHUNK, LANES), F32)
            for tap in range(CONV_W):
                lo = CONV_PAD - CONV_W // 2 + tap
                acc = acc + blk[lo:lo + CONV_CHUNK, :] * wdw_ref[0, tap:tap + 1, lanes]
            conv_scr[out_rows, lanes] = acc + bdw_ref[0, :, lanes]
        return carry

    lax.fori_loop(0, TM // CONV_CHUNK, conv_chunk, 0)
    uc = conv_scr[...]
    mu = jnp.mean(uc, axis=-1, keepdims=True)
    var = jnp.mean(jnp.square(uc - mu), axis=-1, keepdims=True)
    z = (uc - mu) * lax.rsqrt(var + EPS) * lng_ref[0] + lnb_ref[0]
    z = (z * _sigmoid(z)).astype(BF16)
    y = jnp.dot(z, w2_ref[0], preferred_element_type=F32) + b2_ref[0]
    _post_mixer(x, y, m, g2_ref, wr_ref[0], br_ref[0], x_out_ref, h_out_ref, ti_ref, tg_ref, TM)


def _conv_layer(x1, moe0, mod4, g_norm1, w_pw1, b_pw1, w_dw, b_dw, ln_g, ln_b, w_pw2, b_pw2,
                g_norm2, wr_t, br_c):
    def full(shape):
        return pl.BlockSpec(shape, lambda i: (0,) * len(shape))

    pad_rows = (TM // GRID_W) * (GRID_W + CONV_PAD) + CONV_PAD
    return pl.pallas_call(
        _conv_kernel,
        grid=(T // TM,),
        in_specs=[pl.BlockSpec((TM, D), lambda i: (i, 0)),
                  _MOE_OUT_SPEC,
                  pl.BlockSpec((1, 1, 6, D), lambda i: (0, _cond_of_tile(i, TM), 0, 0)),
                  pl.BlockSpec((1, 1, D), lambda i: (1, 0, 0)),
                  full((1, D, 2 * D)), full((1, 1, 2 * D)), full((1, CONV_W, D)), full((1, 1, D)),
                  full((1, 1, D)), full((1, 1, D)), full((1, D, D)), full((1, 1, D))]
        + _post_in_specs(1),
        out_specs=_POST_OUT_SPECS,
        out_shape=_POST_OUT_SHAPES,
        scratch_shapes=[pltpu.VMEM((pad_rows, D), F32), pltpu.VMEM((TM, D), F32)],
        compiler_params=_cparams(("arbitrary",)),
        name="conv",
    )(x1, moe0, mod4, g_norm1, w_pw1, b_pw1.reshape(1, 1, 2 * D), w_dw, b_dw.reshape(1, 1, D),
      ln_g.reshape(1, 1, D), ln_b.reshape(1, 1, D), w_pw2, b_pw2.reshape(1, 1, D),
      mod4, g_norm2, wr_t, br_c)


def _final_kernel(x_ref, moe_ref, mod_ref, g_ref, yp_ref, ys_ref):
    i = pl.program_id(0)
    x = x_ref[...] + mod_ref[0, 0][5:6] * _load_token_major(moe_ref, TM, (0,))
    y = _rms(x) * g_ref[...]

    @pl.when(i < T_PROMPT // TM)
    def _():
        yp_ref[...] = y

    @pl.when(i >= T_PROMPT // TM)
    def _():
        ys_ref[...] = y


def _final(x, moe1, mod4, g_final):
    n_p = T_PROMPT // TM
    return pl.pallas_call(
        _final_kernel,
        grid=(T // TM,),
        in_specs=[pl.BlockSpec((TM, D), lambda i: (i, 0)),
                  _MOE_OUT_SPEC,
                  pl.BlockSpec((1, 1, 6, D), lambda i: (1, _cond_of_tile(i, TM), 0, 0)),
                  pl.BlockSpec((1, D), lambda i: (0, 0))],
        out_specs=[pl.BlockSpec((TM, D), lambda i: (jnp.minimum(i, n_p - 1), 0)),
                   pl.BlockSpec((TM, D), lambda i: (jnp.maximum(i - n_p, 0), 0))],
        out_shape=[jax.ShapeDtypeStruct((T_PROMPT, D), F32), jax.ShapeDtypeStruct((T_SAMPLE, D), F32)],
        compiler_params=_cparams(("arbitrary",)),
        name="final",
    )(x, moe1, mod4, g_final)


def kernel(x_prompt, x_sample, c, state_ret_fwd, state_ret_bwd, c_ctx, w_ada, b_ada, g_norm1, g_norm2,
           ret_w_in, ret_decay, ret_w_out, conv_w_pw1, conv_b_pw1, conv_w_dw, conv_b_dw, conv_ln_g,
           conv_ln_b, conv_w_pw2, conv_b_pw2, moe_w_router, moe_b_router, moe_w_up, moe_b_up,
           moe_w_down, moe_b_down, g_final):
    xp = x_prompt.reshape(T_PROMPT, D)
    xs = x_sample.reshape(T_SAMPLE, D)
    cond = jnp.concatenate([c_ctx[None, :], c, jnp.zeros((SUBLANES - 1 - N_SAMPLE, D), F32)], axis=0)
    mod = _ada(cond.T, w_ada, b_ada)
    mod4 = mod.reshape(mod.shape[0], SUBLANES, 6, D)
    wr_t = jnp.swapaxes(moe_w_router, 1, 2)
    br_c = moe_b_router[:, :, None]
    gn1 = g_norm1[:, None, :]
    gn2 = g_norm2[:, None, :]

    proj = _proj(xp, xs, g_norm1[0:1], mod4, ret_w_in[0].astype(BF16))
    og_p, new_f, new_b = _ret_prompt(ret_decay[0], proj)
    og_s = _ret_sample(ret_decay[0], proj, state_ret_fwd, state_ret_bwd)
    x1, h_tm, ti, tg = _ret_out(og_p, og_s, ret_w_out[0].astype(BF16), xp, xs, mod4, gn2, wr_t, br_c)
    moe0 = _moe_layer(ti, tg, h_tm, moe_w_up[0].astype(BF16), moe_b_up[0],
                      moe_w_down[0].astype(BF16), moe_b_down[0])

    x2, h_tm, ti, tg = _conv_layer(x1, moe0, mod4, gn1, conv_w_pw1.astype(BF16), conv_b_pw1, conv_w_dw,
                                   conv_b_dw, conv_ln_g, conv_ln_b, conv_w_pw2.astype(BF16), conv_b_pw2,
                                   gn2, wr_t, br_c)
    moe1 = _moe_layer(ti, tg, h_tm, moe_w_up[1].astype(BF16), moe_b_up[1],
                      moe_w_down[1].astype(BF16), moe_b_down[1])

    y_p, y_s = _final(x2, moe1, mod4, g_final[None, :])
    return (y_p.reshape(N_PROMPT, L_PROMPT, D), y_s.reshape(N_SAMPLE, L_SAMPLE, D), new_f, new_b)
```

```python
import functools

import numpy as np
import jax
import jax.numpy as jnp
from jax import lax
from jax.experimental import pallas as pl
from jax.experimental.pallas import tpu as pltpu

F32 = jnp.float32
BF16 = jnp.bfloat16

D = 1024
N_PROMPT, L_PROMPT = 32, 256
N_SAMPLE, L_SAMPLE = 2, 4096
T_PROMPT = N_PROMPT * L_PROMPT
T_SAMPLE = N_SAMPLE * L_SAMPLE
T = T_PROMPT + T_SAMPLE
GRID_W = 64
HEADS, DK, DV = 4, 256, 512
HK, HV = HEADS * DK, HEADS * DV
N_PROJ = 2 * HK + 2 * HV
ROPE_BASE = 10000.0
CONV_W = 31
CONV_PAD = 16
CONV_CHUNK = 64
N_EXP, TOP_K, D_EXP = 32, 4, 1024
SWIGLU_LIMIT, SWIGLU_ALPHA = 7.0, 1.702
EPS = 1e-6

LANES = 128
SUBLANES = 8
ROW_CHUNKS = D // LANES
VMEM_LIMIT = 56 * 1024 * 1024

TM = 512
MOE_TILE = 2048
MOE_BLK = 128
MOE_NBLK = MOE_TILE * TOP_K // MOE_BLK + N_EXP
MOE_SLOTS = MOE_NBLK * MOE_BLK
N_MOE_TILES = T // MOE_TILE
RET_CHUNK = 256

NT_DIMS = (((1,), (1,)), ((), ()))
TN_DIMS = (((0,), (0,)), ((), ()))


def _cparams(sem):
    return pltpu.CompilerParams(dimension_semantics=sem, vmem_limit_bytes=VMEM_LIMIT)


def _cond_of_tile(i, tm):
    return jnp.maximum((i * tm) // L_SAMPLE - 1, 0)


def _sigmoid(x):
    return 1.0 / (1.0 + jnp.exp(-x))


def _rms(x):
    return x * lax.rsqrt(jnp.mean(x * x, axis=-1, keepdims=True) + EPS)


def _pick_x(i, tm, xp_ref, xs_ref):
    return jnp.where(i < T_PROMPT // tm, xp_ref[...], xs_ref[...])


def _x_specs(tm):
    n_p = T_PROMPT // tm
    return [pl.BlockSpec((tm, D), lambda i, *_: (jnp.minimum(i, n_p - 1), 0)),
            pl.BlockSpec((tm, D), lambda i, *_: (jnp.maximum(i - n_p, 0), 0))]


def _ada_kernel(ct_ref, w_ref, b_ref, o_ref):
    ct = ct_ref[...]
    s = ct * _sigmoid(ct)
    w = w_ref[0]
    rows = [jnp.sum(w * s[:, r:r + 1], axis=0, keepdims=True) for r in range(3)]
    rows.append(jnp.zeros((SUBLANES - 3, w.shape[1]), F32))
    o_ref[0] = jnp.concatenate(rows, axis=0) + b_ref[0]


def _ada(ct, w_ada, b_ada):
    depth, _, n = w_ada.shape
    tn = 1536
    return pl.pallas_call(
        _ada_kernel,
        grid=(depth, n // tn),
        in_specs=[pl.BlockSpec((D, SUBLANES), lambda l, j: (0, 0)),
                  pl.BlockSpec((1, D, tn), lambda l, j: (l, 0, j)),
                  pl.BlockSpec((1, 1, tn), lambda l, j: (l, 0, j))],
        out_specs=pl.BlockSpec((1, SUBLANES, tn), lambda l, j: (l, 0, j)),
        out_shape=jax.ShapeDtypeStruct((depth, SUBLANES, n), F32),
        compiler_params=_cparams(("arbitrary", "arbitrary")),
        name="ada",
    )(ct, w_ada, b_ada.reshape(depth, 1, n))


def _rope_tables():
    t = np.arange(L_SAMPLE)
    row = (t // GRID_W).astype(np.float32)
    col = (t % GRID_W).astype(np.float32)
    nf = DK // 4
    inv = (np.float32(ROPE_BASE) ** (-np.arange(nf, dtype=np.float32) / np.float32(nf))).astype(np.float32)
    cos, sin = [], []
    for pos in (row, col):
        ang = (pos[:, None] * inv[None, :]).astype(np.float32)
        c, s = np.cos(ang).astype(np.float32), np.sin(ang).astype(np.float32)
        cos += [c, c]
        sin += [-s, s]
    return np.concatenate(cos, axis=1), np.concatenate(sin, axis=1)


def _proj_kernel(xp_ref, xs_ref, g_ref, mod_ref, w_ref, cos_ref, sin_ref, o_ref, h_scr):
    i, j = pl.program_id(0), pl.program_id(1)

    @pl.when(j == 0)
    def _():
        m = mod_ref[0, 0]
        h = _rms(_pick_x(i, TM, xp_ref, xs_ref)) * g_ref[...]
        h_scr[...] = (h * (1.0 + m[1:2]) + m[0:1]).astype(BF16)

    acc = jnp.dot(h_scr[...], w_ref[...], preferred_element_type=F32)
    acc = acc * jnp.where(j == 1, DK ** -0.5, 1.0)
    rope = jnp.logical_and(i >= T_PROMPT // TM, j < 2)

    @pl.when(rope)
    def _():
        for c in range(D // LANES):
            a = acc[:, c * LANES:(c + 1) * LANES]
            p = (c % 2) * LANES
            r = a * cos_ref[:, p:p + LANES] + pltpu.roll(a, LANES // 2, 1) * sin_ref[:, p:p + LANES]
            o_ref[:, c * LANES:(c + 1) * LANES] = r.astype(BF16)

    @pl.when(jnp.logical_not(rope))
    def _():
        o_ref[...] = acc.astype(BF16)


def _proj(xp, xs, g1, mod4, w_in):
    cos, sin = _rope_tables()
    n_p = T_PROMPT // TM
    n_s = L_SAMPLE // TM
    tab_spec = pl.BlockSpec((TM, DK), lambda i, j: (jnp.maximum(i - n_p, 0) % n_s, 0))
    return pl.pallas_call(
        _proj_kernel,
        grid=(T // TM, N_PROJ // D),
        in_specs=_x_specs(TM) + [
            pl.BlockSpec((1, D), lambda i, j: (0, 0)),
            pl.BlockSpec((1, 1, 6, D), lambda i, j: (0, _cond_of_tile(i, TM), 0, 0)),
            pl.BlockSpec((D, D), lambda i, j: (0, j)),
            tab_spec, tab_spec],
        out_specs=pl.BlockSpec((TM, D), lambda i, j: (i, j)),
        out_shape=jax.ShapeDtypeStruct((T, N_PROJ), BF16),
        scratch_shapes=[pltpu.VMEM((TM, D), BF16)],
        compiler_params=_cparams(("arbitrary", "arbitrary")),
        name="proj",
    )(xp, xs, g1, mod4, w_in, jnp.asarray(cos), jnp.asarray(sin))


def _log_decays(dec_ref, head):
    out = []
    for direction in range(2):
        d = jnp.full((1, 1), dec_ref[direction, head], F32)
        out.append(jnp.minimum(d, 0.0) - jnp.log(1.0 + jnp.exp(-jnp.abs(d))))
    return out


def _decay_mask(lgf, lgb, c):
    ii = lax.broadcasted_iota(jnp.int32, (c, c), 0)
    jj = lax.broadcasted_iota(jnp.int32, (c, c), 1)
    diff = (ii - jj).astype(F32)
    fwd = jnp.where(diff >= 0, jnp.exp(lgf * jnp.maximum(diff, 0.0)), 0.0)
    bwd = jnp.where(diff <= 0, jnp.exp(lgb * jnp.maximum(-diff, 0.0)), 0.0)
    return fwd + bwd


def _norm_gate(o, g):
    g = g.astype(F32)
    return (_rms(o) * (g * _sigmoid(g))).astype(BF16)


def _ret_prompt_kernel(dec_ref, q_ref, k_ref, v_ref, g_ref, o_ref, sf_ref, sb_ref):
    c = L_PROMPT
    lgf, lgb = _log_decays(dec_ref, pl.program_id(1))
    q, k, v = q_ref[...], k_ref[...], v_ref[...]
    s = lax.dot_general(q, k, NT_DIMS, preferred_element_type=F32) * _decay_mask(lgf, lgb, c)
    o = jnp.dot(s.astype(BF16), v, preferred_element_type=F32)
    o_ref[...] = _norm_gate(o, g_ref[...])
    pos = lax.broadcasted_iota(jnp.int32, (c, 1), 0).astype(F32)
    kf = k.astype(F32)
    k_fwd = (kf * jnp.exp(lgf * (c - 1.0 - pos))).astype(BF16)
    k_bwd = (kf * jnp.exp(lgb * pos)).astype(BF16)
    sf_ref[0, 0, 0] = lax.dot_general(k_fwd, v, TN_DIMS, preferred_element_type=F32)
    sb_ref[0, 0, 0] = lax.dot_general(k_bwd, v, TN_DIMS, preferred_element_type=F32)


def _ret_specs(seq_len, row0):
    r = row0 // seq_len
    return [pl.BlockSpec((seq_len, DK), lambda b, h: (r + b, h)),
            pl.BlockSpec((seq_len, DK), lambda b, h: (r + b, HK // DK + h)),
            pl.BlockSpec((seq_len, DV), lambda b, h: (r + b, 2 * HK // DV + h)),
            pl.BlockSpec((seq_len, DV), lambda b, h: (r + b, (2 * HK + HV) // DV + h))]


def _ret_prompt(decay, proj):
    state = jax.ShapeDtypeStruct((N_PROMPT, 1, HEADS, DK, DV), F32)
    state_spec = pl.BlockSpec((1, 1, 1, DK, DV), lambda b, h: (b, 0, h, 0, 0))
    return pl.pallas_call(
        _ret_prompt_kernel,
        grid=(N_PROMPT, HEADS),
        in_specs=[pl.BlockSpec(memory_space=pltpu.SMEM)] + _ret_specs(L_PROMPT, 0),
        out_specs=[pl.BlockSpec((L_PROMPT, DV), lambda b, h: (b, h)), state_spec, state_spec],
        out_shape=[jax.ShapeDtypeStruct((T_PROMPT, HV), BF16), state, state],
        compiler_params=_cparams(("arbitrary", "arbitrary")),
        name="ret_prompt",
    )(decay, proj, proj, proj, proj)


def _ret_sample_kernel(dec_ref, q_ref, k_ref, v_ref, g_ref, s0f_ref, s0b_ref, o_ref,
                       ob_scr, s_scr, dm_scr):
    c = RET_CHUNK
    nc = L_SAMPLE // c
    lgf, lgb = _log_decays(dec_ref, pl.program_id(1))
    dm_scr[...] = _decay_mask(lgf, lgb, c)
    pos = lax.broadcasted_iota(jnp.int32, (c, 1), 0).astype(F32)

    def chunk(ci):
        rows = pl.ds(pl.multiple_of(ci * c, c), c)
        return rows, q_ref[rows, :], k_ref[rows, :], v_ref[rows, :]

    def state_update(lg, write_pos, kc, vc):
        kw = (kc.astype(F32) * jnp.exp(lg * write_pos)).astype(BF16)
        s_scr[...] = s_scr[...] * jnp.exp(lg * c) + lax.dot_general(kw, vc, TN_DIMS, preferred_element_type=F32)

    def read_state(lg, read_pos, qc):
        qr = (qc.astype(F32) * jnp.exp(lg * read_pos)).astype(BF16)
        return jnp.dot(qr, s_scr[...].astype(BF16), preferred_element_type=F32)

    s_scr[...] = s0b_ref[0, 0, 0]

    def bwd(step, carry):
        rows, qc, kc, vc = chunk(nc - 1 - step)
        ob_scr[rows, :] = read_state(lgb, c - pos, qc)
        state_update(lgb, pos, kc, vc)
        return carry

    lax.fori_loop(0, nc, bwd, 0)

    s_scr[...] = s0f_ref[0, 0, 0]

    def fwd(ci, carry):
        rows, qc, kc, vc = chunk(ci)
        s = lax.dot_general(qc, kc, NT_DIMS, preferred_element_type=F32) * dm_scr[...]
        o = jnp.dot(s.astype(BF16), vc, preferred_element_type=F32)
        o = o + read_state(lgf, pos + 1.0, qc) + ob_scr[rows, :]
        state_update(lgf, c - 1.0 - pos, kc, vc)
        o_ref[rows, :] = _norm_gate(o, g_ref[rows, :])
        return carry

    lax.fori_loop(0, nc, fwd, 0)


def _ret_sample(decay, proj, s0f, s0b):
    state_spec = pl.BlockSpec((1, 1, 1, DK, DV), lambda b, h: (b, 0, h, 0, 0))
    return pl.pallas_call(
        _ret_sample_kernel,
        grid=(N_SAMPLE, HEADS),
        in_specs=[pl.BlockSpec(memory_space=pltpu.SMEM)] + _ret_specs(L_SAMPLE, T_PROMPT)
        + [state_spec, state_spec],
        out_specs=pl.BlockSpec((L_SAMPLE, DV), lambda b, h: (b, h)),
        out_shape=jax.ShapeDtypeStruct((T_SAMPLE, HV), BF16),
        scratch_shapes=[pltpu.VMEM((L_SAMPLE, DV), F32), pltpu.VMEM((DK, DV), F32),
                        pltpu.VMEM((RET_CHUNK, RET_CHUNK), F32)],
        compiler_params=_cparams(("arbitrary", "arbitrary")),
        name="ret_sample",
    )(decay, proj, proj, proj, proj, s0f, s0b)


def _store_token_major(ref, val, tm):
    for s in range(ROW_CHUNKS):
        ref[pl.ds(s, tm, stride=ROW_CHUNKS), :] = val[:, s * LANES:(s + 1) * LANES]


def _load_token_major(ref, tm, lead=()):
    return jnp.concatenate([ref[lead + (pl.ds(s, tm, stride=ROW_CHUNKS), slice(None))]
                            for s in range(ROW_CHUNKS)], axis=1)


def _post_mixer(x, y, m, g2_ref, wr_ref, br_ref, x_out_ref, h_out_ref, ti_ref, tg_ref, tm):
    x1 = x + m[2:3] * y
    h = _rms(x1) * g2_ref[0] * (1.0 + m[4:5]) + m[3:4]
    x_out_ref[...] = x1
    _store_token_major(h_out_ref, h, tm)
    w = wr_ref[...]
    w_hi = w.astype(BF16)
    w_lo = (w - w_hi.astype(F32)).astype(BF16)
    h_hi = h.astype(BF16)
    h_lo = (h - h_hi.astype(F32)).astype(BF16)
    dot = functools.partial(lax.dot_general, dimension_numbers=NT_DIMS, preferred_element_type=F32)
    cur = dot(w_hi, h_hi) + dot(w_hi, h_lo) + dot(w_lo, h_hi) + br_ref[...]
    ie = lax.broadcasted_iota(jnp.int32, (N_EXP, tm), 0).astype(F32)
    vals, idxs = [], []
    for _ in range(TOP_K):
        top = jnp.max(cur, axis=0, keepdims=True)
        idx = jnp.min(jnp.where(cur == top, ie, float(N_EXP)), axis=0, keepdims=True)
        vals.append(top)
        idxs.append(idx)
        cur = jnp.where(ie == idx, -jnp.inf, cur)
    ex = [jnp.exp(v - vals[0]) for v in vals]
    den = ex[0] + ex[1] + ex[2] + ex[3]
    pad = jnp.zeros((SUBLANES - TOP_K, tm), F32)
    ti_ref[...] = jnp.concatenate(idxs + [pad], axis=0).astype(jnp.int32)
    tg_ref[...] = jnp.concatenate([e / den for e in ex] + [pad], axis=0)


def _post_in_specs(layer):
    return [pl.BlockSpec((1, 1, 6, D), lambda i: (layer, _cond_of_tile(i, TM), 0, 0)),
            pl.BlockSpec((1, 1, D), lambda i: (layer, 0, 0)),
            pl.BlockSpec((1, N_EXP, D), lambda i: (layer, 0, 0)),
            pl.BlockSpec((1, N_EXP, 1), lambda i: (layer, 0, 0))]


_POST_OUT_SPECS = [pl.BlockSpec((TM, D), lambda i: (i, 0)),
                   pl.BlockSpec((TM * ROW_CHUNKS, LANES), lambda i: (i, 0)),
                   pl.BlockSpec((SUBLANES, TM), lambda i: (0, i)),
                   pl.BlockSpec((SUBLANES, TM), lambda i: (0, i))]
_POST_OUT_SHAPES = [jax.ShapeDtypeStruct((T, D), F32),
                    jax.ShapeDtypeStruct((T * ROW_CHUNKS, LANES), F32),
                    jax.ShapeDtypeStruct((SUBLANES, T), jnp.int32),
                    jax.ShapeDtypeStruct((SUBLANES, T), F32)]


def _ret_out_kernel(ogp_ref, ogs_ref, w_ref, xp_ref, xs_ref, mod_ref, g2_ref, wr_ref, br_ref,
                    x_out_ref, h_out_ref, ti_ref, tg_ref):
    i = pl.program_id(0)
    og = jnp.where(i < T_PROMPT // TM, ogp_ref[...], ogs_ref[...])
    y = jnp.dot(og, w_ref[...], preferred_element_type=F32)
    _post_mixer(_pick_x(i, TM, xp_ref, xs_ref), y, mod_ref[0, 0], g2_ref, wr_ref[0], br_ref[0],
                x_out_ref, h_out_ref, ti_ref, tg_ref, TM)


def _ret_out(og_p, og_s, w_out, xp, xs, mod4, g_norm2, wr_t, br_c):
    n_p = T_PROMPT // TM
    return pl.pallas_call(
        _ret_out_kernel,
        grid=(T // TM,),
        in_specs=[pl.BlockSpec((TM, HV), lambda i: (jnp.minimum(i, n_p - 1), 0)),
                  pl.BlockSpec((TM, HV), lambda i: (jnp.maximum(i - n_p, 0), 0)),
                  pl.BlockSpec((HV, D), lambda i: (0, 0))]
        + _x_specs(TM) + _post_in_specs(0),
        out_specs=_POST_OUT_SPECS,
        out_shape=_POST_OUT_SHAPES,
        compiler_params=_cparams(("arbitrary",)),
        name="ret_out",
    )(og_p, og_s, w_out, xp, xs, mod4, g_norm2, wr_t, br_c)


def _route_a_kernel(ti_ref, dest_ref, meta_ref):
    tt, tm = MOE_TILE, MOE_BLK
    ti = ti_ref[...]
    ie = lax.broadcasted_iota(jnp.int32, (N_EXP, tt), 0)
    onehots = [(ie == ti[k:k + 1]).astype(F32) for k in range(TOP_K)]
    oh = onehots[0] + onehots[1] + onehots[2] + onehots[3]
    ch = 512
    upper = (lax.broadcasted_iota(jnp.int32, (ch, ch), 0)
             < lax.broadcasted_iota(jnp.int32, (ch, ch), 1)).astype(BF16)
    carry = jnp.zeros((N_EXP, 1), F32)
    cums = []
    for c in range(tt // ch):
        blk = oh[:, c * ch:(c + 1) * ch]
        cums.append(jnp.dot(blk.astype(BF16), upper, preferred_element_type=F32) + carry)
        carry = carry + jnp.sum(blk, axis=1, keepdims=True)
    cum = jnp.concatenate(cums, axis=1)
    cnt = carry
    nb = jnp.floor((cnt + (tm - 1.0)) * (1.0 / tm))
    lower = (lax.broadcasted_iota(jnp.int32, (N_EXP, N_EXP), 1)
             < lax.broadcasted_iota(jnp.int32, (N_EXP, N_EXP), 0)).astype(BF16)
    offb = jnp.dot(lower, jnp.broadcast_to(nb, (N_EXP, LANES)).astype(BF16),
                   preferred_element_type=F32)[:, :1]
    off = offb * tm
    base = off + cum
    dests = [jnp.sum(onehots[k] * base, axis=0, keepdims=True) for k in range(TOP_K)]
    dests.append(jnp.zeros((SUBLANES - TOP_K, tt), F32))
    dest_ref[0] = jnp.concatenate(dests, axis=0).astype(jnp.int32)
    nused = jnp.sum(nb, axis=0, keepdims=True)
    jl = lax.broadcasted_iota(jnp.int32, (N_EXP, LANES), 1).astype(F32)
    jc = jnp.minimum(jl, nused - 1.0)
    be = jnp.minimum(jnp.sum(((offb + nb) <= jc).astype(F32), axis=0, keepdims=True), N_EXP - 1.0)
    ief = lax.broadcasted_iota(jnp.int32, (N_EXP, LANES), 0).astype(F32)
    end_row = jnp.sum(jnp.where(ief == be, off + cnt, 0.0), axis=0, keepdims=True)
    nvalid = jnp.clip(end_row - jl[:1] * tm, 0.0, float(tm))
    nvalid = jnp.where(jl[:1] < nused, nvalid, 0.0)
    meta = jnp.concatenate([be, nvalid, jnp.broadcast_to(nused, (1, LANES)),
                            jnp.zeros((SUBLANES - 3, LANES), F32)], axis=0)
    meta_ref[0] = meta.astype(jnp.int32)


def _route_a(ti):
    return pl.pallas_call(
        _route_a_kernel,
        grid=(N_MOE_TILES,),
        in_specs=[pl.BlockSpec((SUBLANES, MOE_TILE), lambda i: (0, i))],
        out_specs=[pl.BlockSpec((1, SUBLANES, MOE_TILE), lambda i: (i, 0, 0)),
                   pl.BlockSpec((1, SUBLANES, LANES), lambda i: (i, 0, 0))],
        out_shape=[jax.ShapeDtypeStruct((N_MOE_TILES, SUBLANES, MOE_TILE), jnp.int32),
                   jax.ShapeDtypeStruct((N_MOE_TILES, SUBLANES, LANES), jnp.int32)],
        compiler_params=_cparams(("arbitrary",)),
        name="route_a",
    )(ti)


def _route_b_kernel(dest_ref, meta_ref, slot_ref):
    group = SUBLANES

    def per_block(j, carry):
        def pad(gi, c2):
            base = j * MOE_BLK + gi * group
            for u in range(group):
                slot_ref[base + u] = 0
            return c2
        lax.fori_loop(meta_ref[0, 1, j] // group, MOE_BLK // group, pad, 0)
        return carry

    lax.fori_loop(0, MOE_NBLK, per_block, 0)

    def per_group(tg, carry):
        t0 = tg * group
        loaded = [(k, u, dest_ref[0, k, t0 + u]) for u in range(group) for k in range(TOP_K)]
        for k, u, d in loaded:
            slot_ref[d] = (k * MOE_TILE + t0 + u) * ROW_CHUNKS
        return carry

    lax.fori_loop(0, MOE_TILE // group, per_group, 0)


def _route_b(dest, meta):
    return pl.pallas_call(
        _route_b_kernel,
        grid=(N_MOE_TILES,),
        in_specs=[pl.BlockSpec((1, SUBLANES, MOE_TILE), lambda i: (i, 0, 0), memory_space=pltpu.SMEM),
                  pl.BlockSpec((1, SUBLANES, LANES), lambda i: (i, 0, 0), memory_space=pltpu.SMEM)],
        out_specs=pl.BlockSpec((MOE_SLOTS,), lambda i: (i,), memory_space=pltpu.SMEM),
        out_shape=jax.ShapeDtypeStruct((N_MOE_TILES * MOE_SLOTS,), jnp.int32),
        compiler_params=_cparams(("arbitrary",)),
        name="route_b",
    )(dest, meta)


def _moe_kernel(be_ref, nv_ref, nu_ref, slot_ref, gate_ref, h_ref, wu_ref, bu_ref, wd_ref, bd_ref,
                out_ref, g0_scr, g1_scr, y0_scr, y1_scr):
    i, j = pl.program_id(0), pl.program_id(1)
    tm = MOE_BLK
    nused = nu_ref[i]
    base = i * MOE_NBLK
    row_mask = MOE_TILE * ROW_CHUNKS - 1
    spare_row = MOE_TILE * ROW_CHUNKS

    def row_slice(ii):
        return slice(ii * ROW_CHUNKS, (ii + 1) * ROW_CHUNKS)

    def gather(blk, g_scr):
        for ii in range(tm):
            off = pl.multiple_of(slot_ref[blk * tm + ii] & row_mask, ROW_CHUNKS)
            g_scr[row_slice(ii), :] = h_ref[pl.ds(off, ROW_CHUNKS), :]

    def scatter(blk, n_real, y_scr):
        batch = 4
        for i0 in range(0, tm, batch):
            pending = []
            for ii in range(i0, i0 + batch):
                code = slot_ref[blk * tm + ii]
                gate = gate_ref[code >> 3]
                off = pl.multiple_of(jnp.where(ii < n_real, code & row_mask, spare_row), ROW_CHUNKS)
                rows = pl.ds(off, ROW_CHUNKS)
                pending.append((rows, out_ref[0, rows, :] + gate * y_scr[row_slice(ii), :]))
            for rows, val in pending:
                out_ref[0, rows, :] = val

    def experts(g_scr, y_scr):
        a = _load_token_major(g_scr, tm).astype(BF16)
        up = jnp.dot(a, wu_ref[0], preferred_element_type=F32) + bu_ref[0]
        glu = jnp.minimum(up[:, :D_EXP], SWIGLU_LIMIT)
        lin = jnp.clip(up[:, D_EXP:], -SWIGLU_LIMIT, SWIGLU_LIMIT)
        act = glu * _sigmoid(SWIGLU_ALPHA * glu) * (lin + 1.0)
        y = jnp.dot(act.astype(BF16), wd_ref[0], preferred_element_type=F32) + bd_ref[0]
        _store_token_major(y_scr, y, tm)

    @pl.when(j == 0)
    def _():
        out_ref[...] = jnp.zeros(out_ref.shape, F32)
        y1_scr[...] = jnp.zeros(y1_scr.shape, F32)
        gather(0, g0_scr)

    def step(g_cur, g_nxt, y_cur, y_prv):
        jp = jnp.maximum(j - 1, 0)
        scatter(jp, jnp.where(j > 0, nv_ref[base + jp], 0), y_prv)
        experts(g_cur, y_cur)
        gather(jnp.minimum(j + 1, nused - 1), g_nxt)

        @pl.when(j == nused - 1)
        def _():
            scatter(j, nv_ref[base + j], y_cur)

    live = j < nused

    @pl.when(jnp.logical_and(live, j % 2 == 0))
    def _():
        step(g0_scr, g1_scr, y0_scr, y1_scr)

    @pl.when(jnp.logical_and(live, j % 2 == 1))
    def _():
        step(g1_scr, g0_scr, y1_scr, y0_scr)


def _moe(be, nv, nu, slot, gates, h_tm, w_up, b_up, w_down, b_down):
    def blk(i, j, be_ref, nv_ref, nu_ref):
        return be_ref[i * MOE_NBLK + j]

    row_buf = pltpu.VMEM((MOE_BLK * ROW_CHUNKS, LANES), F32)
    grid_spec = pltpu.PrefetchScalarGridSpec(
        num_scalar_prefetch=3,
        grid=(N_MOE_TILES, MOE_NBLK),
        in_specs=[
            pl.BlockSpec((MOE_SLOTS,), lambda i, j, *_: (i,), memory_space=pltpu.SMEM),
            pl.BlockSpec((TOP_K * MOE_TILE,), lambda i, j, *_: (i,), memory_space=pltpu.SMEM),
            pl.BlockSpec((MOE_TILE * ROW_CHUNKS, LANES), lambda i, j, *_: (i, 0)),
            pl.BlockSpec((1, D, 2 * D_EXP), lambda i, j, *s: (blk(i, j, *s), 0, 0)),
            pl.BlockSpec((1, 1, 2 * D_EXP), lambda i, j, *s: (blk(i, j, *s), 0, 0)),
            pl.BlockSpec((1, D_EXP, D), lambda i, j, *s: (blk(i, j, *s), 0, 0)),
            pl.BlockSpec((1, 1, D), lambda i, j, *s: (blk(i, j, *s), 0, 0)),
        ],
        out_specs=pl.BlockSpec((1, (MOE_TILE + 1) * ROW_CHUNKS, LANES), lambda i, j, *_: (i, 0, 0)),
        scratch_shapes=[row_buf, row_buf, row_buf, row_buf],
    )
    return pl.pallas_call(
        _moe_kernel,
        grid_spec=grid_spec,
        out_shape=jax.ShapeDtypeStruct((N_MOE_TILES, (MOE_TILE + 1) * ROW_CHUNKS, LANES), F32),
        compiler_params=_cparams(("arbitrary", "arbitrary")),
        name="moe",
    )(be, nv, nu, slot, gates, h_tm, w_up, b_up.reshape(N_EXP, 1, 2 * D_EXP), w_down,
      b_down.reshape(N_EXP, 1, D))


def _moe_layer(ti, tg, h_tm, w_up, b_up, w_down, b_down):
    dest, meta = _route_a(ti)
    slot = _route_b(dest, meta)
    be = meta[:, 0, :MOE_NBLK].reshape(-1)
    nv = meta[:, 1, :MOE_NBLK].reshape(-1)
    nu = meta[:, 2, 0]
    gates = tg[:TOP_K].reshape(TOP_K, N_MOE_TILES, MOE_TILE).transpose(1, 0, 2).reshape(-1)
    return _moe(be, nv, nu, slot, gates, h_tm, w_up, b_up, w_down, b_down)


_MOE_OUT_SPEC = pl.BlockSpec((1, TM * ROW_CHUNKS, LANES),
                             lambda i: (i // (MOE_TILE // TM), i % (MOE_TILE // TM), 0))


def _conv_kernel(x_ref, moe_ref, modp_ref, g1_ref, w1_ref, b1_ref, wdw_ref, bdw_ref, lng_ref, lnb_ref,
                 w2_ref, b2_ref, mod_ref, g2_ref, wr_ref, br_ref,
                 x_out_ref, h_out_ref, ti_ref, tg_ref, pad_scr, conv_scr):
    i = pl.program_id(0)
    mp = modp_ref[0, 0]
    m = mod_ref[0, 0]
    x = x_ref[...] + mp[5:6] * _load_token_major(moe_ref, TM, (0,))
    h = (_rms(x) * g1_ref[0] * (1.0 + m[1:2]) + m[0:1]).astype(BF16)
    ag = jnp.dot(h, w1_ref[0], preferred_element_type=F32) + b1_ref[0]
    u = ag[:, :D] * _sigmoid(ag[:, D:])
    is_prompt = i < T_PROMPT // TM

    def fill(seg):
        pitch = seg + CONV_PAD
        for s in range(TM // seg):
            pad_scr[s * pitch:s * pitch + CONV_PAD, :] = jnp.zeros((CONV_PAD, D), F32)
            pad_scr[s * pitch + CONV_PAD:(s + 1) * pitch, :] = u[s * seg:(s + 1) * seg, :]
        end = (TM // seg) * pitch
        pad_scr[end:end + CONV_PAD, :] = jnp.zeros((CONV_PAD, D), F32)

    @pl.when(is_prompt)
    def _():
        fill(L_PROMPT)

    @pl.when(jnp.logical_not(is_prompt))
    def _():
        fill(GRID_W)

    per_seq = L_PROMPT // CONV_CHUNK
    halo = CONV_CHUNK + 2 * CONV_PAD

    def conv_chunk(c, carry):
        base = jnp.where(is_prompt, (c // per_seq) * (L_PROMPT + CONV_PAD) + (c % per_seq) * CONV_CHUNK,
                         c * (GRID_W + CONV_PAD))
        base = pl.multiple_of(base, SUBLANES)
        out_rows = pl.ds(pl.multiple_of(c * CONV_CHUNK, CONV_CHUNK), CONV_CHUNK)
        for gl in range(D // LANES):
            lanes = slice(gl * LANES, (gl + 1) * LANES)
            blk = pad_scr[pl.ds(base, halo), lanes]
            acc = jnp.zeros((CONV_CHUNK, LANES), F32)
            for tap in range(CONV_W):
                lo = CONV_PAD - CONV_W // 2 + tap
                acc = acc + blk[lo:lo + CONV_CHUNK, :] * wdw_ref[0, tap:tap + 1, lanes]
            conv_scr[out_rows, lanes] = acc + bdw_ref[0, :, lanes]
        return carry

    lax.fori_loop(0, TM // CONV_CHUNK, conv_chunk, 0)
    uc = conv_scr[...]
    mu = jnp.mean(uc, axis=-1, keepdims=True)
    var = jnp.mean(jnp.square(uc - mu), axis=-1, keepdims=True)
    z = (uc - mu) * lax.rsqrt(var + EPS) * lng_ref[0] + lnb_ref[0]
    z = (z * _sigmoid(z)).astype(BF16)
    y = jnp.dot(z, w2_ref[0], preferred_element_type=F32) + b2_ref[0]
    _post_mixer(x, y, m, g2_ref, wr_ref[0], br_ref[0], x_out_ref, h_out_ref, ti_ref, tg_ref, TM)


def _conv_layer(x1, moe0, mod4, g_norm1, w_pw1, b_pw1, w_dw, b_dw, ln_g, ln_b, w_pw2, b_pw2,
                g_norm2, wr_t, br_c):
    def full(shape):
        return pl.BlockSpec(shape, lambda i: (0,) * len(shape))

    pad_rows = (TM // GRID_W) * (GRID_W + CONV_PAD) + CONV_PAD
    return pl.pallas_call(
        _conv_kernel,
        grid=(T // TM,),
        in_specs=[pl.BlockSpec((TM, D), lambda i: (i, 0)),
                  _MOE_OUT_SPEC,
                  pl.BlockSpec((1, 1, 6, D), lambda i: (0, _cond_of_tile(i, TM), 0, 0)),
                  pl.BlockSpec((1, 1, D), lambda i: (1, 0, 0)),
                  full((1, D, 2 * D)), full((1, 1, 2 * D)), full((1, CONV_W, D)), full((1, 1, D)),
                  full((1, 1, D)), full((1, 1, D)), full((1, D, D)), full((1, 1, D))]
        + _post_in_specs(1),
        out_specs=_POST_OUT_SPECS,
        out_shape=_POST_OUT_SHAPES,
        scratch_shapes=[pltpu.VMEM((pad_rows, D), F32), pltpu.VMEM((TM, D), F32)],
        compiler_params=_cparams(("arbitrary",)),
        name="conv",
    )(x1, moe0, mod4, g_norm1, w_pw1, b_pw1.reshape(1, 1, 2 * D), w_dw, b_dw.reshape(1, 1, D),
      ln_g.reshape(1, 1, D), ln_b.reshape(1, 1, D), w_pw2, b_pw2.reshape(1, 1, D),
      mod4, g_norm2, wr_t, br_c)


def _final_kernel(x_ref, moe_ref, mod_ref, g_ref, yp_ref, ys_ref):
    i = pl.program_id(0)
    x = x_ref[...] + mod_ref[0, 0][5:6] * _load_token_major(moe_ref, TM, (0,))
    y = _rms(x) * g_ref[...]

    @pl.when(i < T_PROMPT // TM)
    def _():
        yp_ref[...] = y

    @pl.when(i >= T_PROMPT // TM)
    def _():
        ys_ref[...] = y


def _final(x, moe1, mod4, g_final):
    n_p = T_PROMPT // TM
    return pl.pallas_call(
        _final_kernel,
        grid=(T // TM,),
        in_specs=[pl.BlockSpec((TM, D), lambda i: (i, 0)),
                  _MOE_OUT_SPEC,
                  pl.BlockSpec((1, 1, 6, D), lambda i: (1, _cond_of_tile(i, TM), 0, 0)),
                  pl.BlockSpec((1, D), lambda i: (0, 0))],
        out_specs=[pl.BlockSpec((TM, D), lambda i: (jnp.minimum(i, n_p - 1), 0)),
                   pl.BlockSpec((TM, D), lambda i: (jnp.maximum(i - n_p, 0), 0))],
        out_shape=[jax.ShapeDtypeStruct((T_PROMPT, D), F32), jax.ShapeDtypeStruct((T_SAMPLE, D), F32)],
        compiler_params=_cparams(("arbitrary",)),
        name="final",
    )(x, moe1, mod4, g_final)


def kernel(x_prompt, x_sample, c, state_ret_fwd, state_ret_bwd, c_ctx, w_ada, b_ada, g_norm1, g_norm2,
           ret_w_in, ret_decay, ret_w_out, conv_w_pw1, conv_b_pw1, conv_w_dw, conv_b_dw, conv_ln_g,
           conv_ln_b, conv_w_pw2, conv_b_pw2, moe_w_router, moe_b_router, moe_w_up, moe_b_up,
           moe_w_down, moe_b_down, g_final):
    xp = x_prompt.reshape(T_PROMPT, D)
    xs = x_sample.reshape(T_SAMPLE, D)
    cond = jnp.concatenate([c_ctx[None, :], c, jnp.zeros((SUBLANES - 1 - N_SAMPLE, D), F32)], axis=0)
    mod = _ada(cond.T, w_ada, b_ada)
    mod4 = mod.reshape(mod.shape[0], SUBLANES, 6, D)
    wr_t = jnp.swapaxes(moe_w_router, 1, 2)
    br_c = moe_b_router[:, :, None]
    gn1 = g_norm1[:, None, :]
    gn2 = g_norm2[:, None, :]

    proj = _proj(xp, xs, g_norm1[0:1], mod4, ret_w_in[0].astype(BF16))
    og_p, new_f, new_b = _ret_prompt(ret_decay[0], proj)
    og_s = _ret_sample(ret_decay[0], proj, state_ret_fwd, state_ret_bwd)
    x1, h_tm, ti, tg = _ret_out(og_p, og_s, ret_w_out[0].astype(BF16), xp, xs, mod4, gn2, wr_t, br_c)
    moe0 = _moe_layer(ti, tg, h_tm, moe_w_up[0].astype(BF16), moe_b_up[0],
                      moe_w_down[0].astype(BF16), moe_b_down[0])

    x2, h_tm, ti, tg = _conv_layer(x1, moe0, mod4, gn1, conv_w_pw1.astype(BF16), conv_b_pw1, conv_w_dw,
                                   conv_b_dw, conv_ln_g, conv_ln_b, conv_w_pw2.astype(BF16), conv_b_pw2,
                                   gn2, wr_t, br_c)
    moe1 = _moe_layer(ti, tg, h_tm, moe_w_up[1].astype(BF16), moe_b_up[1],
                      moe_w_down[1].astype(BF16), moe_b_down[1])

    y_p, y_s = _final(x2, moe1, mod4, g_final[None, :])
    return (y_p.reshape(N_PROMPT, L_PROMPT, D), y_s.reshape(N_SAMPLE, L_SAMPLE, D), new_f, new_b)
```

```python
import functools

import numpy as np
import jax
import jax.numpy as jnp
from jax import lax
from jax.experimental import pallas as pl
from jax.experimental.pallas import tpu as pltpu

F32 = jnp.float32
BF16 = jnp.bfloat16

D = 1024
N_PROMPT, L_PROMPT = 32, 256
N_SAMPLE, L_SAMPLE = 2, 4096
T_PROMPT = N_PROMPT * L_PROMPT
T_SAMPLE = N_SAMPLE * L_SAMPLE
T = T_PROMPT + T_SAMPLE
GRID_W = 64
HEADS, DK, DV = 4, 256, 512
HK, HV = HEADS * DK, HEADS * DV
N_PROJ = 2 * HK + 2 * HV
ROPE_BASE = 10000.0
CONV_W = 31
CONV_PAD = 16
CONV_CHUNK = 64
N_EXP, TOP_K, D_EXP = 32, 4, 1024
SWIGLU_LIMIT, SWIGLU_ALPHA = 7.0, 1.702
EPS = 1e-6

LANES = 128
SUBLANES = 8
ROW_CHUNKS = D // LANES
VMEM_LIMIT = 56 * 1024 * 1024

TM = 512
MOE_TILE = 2048
MOE_BLK = 128
MOE_NBLK = MOE_TILE * TOP_K // MOE_BLK + N_EXP
MOE_SLOTS = MOE_NBLK * MOE_BLK
GATE_PITCH = 2 * MOE_TILE
N_MOE_TILES = T // MOE_TILE
RET_CHUNK = 256

NT_DIMS = (((1,), (1,)), ((), ()))
TN_DIMS = (((0,), (0,)), ((), ()))


def _cparams(sem):
    return pltpu.CompilerParams(dimension_semantics=sem, vmem_limit_bytes=VMEM_LIMIT)


def _cond_of_tile(i, tm):
    return jnp.maximum((i * tm) // L_SAMPLE - 1, 0)


def _sigmoid(x):
    return 1.0 / (1.0 + jnp.exp(-x))


def _rms(x):
    return x * lax.rsqrt(jnp.mean(x * x, axis=-1, keepdims=True) + EPS)


def _pick_x(i, tm, xp_ref, xs_ref):
    return jnp.where(i < T_PROMPT // tm, xp_ref[...], xs_ref[...])


def _x_specs(tm):
    n_p = T_PROMPT // tm
    return [pl.BlockSpec((tm, D), lambda i, *_: (jnp.minimum(i, n_p - 1), 0)),
            pl.BlockSpec((tm, D), lambda i, *_: (jnp.maximum(i - n_p, 0), 0))]


def _ada_kernel(ct_ref, w_ref, b_ref, o_ref):
    ct = ct_ref[...]
    s = ct * _sigmoid(ct)
    w = w_ref[0]
    rows = [jnp.sum(w * s[:, r:r + 1], axis=0, keepdims=True) for r in range(3)]
    rows.append(jnp.zeros((SUBLANES - 3, w.shape[1]), F32))
    o_ref[0] = jnp.concatenate(rows, axis=0) + b_ref[0]


def _ada(ct, w_ada, b_ada):
    depth, _, n = w_ada.shape
    tn = 1536
    return pl.pallas_call(
        _ada_kernel,
        grid=(depth, n // tn),
        in_specs=[pl.BlockSpec((D, SUBLANES), lambda l, j: (0, 0)),
                  pl.BlockSpec((1, D, tn), lambda l, j: (l, 0, j)),
                  pl.BlockSpec((1, 1, tn), lambda l, j: (l, 0, j))],
        out_specs=pl.BlockSpec((1, SUBLANES, tn), lambda l, j: (l, 0, j)),
        out_shape=jax.ShapeDtypeStruct((depth, SUBLANES, n), F32),
        compiler_params=_cparams(("arbitrary", "arbitrary")),
        name="ada",
    )(ct, w_ada, b_ada.reshape(depth, 1, n))


def _rope_tables():
    t = np.arange(L_SAMPLE)
    row = (t // GRID_W).astype(np.float32)
    col = (t % GRID_W).astype(np.float32)
    nf = DK // 4
    inv = (np.float32(ROPE_BASE) ** (-np.arange(nf, dtype=np.float32) / np.float32(nf))).astype(np.float32)
    cos, sin = [], []
    for pos in (row, col):
        ang = (pos[:, None] * inv[None, :]).astype(np.float32)
        c, s = np.cos(ang).astype(np.float32), np.sin(ang).astype(np.float32)
        cos += [c, c]
        sin += [-s, s]
    return np.concatenate(cos, axis=1), np.concatenate(sin, axis=1)


def _proj_kernel(xp_ref, xs_ref, g_ref, mod_ref, w_ref, cos_ref, sin_ref, o_ref):
    i = pl.program_id(0)
    m = mod_ref[0, 0]
    h = _rms(_pick_x(i, TM, xp_ref, xs_ref)) * g_ref[...]
    h = (h * (1.0 + m[1:2]) + m[0:1]).astype(BF16)
    is_sample = i >= T_PROMPT // TM
    cos = jnp.where(is_sample, cos_ref[...], 1.0)
    sin = jnp.where(is_sample, sin_ref[...], 0.0)
    for col in range(N_PROJ // D):
        acc = jnp.dot(h, w_ref[:, col * D:(col + 1) * D], preferred_element_type=F32)
        if col == 1:
            acc = acc * DK ** -0.5
        if col < 2:
            for c in range(D // LANES):
                a = acc[:, c * LANES:(c + 1) * LANES]
                p = (c % 2) * LANES
                r = a * cos[:, p:p + LANES] + pltpu.roll(a, LANES // 2, 1) * sin[:, p:p + LANES]
                o_ref[:, col * D + c * LANES:col * D + (c + 1) * LANES] = r.astype(BF16)
        else:
            o_ref[:, col * D:(col + 1) * D] = acc.astype(BF16)


def _proj(xp, xs, g1, mod4, w_in):
    cos, sin = _rope_tables()
    n_p = T_PROMPT // TM
    n_s = L_SAMPLE // TM
    tab_spec = pl.BlockSpec((TM, DK), lambda i: (jnp.maximum(i - n_p, 0) % n_s, 0))
    return pl.pallas_call(
        _proj_kernel,
        grid=(T // TM,),
        in_specs=_x_specs(TM) + [
            pl.BlockSpec((1, D), lambda i: (0, 0)),
            pl.BlockSpec((1, 1, 6, D), lambda i: (0, _cond_of_tile(i, TM), 0, 0)),
            pl.BlockSpec((D, N_PROJ), lambda i: (0, 0)),
            tab_spec, tab_spec],
        out_specs=pl.BlockSpec((TM, N_PROJ), lambda i: (i, 0)),
        out_shape=jax.ShapeDtypeStruct((T, N_PROJ), BF16),
        compiler_params=_cparams(("arbitrary",)),
        name="proj",
    )(xp, xs, g1, mod4, w_in, jnp.asarray(cos), jnp.asarray(sin))


def _log_decays(dec_ref, head):
    out = []
    for direction in range(2):
        d = jnp.full((1, 1), dec_ref[direction, head], F32)
        out.append(jnp.minimum(d, 0.0) - jnp.log(1.0 + jnp.exp(-jnp.abs(d))))
    return out


def _decay_mask(lgf, lgb, c):
    ii = lax.broadcasted_iota(jnp.int32, (c, c), 0)
    jj = lax.broadcasted_iota(jnp.int32, (c, c), 1)
    diff = (ii - jj).astype(F32)
    fwd = jnp.where(diff >= 0, jnp.exp(lgf * jnp.maximum(diff, 0.0)), 0.0)
    bwd = jnp.where(diff <= 0, jnp.exp(lgb * jnp.maximum(-diff, 0.0)), 0.0)
    return fwd + bwd


def _norm_gate(o, g):
    g = g.astype(F32)
    return (_rms(o) * (g * _sigmoid(g))).astype(BF16)


def _ret_prompt_kernel(dec_ref, q_ref, k_ref, v_ref, g_ref, o_ref, sf_ref, sb_ref):
    c = L_PROMPT
    lgf, lgb = _log_decays(dec_ref, pl.program_id(1))
    q, k, v = q_ref[...], k_ref[...], v_ref[...]
    s = lax.dot_general(q, k, NT_DIMS, preferred_element_type=F32) * _decay_mask(lgf, lgb, c)
    o = jnp.dot(s.astype(BF16), v, preferred_element_type=F32)
    o_ref[...] = _norm_gate(o, g_ref[...])
    pos = lax.broadcasted_iota(jnp.int32, (c, 1), 0).astype(F32)
    kf = k.astype(F32)
    k_fwd = (kf * jnp.exp(lgf * (c - 1.0 - pos))).astype(BF16)
    k_bwd = (kf * jnp.exp(lgb * pos)).astype(BF16)
    sf_ref[0, 0, 0] = lax.dot_general(k_fwd, v, TN_DIMS, preferred_element_type=F32)
    sb_ref[0, 0, 0] = lax.dot_general(k_bwd, v, TN_DIMS, preferred_element_type=F32)


def _ret_specs(seq_len, row0):
    r = row0 // seq_len
    return [pl.BlockSpec((seq_len, DK), lambda b, h: (r + b, h)),
            pl.BlockSpec((seq_len, DK), lambda b, h: (r + b, HK // DK + h)),
            pl.BlockSpec((seq_len, DV), lambda b, h: (r + b, 2 * HK // DV + h)),
            pl.BlockSpec((seq_len, DV), lambda b, h: (r + b, (2 * HK + HV) // DV + h))]


def _ret_prompt(decay, proj):
    state = jax.ShapeDtypeStruct((N_PROMPT, 1, HEADS, DK, DV), F32)
    state_spec = pl.BlockSpec((1, 1, 1, DK, DV), lambda b, h: (b, 0, h, 0, 0))
    return pl.pallas_call(
        _ret_prompt_kernel,
        grid=(N_PROMPT, HEADS),
        in_specs=[pl.BlockSpec(memory_space=pltpu.SMEM)] + _ret_specs(L_PROMPT, 0),
        out_specs=[pl.BlockSpec((L_PROMPT, DV), lambda b, h: (b, h)), state_spec, state_spec],
        out_shape=[jax.ShapeDtypeStruct((T_PROMPT, HV), BF16), state, state],
        compiler_params=_cparams(("arbitrary", "arbitrary")),
        name="ret_prompt",
    )(decay, proj, proj, proj, proj)


def _ret_sample_kernel(dec_ref, q_ref, k_ref, v_ref, g_ref, s0f_ref, s0b_ref, o_ref,
                       ob_scr, s_scr, dm_scr):
    c = RET_CHUNK
    nc = L_SAMPLE // c
    lgf, lgb = _log_decays(dec_ref, pl.program_id(1))
    dm_scr[...] = _decay_mask(lgf, lgb, c)
    pos = lax.broadcasted_iota(jnp.int32, (c, 1), 0).astype(F32)

    def chunk(ci):
        rows = pl.ds(pl.multiple_of(ci * c, c), c)
        return rows, q_ref[rows, :], k_ref[rows, :], v_ref[rows, :]

    def state_update(lg, write_pos, kc, vc):
        kw = (kc.astype(F32) * jnp.exp(lg * write_pos)).astype(BF16)
        s_scr[...] = s_scr[...] * jnp.exp(lg * c) + lax.dot_general(kw, vc, TN_DIMS, preferred_element_type=F32)

    def read_state(lg, read_pos, qc):
        qr = (qc.astype(F32) * jnp.exp(lg * read_pos)).astype(BF16)
        return jnp.dot(qr, s_scr[...].astype(BF16), preferred_element_type=F32)

    s_scr[...] = s0b_ref[0, 0, 0]

    def bwd(step, carry):
        rows, qc, kc, vc = chunk(nc - 1 - step)
        ob_scr[rows, :] = read_state(lgb, c - pos, qc)
        state_update(lgb, pos, kc, vc)
        return carry

    lax.fori_loop(0, nc, bwd, 0)

    s_scr[...] = s0f_ref[0, 0, 0]

    def fwd(ci, carry):
        rows, qc, kc, vc = chunk(ci)
        s = lax.dot_general(qc, kc, NT_DIMS, preferred_element_type=F32) * dm_scr[...]
        o = jnp.dot(s.astype(BF16), vc, preferred_element_type=F32)
        o = o + read_state(lgf, pos + 1.0, qc) + ob_scr[rows, :]
        state_update(lgf, c - 1.0 - pos, kc, vc)
        o_ref[rows, :] = _norm_gate(o, g_ref[rows, :])
        return carry

    lax.fori_loop(0, nc, fwd, 0)


def _ret_sample(decay, proj, s0f, s0b):
    state_spec = pl.BlockSpec((1, 1, 1, DK, DV), lambda b, h: (b, 0, h, 0, 0))
    return pl.pallas_call(
        _ret_sample_kernel,
        grid=(N_SAMPLE, HEADS),
        in_specs=[pl.BlockSpec(memory_space=pltpu.SMEM)] + _ret_specs(L_SAMPLE, T_PROMPT)
        + [state_spec, state_spec],
        out_specs=pl.BlockSpec((L_SAMPLE, DV), lambda b, h: (b, h)),
        out_shape=jax.ShapeDtypeStruct((T_SAMPLE, HV), BF16),
        scratch_shapes=[pltpu.VMEM((L_SAMPLE, DV), F32), pltpu.VMEM((DK, DV), F32),
                        pltpu.VMEM((RET_CHUNK, RET_CHUNK), F32)],
        compiler_params=_cparams(("arbitrary", "arbitrary")),
        name="ret_sample",
    )(decay, proj, proj, proj, proj, s0f, s0b)


def _store_token_major(ref, val, tm):
    for s in range(ROW_CHUNKS):
        ref[pl.ds(s, tm, stride=ROW_CHUNKS), :] = val[:, s * LANES:(s + 1) * LANES]


def _load_token_major(ref, tm, lead=()):
    return jnp.concatenate([ref[lead + (pl.ds(s, tm, stride=ROW_CHUNKS), slice(None))]
                            for s in range(ROW_CHUNKS)], axis=1)


def _post_mixer(x, y, m, g2_ref, wr_ref, br_ref, x_out_ref, h_out_ref, ti_ref, tg_ref, tm):
    x1 = x + m[2:3] * y
    h = _rms(x1) * g2_ref[0] * (1.0 + m[4:5]) + m[3:4]
    x_out_ref[...] = x1
    _store_token_major(h_out_ref, h, tm)
    w = wr_ref[...]
    w_hi = w.astype(BF16)
    w_lo = (w - w_hi.astype(F32)).astype(BF16)
    h_hi = h.astype(BF16)
    h_lo = (h - h_hi.astype(F32)).astype(BF16)
    dot = functools.partial(lax.dot_general, dimension_numbers=NT_DIMS, preferred_element_type=F32)
    cur = dot(w_hi, h_hi) + dot(w_hi, h_lo) + dot(w_lo, h_hi) + br_ref[...]
    ie = lax.broadcasted_iota(jnp.int32, (N_EXP, tm), 0).astype(F32)
    vals, idxs = [], []
    for _ in range(TOP_K):
        top = jnp.max(cur, axis=0, keepdims=True)
        idx = jnp.min(jnp.where(cur == top, ie, float(N_EXP)), axis=0, keepdims=True)
        vals.append(top)
        idxs.append(idx)
        cur = jnp.where(ie == idx, -jnp.inf, cur)
    ex = [jnp.exp(v - vals[0]) for v in vals]
    den = ex[0] + ex[1] + ex[2] + ex[3]
    pad = jnp.zeros((SUBLANES - TOP_K, tm), F32)
    ti_ref[...] = jnp.concatenate(idxs + [pad], axis=0).astype(jnp.int32)
    tg_ref[...] = jnp.concatenate([e / den for e in ex] + [pad], axis=0)


def _post_in_specs(layer):
    return [pl.BlockSpec((1, 1, 6, D), lambda i: (layer, _cond_of_tile(i, TM), 0, 0)),
            pl.BlockSpec((1, 1, D), lambda i: (layer, 0, 0)),
            pl.BlockSpec((1, N_EXP, D), lambda i: (layer, 0, 0)),
            pl.BlockSpec((1, N_EXP, 1), lambda i: (layer, 0, 0))]


_POST_OUT_SPECS = [pl.BlockSpec((TM, D), lambda i: (i, 0)),
                   pl.BlockSpec((TM * ROW_CHUNKS, LANES), lambda i: (i, 0)),
                   pl.BlockSpec((SUBLANES, TM), lambda i: (0, i)),
                   pl.BlockSpec((SUBLANES, TM), lambda i: (0, i))]
_POST_OUT_SHAPES = [jax.ShapeDtypeStruct((T, D), F32),
                    jax.ShapeDtypeStruct((T * ROW_CHUNKS, LANES), F32),
                    jax.ShapeDtypeStruct((SUBLANES, T), jnp.int32),
                    jax.ShapeDtypeStruct((SUBLANES, T), F32)]


def _ret_out_kernel(ogp_ref, ogs_ref, w_ref, xp_ref, xs_ref, mod_ref, g2_ref, wr_ref, br_ref,
                    x_out_ref, h_out_ref, ti_ref, tg_ref):
    i = pl.program_id(0)
    og = jnp.where(i < T_PROMPT // TM, ogp_ref[...], ogs_ref[...])
    y = jnp.dot(og, w_ref[...], preferred_element_type=F32)
    _post_mixer(_pick_x(i, TM, xp_ref, xs_ref), y, mod_ref[0, 0], g2_ref, wr_ref[0], br_ref[0],
                x_out_ref, h_out_ref, ti_ref, tg_ref, TM)


def _ret_out(og_p, og_s, w_out, xp, xs, mod4, g_norm2, wr_t, br_c):
    n_p = T_PROMPT // TM
    return pl.pallas_call(
        _ret_out_kernel,
        grid=(T // TM,),
        in_specs=[pl.BlockSpec((TM, HV), lambda i: (jnp.minimum(i, n_p - 1), 0)),
                  pl.BlockSpec((TM, HV), lambda i: (jnp.maximum(i - n_p, 0), 0)),
                  pl.BlockSpec((HV, D), lambda i: (0, 0))]
        + _x_specs(TM) + _post_in_specs(0),
        out_specs=_POST_OUT_SPECS,
        out_shape=_POST_OUT_SHAPES,
        compiler_params=_cparams(("arbitrary",)),
        name="ret_out",
    )(og_p, og_s, w_out, xp, xs, mod4, g_norm2, wr_t, br_c)


def _route_a_kernel(ti_ref, dest_ref, meta_ref):
    tt, tm = MOE_TILE, MOE_BLK
    ti = ti_ref[...]
    ie = lax.broadcasted_iota(jnp.int32, (N_EXP, tt), 0)
    onehots = [(ie == ti[k:k + 1]).astype(F32) for k in range(TOP_K)]
    oh = onehots[0] + onehots[1] + onehots[2] + onehots[3]
    ch = 512
    upper = (lax.broadcasted_iota(jnp.int32, (ch, ch), 0)
             < lax.broadcasted_iota(jnp.int32, (ch, ch), 1)).astype(BF16)
    carry = jnp.zeros((N_EXP, 1), F32)
    cums = []
    for c in range(tt // ch):
        blk = oh[:, c * ch:(c + 1) * ch]
        cums.append(jnp.dot(blk.astype(BF16), upper, preferred_element_type=F32) + carry)
        carry = carry + jnp.sum(blk, axis=1, keepdims=True)
    cum = jnp.concatenate(cums, axis=1)
    cnt = carry
    nb = jnp.floor((cnt + (tm - 1.0)) * (1.0 / tm))
    lower = (lax.broadcasted_iota(jnp.int32, (N_EXP, N_EXP), 1)
             < lax.broadcasted_iota(jnp.int32, (N_EXP, N_EXP), 0)).astype(BF16)
    offb = jnp.dot(lower, jnp.broadcast_to(nb, (N_EXP, LANES)).astype(BF16),
                   preferred_element_type=F32)[:, :1]
    off = offb * tm
    base = off + cum
    dests = [jnp.sum(onehots[k] * base, axis=0, keepdims=True) for k in range(TOP_K)]
    dests.append(jnp.zeros((SUBLANES - TOP_K, tt), F32))
    dest_ref[0] = jnp.concatenate(dests, axis=0).astype(jnp.int32)
    nused = jnp.sum(nb, axis=0, keepdims=True)
    jl = lax.broadcasted_iota(jnp.int32, (N_EXP, LANES), 1).astype(F32)
    jc = jnp.minimum(jl, nused - 1.0)
    be = jnp.minimum(jnp.sum(((offb + nb) <= jc).astype(F32), axis=0, keepdims=True), N_EXP - 1.0)
    ief = lax.broadcasted_iota(jnp.int32, (N_EXP, LANES), 0).astype(F32)
    end_row = jnp.sum(jnp.where(ief == be, off + cnt, 0.0), axis=0, keepdims=True)
    nvalid = jnp.clip(end_row - jl[:1] * tm, 0.0, float(tm))
    nvalid = jnp.where(jl[:1] < nused, nvalid, 0.0)
    run_end = jnp.sum(jnp.where(ief == be, offb + nb, 0.0), axis=0, keepdims=True)
    nxt = jnp.sum(((offb + nb) <= run_end).astype(F32), axis=0, keepdims=True)
    nxt = jnp.where(run_end < nused, nxt, -1.0)
    meta = jnp.concatenate([be, nvalid, jnp.broadcast_to(nused, (1, LANES)), nxt,
                            jnp.zeros((SUBLANES - 4, LANES), F32)], axis=0)
    meta_ref[0] = meta.astype(jnp.int32)


def _route_a(ti):
    return pl.pallas_call(
        _route_a_kernel,
        grid=(N_MOE_TILES,),
        in_specs=[pl.BlockSpec((SUBLANES, MOE_TILE), lambda i: (0, i))],
        out_specs=[pl.BlockSpec((1, SUBLANES, MOE_TILE), lambda i: (i, 0, 0)),
                   pl.BlockSpec((1, SUBLANES, LANES), lambda i: (i, 0, 0))],
        out_shape=[jax.ShapeDtypeStruct((N_MOE_TILES, SUBLANES, MOE_TILE), jnp.int32),
                   jax.ShapeDtypeStruct((N_MOE_TILES, SUBLANES, LANES), jnp.int32)],
        compiler_params=_cparams(("arbitrary",)),
        name="route_a",
    )(ti)


def _route_b_kernel(dest_ref, meta_ref, slot_ref):
    group = SUBLANES

    def per_block(j, carry):
        def pad(gi, c2):
            base = j * MOE_BLK + gi * group
            for u in range(group):
                slot_ref[base + u] = MOE_TILE * ROW_CHUNKS
            return c2
        lax.fori_loop(meta_ref[0, 1, j] // group, MOE_BLK // group, pad, 0)
        return carry

    lax.fori_loop(0, MOE_NBLK, per_block, 0)

    def per_group(tg, carry):
        t0 = tg * group
        loaded = [(k, u, dest_ref[0, k, t0 + u]) for u in range(group) for k in range(TOP_K)]
        for k, u, d in loaded:
            slot_ref[d] = (k * GATE_PITCH + t0 + u) * ROW_CHUNKS
        return carry

    lax.fori_loop(0, MOE_TILE // group, per_group, 0)


def _route_b(dest, meta):
    return pl.pallas_call(
        _route_b_kernel,
        grid=(N_MOE_TILES,),
        in_specs=[pl.BlockSpec((1, SUBLANES, MOE_TILE), lambda i: (i, 0, 0), memory_space=pltpu.SMEM),
                  pl.BlockSpec((1, SUBLANES, LANES), lambda i: (i, 0, 0), memory_space=pltpu.SMEM)],
        out_specs=pl.BlockSpec((MOE_SLOTS,), lambda i: (i,), memory_space=pltpu.SMEM),
        out_shape=jax.ShapeDtypeStruct((N_MOE_TILES * MOE_SLOTS,), jnp.int32),
        compiler_params=_cparams(("arbitrary",)),
        name="route_b",
    )(dest, meta)


def _moe_kernel(be_ref, nu_ref, nx_ref, slot_ref, gate_ref, h_ref, bu_ref, bd_ref, wu_hbm, wd_hbm,
                out_ref, g0_scr, g1_scr, y0_scr, y1_scr, wu_scr, wd_scr, sem):
    i = pl.program_id(0)
    tm = MOE_BLK
    nused = nu_ref[i]
    base = i * MOE_NBLK

    def weight_copies(e, buf):
        return (pltpu.make_async_copy(wu_hbm.at[e], wu_scr.at[buf], sem.at[0, buf]),
                pltpu.make_async_copy(wd_hbm.at[e], wd_scr.at[buf], sem.at[1, buf]))
    tile_rows = MOE_TILE * ROW_CHUNKS

    def row_slice(ii):
        return slice(ii * ROW_CHUNKS, (ii + 1) * ROW_CHUNKS)

    def gather(blk, g_scr, lo, hi):
        codes = slot_ref.at[pl.ds(blk * tm, tm)]
        for ii in range(lo, hi):
            off = pl.multiple_of(codes[ii] & (tile_rows - 1), ROW_CHUNKS)
            g_scr[row_slice(ii), :] = h_ref[pl.ds(off, ROW_CHUNKS), :]

    def scatter(blk, y_scr, lo, hi):
        batch = 4
        codes = slot_ref.at[pl.ds(blk * tm, tm)]
        for i0 in range(lo, hi, batch):
            pending = []
            for ii in range(i0, i0 + batch):
                code = codes[ii]
                gate = gate_ref[code >> 3]
                rows = pl.ds(pl.multiple_of(code & (2 * tile_rows - 1), ROW_CHUNKS), ROW_CHUNKS)
                pending.append((rows, out_ref[0, rows, :] + gate * y_scr[row_slice(ii), :]))
            for rows, val in pending:
                out_ref[0, rows, :] = val

    out_ref[...] = jnp.zeros(out_ref.shape, F32)
    y1_scr[...] = jnp.zeros(y1_scr.shape, F32)
    for copy in weight_copies(be_ref[base], 0):
        copy.start()
    gather(0, g0_scr, 0, tm)

    def step(j, run, g_cur, g_nxt, y_cur, y_prv):
        e = be_ref[base + j]
        jp = jnp.maximum(j - 1, 0)
        jn = jnp.minimum(j + 1, nused - 1)
        first = jnp.logical_or(j == 0, be_ref[base + jp] != e)
        run = run + jnp.where(jnp.logical_and(first, j > 0), 1, 0)
        buf = run & 1

        @pl.when(first)
        def _():
            for copy in weight_copies(e, buf):
                copy.wait()
            nxt = nx_ref[base + j]

            @pl.when(nxt >= 0)
            def _():
                for copy in weight_copies(nxt, 1 - buf):
                    copy.start()

        def compute(wbuf):
            scatter(jp, y_prv, 0, tm)
            gather(jn, g_nxt, 0, tm)
            a = _load_token_major(g_cur, tm).astype(BF16)
            up = jnp.dot(a, wu_scr[wbuf], preferred_element_type=F32) + bu_ref[e]
            glu = jnp.minimum(up[:, :D_EXP], SWIGLU_LIMIT)
            lin = jnp.clip(up[:, D_EXP:], -SWIGLU_LIMIT, SWIGLU_LIMIT)
            act = glu * _sigmoid(SWIGLU_ALPHA * glu) * (lin + 1.0)
            y = jnp.dot(act.astype(BF16), wd_scr[wbuf], preferred_element_type=F32) + bd_ref[e]
            _store_token_major(y_cur, y, tm)

        for wbuf in range(2):
            pl.when(buf == wbuf)(functools.partial(compute, wbuf))

        @pl.when(j == nused - 1)
        def _():
            scatter(j, y_cur, 0, tm)

        return run

    def pair(jj, run):
        run = step(2 * jj, run, g0_scr, g1_scr, y0_scr, y1_scr)
        return lax.cond(2 * jj + 1 < nused,
                        lambda r: step(2 * jj + 1, r, g1_scr, g0_scr, y1_scr, y0_scr),
                        lambda r: r, run)

    lax.fori_loop(0, (nused + 1) // 2, pair, jnp.int32(0))


def _moe(be, nu, nx, slot, gates, h_tm, w_up, b_up, w_down, b_down):
    row_buf = pltpu.VMEM((MOE_BLK * ROW_CHUNKS, LANES), F32)
    grid_spec = pltpu.PrefetchScalarGridSpec(
        num_scalar_prefetch=3,
        grid=(N_MOE_TILES,),
        in_specs=[
            pl.BlockSpec((MOE_SLOTS,), lambda i, *_: (i,), memory_space=pltpu.SMEM),
            pl.BlockSpec((TOP_K * GATE_PITCH,), lambda i, *_: (i,), memory_space=pltpu.SMEM),
            pl.BlockSpec((MOE_TILE * ROW_CHUNKS, LANES), lambda i, *_: (i, 0)),
            pl.BlockSpec((N_EXP, 1, 2 * D_EXP), lambda i, *_: (0, 0, 0)),
            pl.BlockSpec((N_EXP, 1, D), lambda i, *_: (0, 0, 0)),
            pl.BlockSpec(memory_space=pl.ANY),
            pl.BlockSpec(memory_space=pl.ANY),
        ],
        out_specs=pl.BlockSpec((1, (MOE_TILE + 1) * ROW_CHUNKS, LANES), lambda i, *_: (i, 0, 0)),
        scratch_shapes=[row_buf, row_buf, row_buf, row_buf,
                        pltpu.VMEM((2, D, 2 * D_EXP), BF16), pltpu.VMEM((2, D_EXP, D), BF16),
                        pltpu.SemaphoreType.DMA((2, 2))],
    )
    return pl.pallas_call(
        _moe_kernel,
        grid_spec=grid_spec,
        out_shape=jax.ShapeDtypeStruct((N_MOE_TILES, (MOE_TILE + 1) * ROW_CHUNKS, LANES), F32),
        compiler_params=_cparams(("arbitrary",)),
        name="moe",
    )(be, nu, nx, slot, gates, h_tm, b_up.reshape(N_EXP, 1, 2 * D_EXP), b_down.reshape(N_EXP, 1, D),
      w_up, w_down)


def _cast_kernel(w_ref, o_ref):
    o_ref[...] = w_ref[0].astype(BF16)


def _expert_weights_bf16(w, layer):
    _, _, k, n = w.shape
    return pl.pallas_call(
        _cast_kernel,
        grid=(N_EXP,),
        in_specs=[pl.BlockSpec((1, 1, k, n), lambda e: (layer, e, 0, 0))],
        out_specs=pl.BlockSpec((1, k, n), lambda e: (e, 0, 0)),
        out_shape=jax.ShapeDtypeStruct((N_EXP, k, n), BF16),
        compiler_params=_cparams(("arbitrary",)),
        name="cast",
    )(w)


def _moe_layer(ti, tg, h_tm, w_up, b_up, w_down, b_down):
    dest, meta = _route_a(ti)
    slot = _route_b(dest, meta)
    be, nx = (meta[:, r, :MOE_NBLK].reshape(-1) for r in (0, 3))
    nu = meta[:, 2, 0]
    gates = tg[:TOP_K].reshape(TOP_K, N_MOE_TILES, MOE_TILE).transpose(1, 0, 2)
    gates = jnp.pad(gates, ((0, 0), (0, 0), (0, GATE_PITCH - MOE_TILE))).reshape(-1)
    return _moe(be, nu, nx, slot, gates, h_tm, w_up, b_up, w_down, b_down)


_MOE_OUT_SPEC = pl.BlockSpec((1, TM * ROW_CHUNKS, LANES),
                             lambda i: (i // (MOE_TILE // TM), i % (MOE_TILE // TM), 0))


def _conv_kernel(x_ref, moe_ref, modp_ref, g1_ref, w1_ref, b1_ref, wdw_ref, bdw_ref, lng_ref, lnb_ref,
                 w2_ref, b2_ref, mod_ref, g2_ref, wr_ref, br_ref,
                 x_out_ref, h_out_ref, ti_ref, tg_ref, pad_scr, conv_scr, shift_scr):
    i = pl.program_id(0)
    mp = modp_ref[0, 0]
    m = mod_ref[0, 0]
    x = x_ref[...] + mp[5:6] * _load_token_major(moe_ref, TM, (0,))
    h = (_rms(x) * g1_ref[0] * (1.0 + m[1:2]) + m[0:1]).astype(BF16)
    ag = jnp.dot(h, w1_ref[0], preferred_element_type=F32) + b1_ref[0]
    u = ag[:, :D] * _sigmoid(ag[:, D:])
    is_prompt = i < T_PROMPT // TM

    def fill(seg):
        pitch = seg + CONV_PAD
        for s in range(TM // seg):
            pad_scr[s * pitch:s * pitch + CONV_PAD, :] = jnp.zeros((CONV_PAD, D), F32)
            pad_scr[s * pitch + CONV_PAD:(s + 1) * pitch, :] = u[s * seg:(s + 1) * seg, :]
        end = (TM // seg) * pitch
        pad_scr[end:end + CONV_PAD, :] = jnp.zeros((CONV_PAD, D), F32)

    @pl.when(is_prompt)
    def _():
        fill(L_PROMPT)

    @pl.when(jnp.logical_not(is_prompt))
    def _():
        fill(GRID_W)

    per_seq = L_PROMPT // CONV_CHUNK
    halo = CONV_CHUNK + 2 * CONV_PAD

    def conv_chunk(c, carry):
        base = jnp.where(is_prompt, (c // per_seq) * (L_PROMPT + CONV_PAD) + (c % per_seq) * CONV_CHUNK,
                         c * (GRID_W + CONV_PAD))
        base = pl.multiple_of(base, SUBLANES)
        out_rows = pl.ds(pl.multiple_of(c * CONV_CHUNK, CONV_CHUNK), CONV_CHUNK)
        for gl in range(D // LANES):
            lanes = slice(gl * LANES, (gl + 1) * LANES)
            blk = pad_scr[pl.ds(base, halo), lanes]
            span = halo - SUBLANES
            for r in range(SUBLANES):
                shift_scr[gl * SUBLANES + r] = blk[r:r + span, :]
            acc = jnp.zeros((CONV_CHUNK, LANES), F32)
            for tap in range(CONV_W):
                lo = CONV_PAD - CONV_W // 2 + tap
                al = lo // SUBLANES * SUBLANES
                acc = acc + (shift_scr[gl * SUBLANES + lo % SUBLANES, al:al + CONV_CHUNK, :]
                             * wdw_ref[0, tap:tap + 1, lanes])
            conv_scr[out_rows, lanes] = acc + bdw_ref[0, :, lanes]
        return carry

    lax.fori_loop(0, TM // CONV_CHUNK, conv_chunk, 0)
    uc = conv_scr[...]
    mu = jnp.mean(uc, axis=-1, keepdims=True)
    var = jnp.mean(jnp.square(uc - mu), axis=-1, keepdims=True)
    z = (uc - mu) * lax.rsqrt(var + EPS) * lng_ref[0] + lnb_ref[0]
    z = (z * _sigmoid(z)).astype(BF16)
    y = jnp.dot(z, w2_ref[0], preferred_element_type=F32) + b2_ref[0]
    _post_mixer(x, y, m, g2_ref, wr_ref[0], br_ref[0], x_out_ref, h_out_ref, ti_ref, tg_ref, TM)


def _conv_layer(x1, moe0, mod4, g_norm1, w_pw1, b_pw1, w_dw, b_dw, ln_g, ln_b, w_pw2, b_pw2,
                g_norm2, wr_t, br_c):
    def full(shape):
        return pl.BlockSpec(shape, lambda i: (0,) * len(shape))

    pad_rows = (TM // GRID_W) * (GRID_W + CONV_PAD) + CONV_PAD
    return pl.pallas_call(
        _conv_kernel,
        grid=(T // TM,),
        in_specs=[pl.BlockSpec((TM, D), lambda i: (i, 0)),
                  _MOE_OUT_SPEC,
                  pl.BlockSpec((1, 1, 6, D), lambda i: (0, _cond_of_tile(i, TM), 0, 0)),
                  pl.BlockSpec((1, 1, D), lambda i: (1, 0, 0)),
                  full((1, D, 2 * D)), full((1, 1, 2 * D)), full((1, CONV_W, D)), full((1, 1, D)),
                  full((1, 1, D)), full((1, 1, D)), full((1, D, D)), full((1, 1, D))]
        + _post_in_specs(1),
        out_specs=_POST_OUT_SPECS,
        out_shape=_POST_OUT_SHAPES,
        scratch_shapes=[pltpu.VMEM((pad_rows, D), F32), pltpu.VMEM((TM, D), F32),
                        pltpu.VMEM((D // LANES * SUBLANES, CONV_CHUNK + 2 * CONV_PAD - SUBLANES, LANES), F32)],
        compiler_params=_cparams(("arbitrary",)),
        name="conv",
    )(x1, moe0, mod4, g_norm1, w_pw1, b_pw1.reshape(1, 1, 2 * D), w_dw, b_dw.reshape(1, 1, D),
      ln_g.reshape(1, 1, D), ln_b.reshape(1, 1, D), w_pw2, b_pw2.reshape(1, 1, D),
      mod4, g_norm2, wr_t, br_c)


def _final_kernel(x_ref, moe_ref, mod_ref, g_ref, yp_ref, ys_ref):
    i = pl.program_id(0)
    x = x_ref[...] + mod_ref[0, 0][5:6] * _load_token_major(moe_ref, TM, (0,))
    y = _rms(x) * g_ref[...]

    @pl.when(i < T_PROMPT // TM)
    def _():
        yp_ref[...] = y

    @pl.when(i >= T_PROMPT // TM)
    def _():
        ys_ref[...] = y


def _final(x, moe1, mod4, g_final):
    n_p = T_PROMPT // TM
    return pl.pallas_call(
        _final_kernel,
        grid=(T // TM,),
        in_specs=[pl.BlockSpec((TM, D), lambda i: (i, 0)),
                  _MOE_OUT_SPEC,
                  pl.BlockSpec((1, 1, 6, D), lambda i: (1, _cond_of_tile(i, TM), 0, 0)),
                  pl.BlockSpec((1, D), lambda i: (0, 0))],
        out_specs=[pl.BlockSpec((TM, D), lambda i: (jnp.minimum(i, n_p - 1), 0)),
                   pl.BlockSpec((TM, D), lambda i: (jnp.maximum(i - n_p, 0), 0))],
        out_shape=[jax.ShapeDtypeStruct((T_PROMPT, D), F32), jax.ShapeDtypeStruct((T_SAMPLE, D), F32)],
        compiler_params=_cparams(("arbitrary",)),
        name="final",
    )(x, moe1, mod4, g_final)


def kernel(x_prompt, x_sample, c, state_ret_fwd, state_ret_bwd, c_ctx, w_ada, b_ada, g_norm1, g_norm2,
           ret_w_in, ret_decay, ret_w_out, conv_w_pw1, conv_b_pw1, conv_w_dw, conv_b_dw, conv_ln_g,
           conv_ln_b, conv_w_pw2, conv_b_pw2, moe_w_router, moe_b_router, moe_w_up, moe_b_up,
           moe_w_down, moe_b_down, g_final):
    xp = x_prompt.reshape(T_PROMPT, D)
    xs = x_sample.reshape(T_SAMPLE, D)
    cond = jnp.concatenate([c_ctx[None, :], c, jnp.zeros((SUBLANES - 1 - N_SAMPLE, D), F32)], axis=0)
    mod = _ada(cond.T, w_ada, b_ada)
    mod4 = mod.reshape(mod.shape[0], SUBLANES, 6, D)
    wr_t = jnp.swapaxes(moe_w_router, 1, 2)
    br_c = moe_b_router[:, :, None]
    gn1 = g_norm1[:, None, :]
    gn2 = g_norm2[:, None, :]

    proj = _proj(xp, xs, g_norm1[0:1], mod4, ret_w_in[0].astype(BF16))
    og_p, new_f, new_b = _ret_prompt(ret_decay[0], proj)
    og_s = _ret_sample(ret_decay[0], proj, state_ret_fwd, state_ret_bwd)
    x1, h_tm, ti, tg = _ret_out(og_p, og_s, ret_w_out[0].astype(BF16), xp, xs, mod4, gn2, wr_t, br_c)
    moe0 = _moe_layer(ti, tg, h_tm, _expert_weights_bf16(moe_w_up, 0), moe_b_up[0],
                      _expert_weights_bf16(moe_w_down, 0), moe_b_down[0])

    x2, h_tm, ti, tg = _conv_layer(x1, moe0, mod4, gn1, conv_w_pw1.astype(BF16), conv_b_pw1, conv_w_dw,
                                   conv_b_dw, conv_ln_g, conv_ln_b, conv_w_pw2.astype(BF16), conv_b_pw2,
                                   gn2, wr_t, br_c)
    moe1 = _moe_layer(ti, tg, h_tm, _expert_weights_bf16(moe_w_up, 1), moe_b_up[1],
                      _expert_weights_bf16(moe_w_down, 1), moe_b_down[1])

    y_p, y_s = _final(x2, moe1, mod4, g_final[None, :])
    return (y_p.reshape(N_PROMPT, L_PROMPT, D), y_s.reshape(N_SAMPLE, L_SAMPLE, D), new_f, new_b)
```

```python
import functools

import numpy as np
import jax
import jax.numpy as jnp
from jax import lax
from jax.experimental import pallas as pl
from jax.experimental.pallas import tpu as pltpu

F32 = jnp.float32
BF16 = jnp.bfloat16

D = 1024
N_PROMPT, L_PROMPT = 32, 256
N_SAMPLE, L_SAMPLE = 2, 4096
T_PROMPT = N_PROMPT * L_PROMPT
T_SAMPLE = N_SAMPLE * L_SAMPLE
T = T_PROMPT + T_SAMPLE
GRID_W = 64
HEADS, DK, DV = 4, 256, 512
HK, HV = HEADS * DK, HEADS * DV
N_PROJ = 2 * HK + 2 * HV
ROPE_BASE = 10000.0
CONV_W = 31
CONV_PAD = 16
CONV_CHUNK = 64
N_EXP, TOP_K, D_EXP = 32, 4, 1024
SWIGLU_LIMIT, SWIGLU_ALPHA = 7.0, 1.702
EPS = 1e-6

LANES = 128
SUBLANES = 8
ROW_CHUNKS = D // LANES
VMEM_LIMIT = 56 * 1024 * 1024

TM = 512
MOE_TILE = 4096
MOE_BLK = 128
MOE_NBLK = MOE_TILE * TOP_K // MOE_BLK + N_EXP
MOE_SLOTS = MOE_NBLK * MOE_BLK
META_LANES = 2 * LANES
CODE_SHIFT = 16
N_MOE_TILES = T // MOE_TILE
RET_CHUNK = 256

NT_DIMS = (((1,), (1,)), ((), ()))
TN_DIMS = (((0,), (0,)), ((), ()))


def _cparams(sem):
    return pltpu.CompilerParams(dimension_semantics=sem, vmem_limit_bytes=VMEM_LIMIT)


def _cond_of_tile(i, tm):
    return jnp.maximum((i * tm) // L_SAMPLE - 1, 0)


def _sigmoid(x):
    return 1.0 / (1.0 + jnp.exp(-x))


def _rms(x):
    return x * lax.rsqrt(jnp.mean(x * x, axis=-1, keepdims=True) + EPS)


def _pick_x(i, tm, xp_ref, xs_ref):
    return jnp.where(i < T_PROMPT // tm, xp_ref[...], xs_ref[...])


def _x_specs(tm):
    n_p = T_PROMPT // tm
    return [pl.BlockSpec((tm, D), lambda i, *_: (jnp.minimum(i, n_p - 1), 0)),
            pl.BlockSpec((tm, D), lambda i, *_: (jnp.maximum(i - n_p, 0), 0))]


def _ada_kernel(ct_ref, w_ref, b_ref, o_ref):
    ct = ct_ref[...]
    s = ct * _sigmoid(ct)
    w = w_ref[0]
    rows = [jnp.sum(w * s[:, r:r + 1], axis=0, keepdims=True) for r in range(3)]
    rows.append(jnp.zeros((SUBLANES - 3, w.shape[1]), F32))
    o_ref[0] = jnp.concatenate(rows, axis=0) + b_ref[0]


def _ada(ct, w_ada, b_ada):
    depth, _, n = w_ada.shape
    tn = 1536
    return pl.pallas_call(
        _ada_kernel,
        grid=(depth, n // tn),
        in_specs=[pl.BlockSpec((D, SUBLANES), lambda l, j: (0, 0)),
                  pl.BlockSpec((1, D, tn), lambda l, j: (l, 0, j)),
                  pl.BlockSpec((1, 1, tn), lambda l, j: (l, 0, j))],
        out_specs=pl.BlockSpec((1, SUBLANES, tn), lambda l, j: (l, 0, j)),
        out_shape=jax.ShapeDtypeStruct((depth, SUBLANES, n), F32),
        compiler_params=_cparams(("arbitrary", "arbitrary")),
        name="ada",
    )(ct, w_ada, b_ada.reshape(depth, 1, n))


def _rope_tables():
    t = np.arange(L_SAMPLE)
    row = (t // GRID_W).astype(np.float32)
    col = (t % GRID_W).astype(np.float32)
    nf = DK // 4
    inv = (np.float32(ROPE_BASE) ** (-np.arange(nf, dtype=np.float32) / np.float32(nf))).astype(np.float32)
    cos, sin = [], []
    for pos in (row, col):
        ang = (pos[:, None] * inv[None, :]).astype(np.float32)
        c, s = np.cos(ang).astype(np.float32), np.sin(ang).astype(np.float32)
        cos += [c, c]
        sin += [-s, s]
    return np.concatenate(cos, axis=1), np.concatenate(sin, axis=1)


def _proj_kernel(xp_ref, xs_ref, g_ref, mod_ref, w_ref, cos_ref, sin_ref, o_ref):
    i = pl.program_id(0)
    m = mod_ref[0, 0]
    h = _rms(_pick_x(i, TM, xp_ref, xs_ref)) * g_ref[...]
    h = (h * (1.0 + m[1:2]) + m[0:1]).astype(BF16)
    is_sample = i >= T_PROMPT // TM
    cos = jnp.where(is_sample, cos_ref[...], 1.0)
    sin = jnp.where(is_sample, sin_ref[...], 0.0)
    for col in range(N_PROJ // D):
        acc = jnp.dot(h, w_ref[:, col * D:(col + 1) * D], preferred_element_type=F32)
        if col == 1:
            acc = acc * DK ** -0.5
        if col < 2:
            for c in range(D // LANES):
                a = acc[:, c * LANES:(c + 1) * LANES]
                p = (c % 2) * LANES
                r = a * cos[:, p:p + LANES] + pltpu.roll(a, LANES // 2, 1) * sin[:, p:p + LANES]
                o_ref[:, col * D + c * LANES:col * D + (c + 1) * LANES] = r.astype(BF16)
        else:
            o_ref[:, col * D:(col + 1) * D] = acc.astype(BF16)


def _proj(xp, xs, g1, mod4, w_in):
    cos, sin = _rope_tables()
    n_p = T_PROMPT // TM
    n_s = L_SAMPLE // TM
    tab_spec = pl.BlockSpec((TM, DK), lambda i: (jnp.maximum(i - n_p, 0) % n_s, 0))
    return pl.pallas_call(
        _proj_kernel,
        grid=(T // TM,),
        in_specs=_x_specs(TM) + [
            pl.BlockSpec((1, D), lambda i: (0, 0)),
            pl.BlockSpec((1, 1, 6, D), lambda i: (0, _cond_of_tile(i, TM), 0, 0)),
            pl.BlockSpec((D, N_PROJ), lambda i: (0, 0)),
            tab_spec, tab_spec],
        out_specs=pl.BlockSpec((TM, N_PROJ), lambda i: (i, 0)),
        out_shape=jax.ShapeDtypeStruct((T, N_PROJ), BF16),
        compiler_params=_cparams(("arbitrary",)),
        name="proj",
    )(xp, xs, g1, mod4, w_in, jnp.asarray(cos), jnp.asarray(sin))


def _log_decays(dec_ref, head):
    out = []
    for direction in range(2):
        d = jnp.full((1, 1), dec_ref[direction, head], F32)
        out.append(jnp.minimum(d, 0.0) - jnp.log(1.0 + jnp.exp(-jnp.abs(d))))
    return out


def _decay_mask(lgf, lgb, c):
    ii = lax.broadcasted_iota(jnp.int32, (c, c), 0)
    jj = lax.broadcasted_iota(jnp.int32, (c, c), 1)
    diff = (ii - jj).astype(F32)
    fwd = jnp.where(diff >= 0, jnp.exp(lgf * jnp.maximum(diff, 0.0)), 0.0)
    bwd = jnp.where(diff <= 0, jnp.exp(lgb * jnp.maximum(-diff, 0.0)), 0.0)
    return fwd + bwd


def _norm_gate(o, g):
    g = g.astype(F32)
    return (_rms(o) * (g * _sigmoid(g))).astype(BF16)


def _ret_prompt_kernel(dec_ref, q_ref, k_ref, v_ref, g_ref, o_ref, sf_ref, sb_ref):
    c = L_PROMPT
    lgf, lgb = _log_decays(dec_ref, pl.program_id(1))
    q, k, v = q_ref[...], k_ref[...], v_ref[...]
    s = lax.dot_general(q, k, NT_DIMS, preferred_element_type=F32) * _decay_mask(lgf, lgb, c)
    o = jnp.dot(s.astype(BF16), v, preferred_element_type=F32)
    o_ref[...] = _norm_gate(o, g_ref[...])
    pos = lax.broadcasted_iota(jnp.int32, (c, 1), 0).astype(F32)
    kf = k.astype(F32)
    k_fwd = (kf * jnp.exp(lgf * (c - 1.0 - pos))).astype(BF16)
    k_bwd = (kf * jnp.exp(lgb * pos)).astype(BF16)
    sf_ref[0, 0, 0] = lax.dot_general(k_fwd, v, TN_DIMS, preferred_element_type=F32)
    sb_ref[0, 0, 0] = lax.dot_general(k_bwd, v, TN_DIMS, preferred_element_type=F32)


def _ret_specs(seq_len, row0):
    r = row0 // seq_len
    return [pl.BlockSpec((seq_len, DK), lambda b, h: (r + b, h)),
            pl.BlockSpec((seq_len, DK), lambda b, h: (r + b, HK // DK + h)),
            pl.BlockSpec((seq_len, DV), lambda b, h: (r + b, 2 * HK // DV + h)),
            pl.BlockSpec((seq_len, DV), lambda b, h: (r + b, (2 * HK + HV) // DV + h))]


def _ret_prompt(decay, proj):
    state = jax.ShapeDtypeStruct((N_PROMPT, 1, HEADS, DK, DV), F32)
    state_spec = pl.BlockSpec((1, 1, 1, DK, DV), lambda b, h: (b, 0, h, 0, 0))
    return pl.pallas_call(
        _ret_prompt_kernel,
        grid=(N_PROMPT, HEADS),
        in_specs=[pl.BlockSpec(memory_space=pltpu.SMEM)] + _ret_specs(L_PROMPT, 0),
        out_specs=[pl.BlockSpec((L_PROMPT, DV), lambda b, h: (b, h)), state_spec, state_spec],
        out_shape=[jax.ShapeDtypeStruct((T_PROMPT, HV), BF16), state, state],
        compiler_params=_cparams(("arbitrary", "arbitrary")),
        name="ret_prompt",
    )(decay, proj, proj, proj, proj)


def _ret_sample_kernel(dec_ref, q_ref, k_ref, v_ref, g_ref, s0f_ref, s0b_ref, o_ref,
                       ob_scr, s_scr, dm_scr):
    c = RET_CHUNK
    nc = L_SAMPLE // c
    lgf, lgb = _log_decays(dec_ref, pl.program_id(1))
    dm_scr[...] = _decay_mask(lgf, lgb, c)
    pos = lax.broadcasted_iota(jnp.int32, (c, 1), 0).astype(F32)

    def chunk(ci):
        rows = pl.ds(pl.multiple_of(ci * c, c), c)
        return rows, q_ref[rows, :], k_ref[rows, :], v_ref[rows, :]

    def state_update(lg, write_pos, kc, vc):
        kw = (kc.astype(F32) * jnp.exp(lg * write_pos)).astype(BF16)
        s_scr[...] = s_scr[...] * jnp.exp(lg * c) + lax.dot_general(kw, vc, TN_DIMS, preferred_element_type=F32)

    def read_state(lg, read_pos, qc):
        qr = (qc.astype(F32) * jnp.exp(lg * read_pos)).astype(BF16)
        return jnp.dot(qr, s_scr[...].astype(BF16), preferred_element_type=F32)

    s_scr[...] = s0b_ref[0, 0, 0]

    def bwd(step, carry):
        rows, qc, kc, vc = chunk(nc - 1 - step)
        ob_scr[rows, :] = read_state(lgb, c - pos, qc)
        state_update(lgb, pos, kc, vc)
        return carry

    lax.fori_loop(0, nc, bwd, 0)

    s_scr[...] = s0f_ref[0, 0, 0]

    def fwd(ci, carry):
        rows, qc, kc, vc = chunk(ci)
        s = lax.dot_general(qc, kc, NT_DIMS, preferred_element_type=F32) * dm_scr[...]
        o = jnp.dot(s.astype(BF16), vc, preferred_element_type=F32)
        o = o + read_state(lgf, pos + 1.0, qc) + ob_scr[rows, :]
        state_update(lgf, c - 1.0 - pos, kc, vc)
        o_ref[rows, :] = _norm_gate(o, g_ref[rows, :])
        return carry

    lax.fori_loop(0, nc, fwd, 0)


def _ret_sample(decay, proj, s0f, s0b):
    state_spec = pl.BlockSpec((1, 1, 1, DK, DV), lambda b, h: (b, 0, h, 0, 0))
    return pl.pallas_call(
        _ret_sample_kernel,
        grid=(N_SAMPLE, HEADS),
        in_specs=[pl.BlockSpec(memory_space=pltpu.SMEM)] + _ret_specs(L_SAMPLE, T_PROMPT)
        + [state_spec, state_spec],
        out_specs=pl.BlockSpec((L_SAMPLE, DV), lambda b, h: (b, h)),
        out_shape=jax.ShapeDtypeStruct((T_SAMPLE, HV), BF16),
        scratch_shapes=[pltpu.VMEM((L_SAMPLE, DV), F32), pltpu.VMEM((DK, DV), F32),
                        pltpu.VMEM((RET_CHUNK, RET_CHUNK), F32)],
        compiler_params=_cparams(("arbitrary", "arbitrary")),
        name="ret_sample",
    )(decay, proj, proj, proj, proj, s0f, s0b)


def _store_token_major(ref, val, tm):
    for s in range(ROW_CHUNKS):
        ref[pl.ds(s, tm, stride=ROW_CHUNKS), :] = val[:, s * LANES:(s + 1) * LANES]


def _load_token_major(ref, tm, lead=()):
    return jnp.concatenate([ref[lead + (pl.ds(s, tm, stride=ROW_CHUNKS), slice(None))]
                            for s in range(ROW_CHUNKS)], axis=1)


def _post_mixer(x, y, m, g2_ref, wr_ref, br_ref, x_out_ref, h_out_ref, ti_ref, tg_ref, tm):
    x1 = x + m[2:3] * y
    h = _rms(x1) * g2_ref[0] * (1.0 + m[4:5]) + m[3:4]
    x_out_ref[...] = x1
    _store_token_major(h_out_ref, h, tm)
    w = wr_ref[...]
    w_hi = w.astype(BF16)
    w_lo = (w - w_hi.astype(F32)).astype(BF16)
    h_hi = h.astype(BF16)
    h_lo = (h - h_hi.astype(F32)).astype(BF16)
    dot = functools.partial(lax.dot_general, dimension_numbers=NT_DIMS, preferred_element_type=F32)
    cur = dot(w_hi, h_hi) + dot(w_hi, h_lo) + dot(w_lo, h_hi) + br_ref[...]
    ie = lax.broadcasted_iota(jnp.int32, (N_EXP, tm), 0).astype(F32)
    vals, idxs = [], []
    for _ in range(TOP_K):
        top = jnp.max(cur, axis=0, keepdims=True)
        idx = jnp.min(jnp.where(cur == top, ie, float(N_EXP)), axis=0, keepdims=True)
        vals.append(top)
        idxs.append(idx)
        cur = jnp.where(ie == idx, -jnp.inf, cur)
    ex = [jnp.exp(v - vals[0]) for v in vals]
    den = ex[0] + ex[1] + ex[2] + ex[3]
    pad = jnp.zeros((SUBLANES - TOP_K, tm), F32)
    ti_ref[...] = jnp.concatenate(idxs + [pad], axis=0).astype(jnp.int32)
    tg_ref[...] = jnp.concatenate([e / den for e in ex] + [pad], axis=0)


def _post_in_specs(layer):
    return [pl.BlockSpec((1, 1, 6, D), lambda i: (layer, _cond_of_tile(i, TM), 0, 0)),
            pl.BlockSpec((1, 1, D), lambda i: (layer, 0, 0)),
            pl.BlockSpec((1, N_EXP, D), lambda i: (layer, 0, 0)),
            pl.BlockSpec((1, N_EXP, 1), lambda i: (layer, 0, 0))]


_POST_OUT_SPECS = [pl.BlockSpec((TM, D), lambda i: (i, 0)),
                   pl.BlockSpec((TM * ROW_CHUNKS, LANES), lambda i: (i, 0)),
                   pl.BlockSpec((SUBLANES, TM), lambda i: (0, i)),
                   pl.BlockSpec((SUBLANES, TM), lambda i: (0, i))]
_POST_OUT_SHAPES = [jax.ShapeDtypeStruct((T, D), F32),
                    jax.ShapeDtypeStruct((T * ROW_CHUNKS, LANES), F32),
                    jax.ShapeDtypeStruct((SUBLANES, T), jnp.int32),
                    jax.ShapeDtypeStruct((SUBLANES, T), F32)]


def _ret_out_kernel(ogp_ref, ogs_ref, w_ref, xp_ref, xs_ref, mod_ref, g2_ref, wr_ref, br_ref,
                    x_out_ref, h_out_ref, ti_ref, tg_ref):
    i = pl.program_id(0)
    og = jnp.where(i < T_PROMPT // TM, ogp_ref[...], ogs_ref[...])
    y = jnp.dot(og, w_ref[...], preferred_element_type=F32)
    _post_mixer(_pick_x(i, TM, xp_ref, xs_ref), y, mod_ref[0, 0], g2_ref, wr_ref[0], br_ref[0],
                x_out_ref, h_out_ref, ti_ref, tg_ref, TM)


def _ret_out(og_p, og_s, w_out, xp, xs, mod4, g_norm2, wr_t, br_c):
    n_p = T_PROMPT // TM
    return pl.pallas_call(
        _ret_out_kernel,
        grid=(T // TM,),
        in_specs=[pl.BlockSpec((TM, HV), lambda i: (jnp.minimum(i, n_p - 1), 0)),
                  pl.BlockSpec((TM, HV), lambda i: (jnp.maximum(i - n_p, 0), 0)),
                  pl.BlockSpec((HV, D), lambda i: (0, 0))]
        + _x_specs(TM) + _post_in_specs(0),
        out_specs=_POST_OUT_SPECS,
        out_shape=_POST_OUT_SHAPES,
        compiler_params=_cparams(("arbitrary",)),
        name="ret_out",
    )(og_p, og_s, w_out, xp, xs, mod4, g_norm2, wr_t, br_c)


def _route_a_kernel(ti_ref, dest_ref, meta_ref):
    tt, tm = MOE_TILE, MOE_BLK
    ti = ti_ref[...]
    ie = lax.broadcasted_iota(jnp.int32, (N_EXP, tt), 0)
    onehots = [(ie == ti[k:k + 1]).astype(F32) for k in range(TOP_K)]
    oh = onehots[0] + onehots[1] + onehots[2] + onehots[3]
    ch = 512
    upper = (lax.broadcasted_iota(jnp.int32, (ch, ch), 0)
             < lax.broadcasted_iota(jnp.int32, (ch, ch), 1)).astype(BF16)
    carry = jnp.zeros((N_EXP, 1), F32)
    cums = []
    for c in range(tt // ch):
        blk = oh[:, c * ch:(c + 1) * ch]
        cums.append(jnp.dot(blk.astype(BF16), upper, preferred_element_type=F32) + carry)
        carry = carry + jnp.sum(blk, axis=1, keepdims=True)
    cum = jnp.concatenate(cums, axis=1)
    cnt = carry
    nb = jnp.floor((cnt + (tm - 1.0)) * (1.0 / tm))
    lower = (lax.broadcasted_iota(jnp.int32, (N_EXP, N_EXP), 1)
             < lax.broadcasted_iota(jnp.int32, (N_EXP, N_EXP), 0)).astype(BF16)
    offb = jnp.dot(lower, jnp.broadcast_to(nb, (N_EXP, LANES)).astype(BF16),
                   preferred_element_type=F32)[:, :1]
    off = offb * tm
    base = off + cum
    dests = [jnp.sum(onehots[k] * base, axis=0, keepdims=True) for k in range(TOP_K)]
    dests.append(jnp.zeros((SUBLANES - TOP_K, tt), F32))
    dest_ref[0] = jnp.concatenate(dests, axis=0).astype(jnp.int32)
    nused = jnp.sum(nb, axis=0, keepdims=True)
    jl = lax.broadcasted_iota(jnp.int32, (N_EXP, META_LANES), 1).astype(F32)
    jc = jnp.minimum(jl, nused - 1.0)
    be = jnp.minimum(jnp.sum(((offb + nb) <= jc).astype(F32), axis=0, keepdims=True), N_EXP - 1.0)
    ief = lax.broadcasted_iota(jnp.int32, (N_EXP, META_LANES), 0).astype(F32)
    end_row = jnp.sum(jnp.where(ief == be, off + cnt, 0.0), axis=0, keepdims=True)
    nvalid = jnp.clip(end_row - jl[:1] * tm, 0.0, float(tm))
    nvalid = jnp.where(jl[:1] < nused, nvalid, 0.0)
    run_end = jnp.sum(jnp.where(ief == be, offb + nb, 0.0), axis=0, keepdims=True)
    nxt = jnp.sum(((offb + nb) <= run_end).astype(F32), axis=0, keepdims=True)
    nxt = jnp.where(run_end < nused, nxt, -1.0)
    meta = jnp.concatenate([be, nvalid, jnp.broadcast_to(nused, (1, META_LANES)), nxt,
                            jnp.zeros((SUBLANES - 4, META_LANES), F32)], axis=0)
    meta_ref[0] = meta.astype(jnp.int32)


def _route_a(ti):
    return pl.pallas_call(
        _route_a_kernel,
        grid=(N_MOE_TILES,),
        in_specs=[pl.BlockSpec((SUBLANES, MOE_TILE), lambda i: (0, i))],
        out_specs=[pl.BlockSpec((1, SUBLANES, MOE_TILE), lambda i: (i, 0, 0)),
                   pl.BlockSpec((1, SUBLANES, META_LANES), lambda i: (i, 0, 0))],
        out_shape=[jax.ShapeDtypeStruct((N_MOE_TILES, SUBLANES, MOE_TILE), jnp.int32),
                   jax.ShapeDtypeStruct((N_MOE_TILES, SUBLANES, META_LANES), jnp.int32)],
        compiler_params=_cparams(("arbitrary",)),
        name="route_a",
    )(ti)


ROUTE_STEPS = N_EXP // N_MOE_TILES


def _cast_route_kernel(w_ref, dest_ref, meta_ref, o_ref, slot_ref):
    o_ref[...] = w_ref[0].astype(BF16)
    part = pl.program_id(0) % ROUTE_STEPS
    group = SUBLANES

    @pl.when(part == 0)
    def _():
        def per_block(j, carry):
            def pad(gi, c2):
                slots = slot_ref.at[pl.ds(j * MOE_BLK + gi * group, group)]
                for u in range(group):
                    slots[u] = MOE_TILE * ROW_CHUNKS
                return c2
            lax.fori_loop(meta_ref[0, 1, j] // group, MOE_BLK // group, pad, 0)
            return carry

        lax.fori_loop(0, MOE_NBLK, per_block, 0)

    tokens = MOE_TILE // ROUTE_STEPS

    def per_group(tg, carry):
        t0 = part * tokens + tg * group
        rows = [dest_ref.at[0, k, pl.ds(t0, group)] for k in range(TOP_K)]
        tok_code = t0 * ((1 << CODE_SHIFT) + ROW_CHUNKS)
        for u0 in range(0, group, 2):
            loaded = [(k, u, rows[k][u]) for u in range(u0, u0 + 2) for k in range(TOP_K)]
            for k, u, d in loaded:
                slot_ref[d] = tok_code + ((k * MOE_TILE + u) << CODE_SHIFT) + u * ROW_CHUNKS
        return carry

    lax.fori_loop(0, tokens // group, per_group, 0)


def _cast_route(w, layer, dest, meta):
    _, _, k, n = w.shape
    tile = lambda e: e // ROUTE_STEPS
    return pl.pallas_call(
        _cast_route_kernel,
        grid=(N_EXP,),
        in_specs=[pl.BlockSpec((1, 1, k, n), lambda e: (layer, e, 0, 0)),
                  pl.BlockSpec((1, SUBLANES, MOE_TILE), lambda e: (tile(e), 0, 0), memory_space=pltpu.SMEM),
                  pl.BlockSpec((1, SUBLANES, META_LANES), lambda e: (tile(e), 0, 0),
                               memory_space=pltpu.SMEM)],
        out_specs=[pl.BlockSpec((1, k, n), lambda e: (e, 0, 0)),
                   pl.BlockSpec((MOE_SLOTS,), lambda e: (tile(e),), memory_space=pltpu.SMEM)],
        out_shape=[jax.ShapeDtypeStruct((N_EXP, k, n), BF16),
                   jax.ShapeDtypeStruct((N_MOE_TILES * MOE_SLOTS,), jnp.int32)],
        compiler_params=_cparams(("arbitrary",)),
        name="cast_route",
    )(w, dest, meta)


def _moe_kernel(be_ref, nu_ref, nx_ref, slot_ref, gate_ref, bu_ref, bd_ref, h_hbm, wu_hbm, wd_hbm,
                out_hbm, g0_scr, g1_scr, y0_scr, y1_scr, wu_scr, wd_scr, h_scr, out_scr, sem, tile_sem):
    i = pl.program_id(0)
    tm = MOE_BLK
    nused = nu_ref[i]
    base = i * MOE_NBLK

    def weight_copies(e, buf):
        return (pltpu.make_async_copy(wu_hbm.at[e], wu_scr.at[buf], sem.at[0, buf]),
                pltpu.make_async_copy(wd_hbm.at[e], wd_scr.at[buf], sem.at[1, buf]))
    tile_rows = MOE_TILE * ROW_CHUNKS
    h_copy = pltpu.make_async_copy(h_hbm.at[pl.ds(i * tile_rows, tile_rows)], h_scr, tile_sem.at[0])
    out_copy = pltpu.make_async_copy(out_scr, out_hbm.at[i], tile_sem.at[1])

    def row_slice(ii):
        return slice(ii * ROW_CHUNKS, (ii + 1) * ROW_CHUNKS)

    def gather(blk, g_scr, lo, hi):
        codes = slot_ref.at[pl.ds(blk * tm, tm)]
        for ii in range(lo, hi):
            off = pl.multiple_of(codes[ii] & (tile_rows - 1), ROW_CHUNKS)
            g_scr[row_slice(ii), :] = h_scr[pl.ds(off, ROW_CHUNKS), :]

    def scatter(blk, y_scr, lo, hi):
        batch = 4
        codes = slot_ref.at[pl.ds(blk * tm, tm)]
        for i0 in range(lo, hi, batch):
            pending = []
            for ii in range(i0, i0 + batch):
                code = codes[ii]
                gate = gate_ref[code >> CODE_SHIFT]
                off = code & ((1 << CODE_SHIFT) - 1)
                rows = pl.ds(pl.multiple_of(off, ROW_CHUNKS), ROW_CHUNKS)
                pending.append((rows, out_scr[rows, :] + gate * y_scr[row_slice(ii), :]))
            for rows, val in pending:
                out_scr[rows, :] = val

    h_copy.start()
    for copy in weight_copies(be_ref[base], 0):
        copy.start()
    out_scr[...] = jnp.zeros(out_scr.shape, F32)
    y1_scr[...] = jnp.zeros(y1_scr.shape, F32)
    h_copy.wait()
    gather(0, g0_scr, 0, tm)

    def step(j, run, g_cur, g_nxt, y_cur, y_prv):
        e = be_ref[base + j]
        jp = jnp.maximum(j - 1, 0)
        jn = jnp.minimum(j + 1, nused - 1)
        first = jnp.logical_or(j == 0, be_ref[base + jp] != e)
        run = run + jnp.where(jnp.logical_and(first, j > 0), 1, 0)
        buf = run & 1

        @pl.when(first)
        def _():
            for copy in weight_copies(e, buf):
                copy.wait()
            nxt = nx_ref[base + j]

            @pl.when(nxt >= 0)
            def _():
                for copy in weight_copies(nxt, 1 - buf):
                    copy.start()

        def compute(wbuf):
            scatter(jp, y_prv, 0, tm)
            gather(jn, g_nxt, 0, tm)
            a = _load_token_major(g_cur, tm).astype(BF16)
            up = jnp.dot(a, wu_scr[wbuf], preferred_element_type=F32) + bu_ref[e]
            glu = jnp.minimum(up[:, :D_EXP], SWIGLU_LIMIT)
            lin = jnp.clip(up[:, D_EXP:], -SWIGLU_LIMIT, SWIGLU_LIMIT)
            act = glu * _sigmoid(SWIGLU_ALPHA * glu) * (lin + 1.0)
            y = jnp.dot(act.astype(BF16), wd_scr[wbuf], preferred_element_type=F32) + bd_ref[e]
            _store_token_major(y_cur, y, tm)

        for wbuf in range(2):
            pl.when(buf == wbuf)(functools.partial(compute, wbuf))

        @pl.when(j == nused - 1)
        def _():
            scatter(j, y_cur, 0, tm)

        return run

    def pair(jj, run):
        run = step(2 * jj, run, g0_scr, g1_scr, y0_scr, y1_scr)
        return lax.cond(2 * jj + 1 < nused,
                        lambda r: step(2 * jj + 1, r, g1_scr, g0_scr, y1_scr, y0_scr),
                        lambda r: r, run)

    lax.fori_loop(0, (nused + 1) // 2, pair, jnp.int32(0))
    out_copy.start()
    out_copy.wait()


def _moe(be, nu, nx, slot, gates, h_tm, w_up, b_up, w_down, b_down):
    row_buf = pltpu.VMEM((MOE_BLK * ROW_CHUNKS, LANES), F32)
    grid_spec = pltpu.PrefetchScalarGridSpec(
        num_scalar_prefetch=3,
        grid=(N_MOE_TILES,),
        in_specs=[
            pl.BlockSpec((MOE_SLOTS,), lambda i, *_: (i,), memory_space=pltpu.SMEM),
            pl.BlockSpec((TOP_K * MOE_TILE,), lambda i, *_: (i,), memory_space=pltpu.SMEM),
            pl.BlockSpec((N_EXP, 1, 2 * D_EXP), lambda i, *_: (0, 0, 0)),
            pl.BlockSpec((N_EXP, 1, D), lambda i, *_: (0, 0, 0)),
            pl.BlockSpec(memory_space=pl.ANY),
            pl.BlockSpec(memory_space=pl.ANY),
            pl.BlockSpec(memory_space=pl.ANY),
        ],
        out_specs=pl.BlockSpec(memory_space=pl.ANY),
        scratch_shapes=[row_buf, row_buf, row_buf, row_buf,
                        pltpu.VMEM((2, D, 2 * D_EXP), BF16), pltpu.VMEM((2, D_EXP, D), BF16),
                        pltpu.VMEM((MOE_TILE * ROW_CHUNKS, LANES), F32),
                        pltpu.VMEM(((MOE_TILE + 1) * ROW_CHUNKS, LANES), F32),
                        pltpu.SemaphoreType.DMA((2, 2)), pltpu.SemaphoreType.DMA((2,))],
    )
    return pl.pallas_call(
        _moe_kernel,
        grid_spec=grid_spec,
        out_shape=jax.ShapeDtypeStruct((N_MOE_TILES, (MOE_TILE + 1) * ROW_CHUNKS, LANES), F32),
        compiler_params=_cparams(("arbitrary",)),
        name="moe",
    )(be, nu, nx, slot, gates, b_up.reshape(N_EXP, 1, 2 * D_EXP), b_down.reshape(N_EXP, 1, D),
      h_tm, w_up, w_down)


def _cast_kernel(w_ref, o_ref):
    o_ref[...] = w_ref[0].astype(BF16)


def _expert_weights_bf16(w, layer):
    _, _, k, n = w.shape
    return pl.pallas_call(
        _cast_kernel,
        grid=(N_EXP,),
        in_specs=[pl.BlockSpec((1, 1, k, n), lambda e: (layer, e, 0, 0))],
        out_specs=pl.BlockSpec((1, k, n), lambda e: (e, 0, 0)),
        out_shape=jax.ShapeDtypeStruct((N_EXP, k, n), BF16),
        compiler_params=_cparams(("arbitrary",)),
        name="cast",
    )(w)


def _moe_layer(layer, ti, tg, h_tm, w_up, b_up, w_down, b_down):
    dest, meta = _route_a(ti)
    w_up_bf16, slot = _cast_route(w_up, layer, dest, meta)
    be, nx = (meta[:, r, :MOE_NBLK].reshape(-1) for r in (0, 3))
    nu = meta[:, 2, 0]
    gates = tg[:TOP_K].reshape(TOP_K, N_MOE_TILES, MOE_TILE).transpose(1, 0, 2).reshape(-1)
    return _moe(be, nu, nx, slot, gates, h_tm, w_up_bf16, b_up[layer],
                _expert_weights_bf16(w_down, layer), b_down[layer])


_MOE_OUT_SPEC = pl.BlockSpec((1, TM * ROW_CHUNKS, LANES),
                             lambda i: (i // (MOE_TILE // TM), i % (MOE_TILE // TM), 0))


def _conv_kernel(x_ref, moe_ref, modp_ref, g1_ref, w1_ref, b1_ref, wdw_ref, bdw_ref, lng_ref, lnb_ref,
                 w2_ref, b2_ref, mod_ref, g2_ref, wr_ref, br_ref,
                 x_out_ref, h_out_ref, ti_ref, tg_ref, pad_scr, conv_scr, shift_scr):
    i = pl.program_id(0)
    mp = modp_ref[0, 0]
    m = mod_ref[0, 0]
    x = x_ref[...] + mp[5:6] * _load_token_major(moe_ref, TM, (0,))
    h = (_rms(x) * g1_ref[0] * (1.0 + m[1:2]) + m[0:1]).astype(BF16)
    ag = jnp.dot(h, w1_ref[0], preferred_element_type=F32) + b1_ref[0]
    u = ag[:, :D] * _sigmoid(ag[:, D:])
    is_prompt = i < T_PROMPT // TM

    def fill(seg):
        pitch = seg + CONV_PAD
        for s in range(TM // seg):
            pad_scr[s * pitch:s * pitch + CONV_PAD, :] = jnp.zeros((CONV_PAD, D), F32)
            pad_scr[s * pitch + CONV_PAD:(s + 1) * pitch, :] = u[s * seg:(s + 1) * seg, :]
        end = (TM // seg) * pitch
        pad_scr[end:end + CONV_PAD, :] = jnp.zeros((CONV_PAD, D), F32)

    @pl.when(is_prompt)
    def _():
        fill(L_PROMPT)

    @pl.when(jnp.logical_not(is_prompt))
    def _():
        fill(GRID_W)

    per_seq = L_PROMPT // CONV_CHUNK
    halo = CONV_CHUNK + 2 * CONV_PAD

    def conv_chunk(c, carry):
        base = jnp.where(is_prompt, (c // per_seq) * (L_PROMPT + CONV_PAD) + (c % per_seq) * CONV_CHUNK,
                         c * (GRID_W + CONV_PAD))
        base = pl.multiple_of(base, SUBLANES)
        out_rows = pl.ds(pl.multiple_of(c * CONV_CHUNK, CONV_CHUNK), CONV_CHUNK)
        for gl in range(D // LANES):
            lanes = slice(gl * LANES, (gl + 1) * LANES)
            blk = pad_scr[pl.ds(base, halo), lanes]
            span = halo - SUBLANES
            for r in range(SUBLANES):
                shift_scr[gl * SUBLANES + r] = blk[r:r + span, :]
            acc = jnp.zeros((CONV_CHUNK, LANES), F32)
            for tap in range(CONV_W):
                lo = CONV_PAD - CONV_W // 2 + tap
                al = lo // SUBLANES * SUBLANES
                acc = acc + (shift_scr[gl * SUBLANES + lo % SUBLANES, al:al + CONV_CHUNK, :]
                             * wdw_ref[0, tap:tap + 1, lanes])
            conv_scr[out_rows, lanes] = acc + bdw_ref[0, :, lanes]
        return carry

    lax.fori_loop(0, TM // CONV_CHUNK, conv_chunk, 0)
    uc = conv_scr[...]
    mu = jnp.mean(uc, axis=-1, keepdims=True)
    var = jnp.mean(jnp.square(uc - mu), axis=-1, keepdims=True)
    z = (uc - mu) * lax.rsqrt(var + EPS) * lng_ref[0] + lnb_ref[0]
    z = (z * _sigmoid(z)).astype(BF16)
    y = jnp.dot(z, w2_ref[0], preferred_element_type=F32) + b2_ref[0]
    _post_mixer(x, y, m, g2_ref, wr_ref[0], br_ref[0], x_out_ref, h_out_ref, ti_ref, tg_ref, TM)


def _conv_layer(x1, moe0, mod4, g_norm1, w_pw1, b_pw1, w_dw, b_dw, ln_g, ln_b, w_pw2, b_pw2,
                g_norm2, wr_t, br_c):
    def full(shape):
        return pl.BlockSpec(shape, lambda i: (0,) * len(shape))

    pad_rows = (TM // GRID_W) * (GRID_W + CONV_PAD) + CONV_PAD
    return pl.pallas_call(
        _conv_kernel,
        grid=(T // TM,),
        in_specs=[pl.BlockSpec((TM, D), lambda i: (i, 0)),
                  _MOE_OUT_SPEC,
                  pl.BlockSpec((1, 1, 6, D), lambda i: (0, _cond_of_tile(i, TM), 0, 0)),
                  pl.BlockSpec((1, 1, D), lambda i: (1, 0, 0)),
                  full((1, D, 2 * D)), full((1, 1, 2 * D)), full((1, CONV_W, D)), full((1, 1, D)),
                  full((1, 1, D)), full((1, 1, D)), full((1, D, D)), full((1, 1, D))]
        + _post_in_specs(1),
        out_specs=_POST_OUT_SPECS,
        out_shape=_POST_OUT_SHAPES,
        scratch_shapes=[pltpu.VMEM((pad_rows, D), F32), pltpu.VMEM((TM, D), F32),
                        pltpu.VMEM((D // LANES * SUBLANES, CONV_CHUNK + 2 * CONV_PAD - SUBLANES, LANES), F32)],
        compiler_params=_cparams(("arbitrary",)),
        name="conv",
    )(x1, moe0, mod4, g_norm1, w_pw1, b_pw1.reshape(1, 1, 2 * D), w_dw, b_dw.reshape(1, 1, D),
      ln_g.reshape(1, 1, D), ln_b.reshape(1, 1, D), w_pw2, b_pw2.reshape(1, 1, D),
      mod4, g_norm2, wr_t, br_c)


def _final_kernel(x_ref, moe_ref, mod_ref, g_ref, yp_ref, ys_ref):
    i = pl.program_id(0)
    x = x_ref[...] + mod_ref[0, 0][5:6] * _load_token_major(moe_ref, TM, (0,))
    y = _rms(x) * g_ref[...]

    @pl.when(i < T_PROMPT // TM)
    def _():
        yp_ref[...] = y

    @pl.when(i >= T_PROMPT // TM)
    def _():
        ys_ref[...] = y


def _final(x, moe1, mod4, g_final):
    n_p = T_PROMPT // TM
    return pl.pallas_call(
        _final_kernel,
        grid=(T // TM,),
        in_specs=[pl.BlockSpec((TM, D), lambda i: (i, 0)),
                  _MOE_OUT_SPEC,
                  pl.BlockSpec((1, 1, 6, D), lambda i: (1, _cond_of_tile(i, TM), 0, 0)),
                  pl.BlockSpec((1, D), lambda i: (0, 0))],
        out_specs=[pl.BlockSpec((TM, D), lambda i: (jnp.minimum(i, n_p - 1), 0)),
                   pl.BlockSpec((TM, D), lambda i: (jnp.maximum(i - n_p, 0), 0))],
        out_shape=[jax.ShapeDtypeStruct((T_PROMPT, D), F32), jax.ShapeDtypeStruct((T_SAMPLE, D), F32)],
        compiler_params=_cparams(("arbitrary",)),
        name="final",
    )(x, moe1, mod4, g_final)


def kernel(x_prompt, x_sample, c, state_ret_fwd, state_ret_bwd, c_ctx, w_ada, b_ada, g_norm1, g_norm2,
           ret_w_in, ret_decay, ret_w_out, conv_w_pw1, conv_b_pw1, conv_w_dw, conv_b_dw, conv_ln_g,
           conv_ln_b, conv_w_pw2, conv_b_pw2, moe_w_router, moe_b_router, moe_w_up, moe_b_up,
           moe_w_down, moe_b_down, g_final):
    xp = x_prompt.reshape(T_PROMPT, D)
    xs = x_sample.reshape(T_SAMPLE, D)
    cond = jnp.concatenate([c_ctx[None, :], c, jnp.zeros((SUBLANES - 1 - N_SAMPLE, D), F32)], axis=0)
    mod = _ada(cond.T, w_ada, b_ada)
    mod4 = mod.reshape(mod.shape[0], SUBLANES, 6, D)
    wr_t = jnp.swapaxes(moe_w_router, 1, 2)
    br_c = moe_b_router[:, :, None]
    gn1 = g_norm1[:, None, :]
    gn2 = g_norm2[:, None, :]

    proj = _proj(xp, xs, g_norm1[0:1], mod4, ret_w_in[0].astype(BF16))
    og_p, new_f, new_b = _ret_prompt(ret_decay[0], proj)
    og_s = _ret_sample(ret_decay[0], proj, state_ret_fwd, state_ret_bwd)
    x1, h_tm, ti, tg = _ret_out(og_p, og_s, ret_w_out[0].astype(BF16), xp, xs, mod4, gn2, wr_t, br_c)
    moe0 = _moe_layer(0, ti, tg, h_tm, moe_w_up, moe_b_up, moe_w_down, moe_b_down)

    x2, h_tm, ti, tg = _conv_layer(x1, moe0, mod4, gn1, conv_w_pw1.astype(BF16), conv_b_pw1, conv_w_dw,
                                   conv_b_dw, conv_ln_g, conv_ln_b, conv_w_pw2.astype(BF16), conv_b_pw2,
                                   gn2, wr_t, br_c)
    moe1 = _moe_layer(1, ti, tg, h_tm, moe_w_up, moe_b_up, moe_w_down, moe_b_down)

    y_p, y_s = _final(x2, moe1, mod4, g_final[None, :])
    return (y_p.reshape(N_PROMPT, L_PROMPT, D), y_s.reshape(N_SAMPLE, L_SAMPLE, D), new_f, new_b)
```

```python
import functools

import numpy as np
import jax
import jax.numpy as jnp
from jax import lax
from jax.experimental import pallas as pl
from jax.experimental.pallas import tpu as pltpu

F32 = jnp.float32
BF16 = jnp.bfloat16

D = 1024
N_PROMPT, L_PROMPT = 32, 256
N_SAMPLE, L_SAMPLE = 2, 4096
T_PROMPT = N_PROMPT * L_PROMPT
T_SAMPLE = N_SAMPLE * L_SAMPLE
T = T_PROMPT + T_SAMPLE
GRID_W = 64
HEADS, DK, DV = 4, 256, 512
HK, HV = HEADS * DK, HEADS * DV
N_PROJ = 2 * HK + 2 * HV
ROPE_BASE = 10000.0
CONV_W = 31
CONV_PAD = 16
CONV_CHUNK = 64
N_EXP, TOP_K, D_EXP = 32, 4, 1024
SWIGLU_LIMIT, SWIGLU_ALPHA = 7.0, 1.702
EPS = 1e-6

LANES = 128
SUBLANES = 8
ROW_CHUNKS = D // LANES
VMEM_LIMIT = 56 * 1024 * 1024

TM = 512
MOE_TILE = 4096
MOE_BLK = 128
MOE_NBLK = MOE_TILE * TOP_K // MOE_BLK + N_EXP
MOE_SLOTS = MOE_NBLK * MOE_BLK
META_LANES = 2 * LANES
CODE_SHIFT = 16
N_MOE_TILES = T // MOE_TILE
RET_CHUNK = 256

NT_DIMS = (((1,), (1,)), ((), ()))
TN_DIMS = (((0,), (0,)), ((), ()))


def _cparams(sem):
    return pltpu.CompilerParams(dimension_semantics=sem, vmem_limit_bytes=VMEM_LIMIT)


def _cond_of_tile(i, tm):
    return jnp.maximum((i * tm) // L_SAMPLE - 1, 0)


def _sigmoid(x):
    return 0.5 * jnp.tanh(0.5 * x) + 0.5


def _rms(x):
    return x * lax.rsqrt(jnp.mean(x * x, axis=-1, keepdims=True) + EPS)


def _pick_x(i, tm, xp_ref, xs_ref):
    return jnp.where(i < T_PROMPT // tm, xp_ref[...], xs_ref[...])


def _x_specs(tm):
    n_p = T_PROMPT // tm
    return [pl.BlockSpec((tm, D), lambda i, *_: (jnp.minimum(i, n_p - 1), 0)),
            pl.BlockSpec((tm, D), lambda i, *_: (jnp.maximum(i - n_p, 0), 0))]


def _ada_kernel(ct_ref, w_ref, b_ref, o_ref):
    ct = ct_ref[...]
    s = ct * _sigmoid(ct)
    w = w_ref[0]
    rows = [jnp.sum(w * s[:, r:r + 1], axis=0, keepdims=True) for r in range(3)]
    rows.append(jnp.zeros((SUBLANES - 3, w.shape[1]), F32))
    o_ref[0] = jnp.concatenate(rows, axis=0) + b_ref[0]


def _ada(ct, w_ada, b_ada):
    depth, _, n = w_ada.shape
    tn = 1536
    return pl.pallas_call(
        _ada_kernel,
        grid=(depth, n // tn),
        in_specs=[pl.BlockSpec((D, SUBLANES), lambda l, j: (0, 0)),
                  pl.BlockSpec((1, D, tn), lambda l, j: (l, 0, j)),
                  pl.BlockSpec((1, 1, tn), lambda l, j: (l, 0, j))],
        out_specs=pl.BlockSpec((1, SUBLANES, tn), lambda l, j: (l, 0, j)),
        out_shape=jax.ShapeDtypeStruct((depth, SUBLANES, n), F32),
        compiler_params=_cparams(("arbitrary", "arbitrary")),
        name="ada",
    )(ct, w_ada, b_ada.reshape(depth, 1, n))


def _rope_tables():
    t = np.arange(L_SAMPLE)
    row = (t // GRID_W).astype(np.float32)
    col = (t % GRID_W).astype(np.float32)
    nf = DK // 4
    inv = (np.float32(ROPE_BASE) ** (-np.arange(nf, dtype=np.float32) / np.float32(nf))).astype(np.float32)
    cos, sin = [], []
    for pos in (row, col):
        ang = (pos[:, None] * inv[None, :]).astype(np.float32)
        c, s = np.cos(ang).astype(np.float32), np.sin(ang).astype(np.float32)
        cos += [c, c]
        sin += [-s, s]
    return np.concatenate(cos, axis=1), np.concatenate(sin, axis=1)


def _proj_kernel(xp_ref, xs_ref, g_ref, mod_ref, w_ref, cos_ref, sin_ref, o_ref):
    i = pl.program_id(0)
    m = mod_ref[0, 0]
    h = _rms(_pick_x(i, TM, xp_ref, xs_ref)) * (g_ref[...] * (1.0 + m[1:2]))
    h = (h + m[0:1]).astype(BF16)
    is_sample = i >= T_PROMPT // TM
    cos = jnp.where(is_sample, cos_ref[...], 1.0)
    sin = jnp.where(is_sample, sin_ref[...], 0.0)
    for col in range(N_PROJ // D):
        acc = jnp.dot(h, w_ref[:, col * D:(col + 1) * D], preferred_element_type=F32)
        if col == 1:
            acc = acc * DK ** -0.5
        if col < 2:
            for c in range(D // LANES):
                a = acc[:, c * LANES:(c + 1) * LANES]
                p = (c % 2) * LANES
                r = a * cos[:, p:p + LANES] + pltpu.roll(a, LANES // 2, 1) * sin[:, p:p + LANES]
                o_ref[:, col * D + c * LANES:col * D + (c + 1) * LANES] = r.astype(BF16)
        else:
            o_ref[:, col * D:(col + 1) * D] = acc.astype(BF16)


def _proj(xp, xs, g1, mod4, w_in):
    cos, sin = _rope_tables()
    n_p = T_PROMPT // TM
    n_s = L_SAMPLE // TM
    tab_spec = pl.BlockSpec((TM, DK), lambda i: (jnp.maximum(i - n_p, 0) % n_s, 0))
    return pl.pallas_call(
        _proj_kernel,
        grid=(T // TM,),
        in_specs=_x_specs(TM) + [
            pl.BlockSpec((1, D), lambda i: (0, 0)),
            pl.BlockSpec((1, 1, 6, D), lambda i: (0, _cond_of_tile(i, TM), 0, 0)),
            pl.BlockSpec((D, N_PROJ), lambda i: (0, 0)),
            tab_spec, tab_spec],
        out_specs=pl.BlockSpec((TM, N_PROJ), lambda i: (i, 0)),
        out_shape=jax.ShapeDtypeStruct((T, N_PROJ), BF16),
        compiler_params=_cparams(("arbitrary",)),
        name="proj",
    )(xp, xs, g1, mod4, w_in, jnp.asarray(cos), jnp.asarray(sin))


def _log_decays(dec_ref, head):
    out = []
    for direction in range(2):
        d = jnp.full((1, 1), dec_ref[direction, head], F32)
        out.append(jnp.minimum(d, 0.0) - jnp.log(1.0 + jnp.exp(-jnp.abs(d))))
    return out


def _decay_mask(lgf, lgb, c):
    ii = lax.broadcasted_iota(jnp.int32, (c, c), 0)
    jj = lax.broadcasted_iota(jnp.int32, (c, c), 1)
    diff = (ii - jj).astype(F32)
    fwd = jnp.where(diff >= 0, jnp.exp(lgf * jnp.maximum(diff, 0.0)), 0.0)
    bwd = jnp.where(diff <= 0, jnp.exp(lgb * jnp.maximum(-diff, 0.0)), 0.0)
    return fwd + bwd


def _norm_gate(o, g):
    g = g.astype(F32)
    return (_rms(o) * (g * _sigmoid(g))).astype(BF16)


def _ret_prompt_kernel(dec_ref, p_ref, o_ref, sf_ref, sb_ref):
    c = L_PROMPT
    pos = lax.broadcasted_iota(jnp.int32, (c, 1), 0).astype(F32)
    for head in range(HEADS):
        lgf, lgb = _log_decays(dec_ref, head)
        q = p_ref[:, head * DK:(head + 1) * DK]
        k = p_ref[:, HK + head * DK:HK + (head + 1) * DK]
        v = p_ref[:, 2 * HK + head * DV:2 * HK + (head + 1) * DV]
        g = p_ref[:, 2 * HK + HV + head * DV:2 * HK + HV + (head + 1) * DV]
        s = lax.dot_general(q, k, NT_DIMS, preferred_element_type=F32) * _decay_mask(lgf, lgb, c)
        o = jnp.dot(s.astype(BF16), v, preferred_element_type=F32)
        o_ref[:, head * DV:(head + 1) * DV] = _norm_gate(o, g)
        kf = k.astype(F32)
        k_fwd = (kf * jnp.exp(lgf * (c - 1.0 - pos))).astype(BF16)
        k_bwd = (kf * jnp.exp(lgb * pos)).astype(BF16)
        sf_ref[0, 0, head] = lax.dot_general(k_fwd, v, TN_DIMS, preferred_element_type=F32)
        sb_ref[0, 0, head] = lax.dot_general(k_bwd, v, TN_DIMS, preferred_element_type=F32)


def _ret_specs(seq_len, row0):
    r = row0 // seq_len
    return [pl.BlockSpec((seq_len, DK), lambda b, h: (r + b, h)),
            pl.BlockSpec((seq_len, DK), lambda b, h: (r + b, HK // DK + h)),
            pl.BlockSpec((seq_len, DV), lambda b, h: (r + b, 2 * HK // DV + h)),
            pl.BlockSpec((seq_len, DV), lambda b, h: (r + b, (2 * HK + HV) // DV + h))]


def _ret_prompt(decay, proj):
    state = jax.ShapeDtypeStruct((N_PROMPT, 1, HEADS, DK, DV), F32)
    state_spec = pl.BlockSpec((1, 1, HEADS, DK, DV), lambda b: (b, 0, 0, 0, 0))
    return pl.pallas_call(
        _ret_prompt_kernel,
        grid=(N_PROMPT,),
        in_specs=[pl.BlockSpec(memory_space=pltpu.SMEM),
                  pl.BlockSpec((L_PROMPT, N_PROJ), lambda b: (b, 0))],
        out_specs=[pl.BlockSpec((L_PROMPT, HV), lambda b: (b, 0)), state_spec, state_spec],
        out_shape=[jax.ShapeDtypeStruct((T_PROMPT, HV), BF16), state, state],
        compiler_params=_cparams(("arbitrary",)),
        name="ret_prompt",
    )(decay, proj)


def _ret_sample_kernel(dec_ref, q_ref, k_ref, v_ref, g_ref, s0f_ref, s0b_ref, o_ref,
                       of_scr, ob_scr, sf_scr, sb_scr, dm_scr):
    c = RET_CHUNK
    nc = L_SAMPLE // c
    lgf, lgb = _log_decays(dec_ref, pl.program_id(1))
    dm_scr[...] = _decay_mask(lgf, lgb, c)
    pos = lax.broadcasted_iota(jnp.int32, (c, 1), 0).astype(F32)

    def chunk(ci):
        rows = pl.ds(pl.multiple_of(ci * c, c), c)
        return rows, q_ref[rows, :], k_ref[rows, :], v_ref[rows, :]

    def state_update(s_scr, lg, write_pos, kc, vc):
        kw = (kc.astype(F32) * jnp.exp(lg * write_pos)).astype(BF16)
        s_scr[...] = s_scr[...] * jnp.exp(lg * c) + lax.dot_general(kw, vc, TN_DIMS, preferred_element_type=F32)

    def read_state(s_scr, lg, read_pos, qc):
        qr = (qc.astype(F32) * jnp.exp(lg * read_pos)).astype(BF16)
        return jnp.dot(qr, s_scr[...].astype(BF16), preferred_element_type=F32)

    sf_scr[...] = s0f_ref[0, 0, 0]
    sb_scr[...] = s0b_ref[0, 0, 0]

    def scan(step, carry):
        rows, qc, kc, vc = chunk(nc - 1 - step)
        ob_scr[rows, :] = read_state(sb_scr, lgb, c - pos, qc)
        state_update(sb_scr, lgb, pos, kc, vc)
        rows, qc, kc, vc = chunk(step)
        s = lax.dot_general(qc, kc, NT_DIMS, preferred_element_type=F32) * dm_scr[...]
        o = jnp.dot(s.astype(BF16), vc, preferred_element_type=F32)
        of_scr[rows, :] = o + read_state(sf_scr, lgf, pos + 1.0, qc)
        state_update(sf_scr, lgf, c - 1.0 - pos, kc, vc)
        return carry

    lax.fori_loop(0, nc, scan, 0)

    def finish(ci, carry):
        rows = pl.ds(pl.multiple_of(ci * c, c), c)
        o_ref[rows, :] = _norm_gate(of_scr[rows, :] + ob_scr[rows, :], g_ref[rows, :])
        return carry

    lax.fori_loop(0, nc, finish, 0)


def _ret_sample(decay, proj, s0f, s0b):
    state_spec = pl.BlockSpec((1, 1, 1, DK, DV), lambda b, h: (b, 0, h, 0, 0))
    return pl.pallas_call(
        _ret_sample_kernel,
        grid=(N_SAMPLE, HEADS),
        in_specs=[pl.BlockSpec(memory_space=pltpu.SMEM)] + _ret_specs(L_SAMPLE, T_PROMPT)
        + [state_spec, state_spec],
        out_specs=pl.BlockSpec((L_SAMPLE, DV), lambda b, h: (b, h)),
        out_shape=jax.ShapeDtypeStruct((T_SAMPLE, HV), BF16),
        scratch_shapes=[pltpu.VMEM((L_SAMPLE, DV), F32), pltpu.VMEM((L_SAMPLE, DV), F32),
                        pltpu.VMEM((DK, DV), F32), pltpu.VMEM((DK, DV), F32),
                        pltpu.VMEM((RET_CHUNK, RET_CHUNK), F32)],
        compiler_params=_cparams(("arbitrary", "arbitrary")),
        name="ret_sample",
    )(decay, proj, proj, proj, proj, s0f, s0b)


def _store_token_major(ref, val, tm):
    for s in range(ROW_CHUNKS):
        ref[pl.ds(s, tm, stride=ROW_CHUNKS), :] = val[:, s * LANES:(s + 1) * LANES]


def _load_token_major(ref, tm, lead=()):
    return jnp.concatenate([ref[lead + (pl.ds(s, tm, stride=ROW_CHUNKS), slice(None))]
                            for s in range(ROW_CHUNKS)], axis=1)


def _post_mixer(x, y, m, g2_ref, wr_ref, br_ref, x_out_ref, h_out_ref, ti_ref, tg_ref, tm):
    x1 = x + m[2:3] * y
    h = _rms(x1) * (g2_ref[0] * (1.0 + m[4:5])) + m[3:4]
    x_out_ref[...] = x1
    _store_token_major(h_out_ref, h, tm)
    w = wr_ref[...]
    w_hi = w.astype(BF16)
    w_lo = (w - w_hi.astype(F32)).astype(BF16)
    h_hi = h.astype(BF16)
    h_lo = (h - h_hi.astype(F32)).astype(BF16)
    dot = functools.partial(lax.dot_general, dimension_numbers=NT_DIMS, preferred_element_type=F32)
    cur = dot(w_hi, h_hi) + dot(w_hi, h_lo) + dot(w_lo, h_hi) + br_ref[...]
    ie = lax.broadcasted_iota(jnp.int32, (N_EXP, tm), 0).astype(F32)
    vals, idxs = [], []
    for _ in range(TOP_K):
        top = jnp.max(cur, axis=0, keepdims=True)
        idx = jnp.min(jnp.where(cur == top, ie, float(N_EXP)), axis=0, keepdims=True)
        vals.append(top)
        idxs.append(idx)
        cur = jnp.where(ie == idx, -jnp.inf, cur)
    ex = [jnp.exp(v - vals[0]) for v in vals]
    den = ex[0] + ex[1] + ex[2] + ex[3]
    pad = jnp.zeros((SUBLANES - TOP_K, tm), F32)
    ti_ref[...] = jnp.concatenate(idxs + [pad], axis=0).astype(jnp.int32)
    tg_ref[...] = jnp.concatenate([e / den for e in ex] + [pad], axis=0)


def _post_in_specs(layer):
    return [pl.BlockSpec((1, 1, 6, D), lambda i: (layer, _cond_of_tile(i, TM), 0, 0)),
            pl.BlockSpec((1, 1, D), lambda i: (layer, 0, 0)),
            pl.BlockSpec((1, N_EXP, D), lambda i: (layer, 0, 0)),
            pl.BlockSpec((1, N_EXP, 1), lambda i: (layer, 0, 0))]


_POST_OUT_SPECS = [pl.BlockSpec((TM, D), lambda i: (i, 0)),
                   pl.BlockSpec((TM * ROW_CHUNKS, LANES), lambda i: (i, 0)),
                   pl.BlockSpec((SUBLANES, TM), lambda i: (0, i)),
                   pl.BlockSpec((SUBLANES, TM), lambda i: (0, i))]
_POST_OUT_SHAPES = [jax.ShapeDtypeStruct((T, D), F32),
                    jax.ShapeDtypeStruct((T * ROW_CHUNKS, LANES), F32),
                    jax.ShapeDtypeStruct((SUBLANES, T), jnp.int32),
                    jax.ShapeDtypeStruct((SUBLANES, T), F32)]


def _ret_out_kernel(ogp_ref, ogs_ref, w_ref, xp_ref, xs_ref, mod_ref, g2_ref, wr_ref, br_ref,
                    x_out_ref, h_out_ref, ti_ref, tg_ref):
    i = pl.program_id(0)
    og = jnp.where(i < T_PROMPT // TM, ogp_ref[...], ogs_ref[...])
    y = jnp.dot(og, w_ref[...], preferred_element_type=F32)
    _post_mixer(_pick_x(i, TM, xp_ref, xs_ref), y, mod_ref[0, 0], g2_ref, wr_ref[0], br_ref[0],
                x_out_ref, h_out_ref, ti_ref, tg_ref, TM)


def _ret_out(og_p, og_s, w_out, xp, xs, mod4, g_norm2, wr_t, br_c):
    n_p = T_PROMPT // TM
    return pl.pallas_call(
        _ret_out_kernel,
        grid=(T // TM,),
        in_specs=[pl.BlockSpec((TM, HV), lambda i: (jnp.minimum(i, n_p - 1), 0)),
                  pl.BlockSpec((TM, HV), lambda i: (jnp.maximum(i - n_p, 0), 0)),
                  pl.BlockSpec((HV, D), lambda i: (0, 0))]
        + _x_specs(TM) + _post_in_specs(0),
        out_specs=_POST_OUT_SPECS,
        out_shape=_POST_OUT_SHAPES,
        compiler_params=_cparams(("arbitrary",)),
        name="ret_out",
    )(og_p, og_s, w_out, xp, xs, mod4, g_norm2, wr_t, br_c)


def _route_a_kernel(ti_ref, dest_ref, meta_ref):
    tt, tm = MOE_TILE, MOE_BLK
    ti = ti_ref[...]
    ie = lax.broadcasted_iota(jnp.int32, (N_EXP, tt), 0)
    onehots = [(ie == ti[k:k + 1]).astype(F32) for k in range(TOP_K)]
    oh = onehots[0] + onehots[1] + onehots[2] + onehots[3]
    ch = 512
    upper = (lax.broadcasted_iota(jnp.int32, (ch, ch), 0)
             < lax.broadcasted_iota(jnp.int32, (ch, ch), 1)).astype(BF16)
    carry = jnp.zeros((N_EXP, 1), F32)
    cums = []
    for c in range(tt // ch):
        blk = oh[:, c * ch:(c + 1) * ch]
        cums.append(jnp.dot(blk.astype(BF16), upper, preferred_element_type=F32) + carry)
        carry = carry + jnp.sum(blk, axis=1, keepdims=True)
    cum = jnp.concatenate(cums, axis=1)
    cnt = carry
    nb = jnp.floor((cnt + (tm - 1.0)) * (1.0 / tm))
    lower = (lax.broadcasted_iota(jnp.int32, (N_EXP, N_EXP), 1)
             < lax.broadcasted_iota(jnp.int32, (N_EXP, N_EXP), 0)).astype(BF16)
    offb = jnp.dot(lower, jnp.broadcast_to(nb, (N_EXP, LANES)).astype(BF16),
                   preferred_element_type=F32)[:, :1]
    off = offb * tm
    base = off + cum
    dests = [jnp.sum(onehots[k] * base, axis=0, keepdims=True) for k in range(TOP_K)]
    dests.append(jnp.zeros((SUBLANES - TOP_K, tt), F32))
    dest_ref[0] = jnp.concatenate(dests, axis=0).astype(jnp.int32)
    nused = jnp.sum(nb, axis=0, keepdims=True)
    jl = lax.broadcasted_iota(jnp.int32, (N_EXP, META_LANES), 1).astype(F32)
    jc = jnp.minimum(jl, nused - 1.0)
    be = jnp.minimum(jnp.sum(((offb + nb) <= jc).astype(F32), axis=0, keepdims=True), N_EXP - 1.0)
    ief = lax.broadcasted_iota(jnp.int32, (N_EXP, META_LANES), 0).astype(F32)
    end_row = jnp.sum(jnp.where(ief == be, off + cnt, 0.0), axis=0, keepdims=True)
    nvalid = jnp.clip(end_row - jl[:1] * tm, 0.0, float(tm))
    nvalid = jnp.where(jl[:1] < nused, nvalid, 0.0)
    run_end = jnp.sum(jnp.where(ief == be, offb + nb, 0.0), axis=0, keepdims=True)
    nxt = jnp.sum(((offb + nb) <= run_end).astype(F32), axis=0, keepdims=True)
    nxt = jnp.where(run_end < nused, nxt, -1.0)
    meta = jnp.concatenate([be, nvalid, jnp.broadcast_to(nused, (1, META_LANES)), nxt,
                            jnp.zeros((SUBLANES - 4, META_LANES), F32)], axis=0)
    meta_ref[0] = meta.astype(jnp.int32)


def _route_a(ti):
    return pl.pallas_call(
        _route_a_kernel,
        grid=(N_MOE_TILES,),
        in_specs=[pl.BlockSpec((SUBLANES, MOE_TILE), lambda i: (0, i))],
        out_specs=[pl.BlockSpec((1, SUBLANES, MOE_TILE), lambda i: (i, 0, 0)),
                   pl.BlockSpec((1, SUBLANES, META_LANES), lambda i: (i, 0, 0))],
        out_shape=[jax.ShapeDtypeStruct((N_MOE_TILES, SUBLANES, MOE_TILE), jnp.int32),
                   jax.ShapeDtypeStruct((N_MOE_TILES, SUBLANES, META_LANES), jnp.int32)],
        compiler_params=_cparams(("arbitrary",)),
        name="route_a",
    )(ti)


ROUTE_STEPS = N_EXP // N_MOE_TILES


def _cast_route_kernel(wu_ref, wd_ref, dest_ref, meta_ref, ou_ref, od_ref, slot_ref):
    ou_ref[...] = wu_ref[0].astype(BF16)
    od_ref[...] = wd_ref[0].astype(BF16)
    part = pl.program_id(0) % ROUTE_STEPS
    group = SUBLANES

    def per_block(j, carry):
        def pad(s, c2):
            slot_ref[j * MOE_BLK + s] = MOE_TILE * ROW_CHUNKS
            return c2
        lax.fori_loop(meta_ref[0, 1, j], MOE_BLK, pad, 0)
        return carry

    blocks = MOE_NBLK // ROUTE_STEPS
    lax.fori_loop(part * blocks, (part + 1) * blocks, per_block, 0)

    tokens = MOE_TILE // ROUTE_STEPS

    def per_group(tg, carry):
        t0 = part * tokens + tg * group
        rows = [dest_ref.at[0, k, pl.ds(t0, group)] for k in range(TOP_K)]
        tok_code = t0 * ((1 << CODE_SHIFT) + ROW_CHUNKS)
        for u0 in range(0, group, 2):
            loaded = [(k, u, rows[k][u]) for u in range(u0, u0 + 2) for k in range(TOP_K)]
            for k, u, d in loaded:
                slot_ref[d] = tok_code + ((k * MOE_TILE + u) << CODE_SHIFT) + u * ROW_CHUNKS
        return carry

    lax.fori_loop(0, tokens // group, per_group, 0)


def _cast_route(w_up, w_down, layer, dest, meta):
    tile = lambda e: e // ROUTE_STEPS
    weight_in = lambda w: pl.BlockSpec((1, 1) + w.shape[2:], lambda e: (layer, e, 0, 0))
    weight_out = lambda w: pl.BlockSpec((1,) + w.shape[2:], lambda e: (e, 0, 0))
    return pl.pallas_call(
        _cast_route_kernel,
        grid=(N_EXP,),
        in_specs=[weight_in(w_up), weight_in(w_down),
                  pl.BlockSpec((1, SUBLANES, MOE_TILE), lambda e: (tile(e), 0, 0), memory_space=pltpu.SMEM),
                  pl.BlockSpec((1, SUBLANES, META_LANES), lambda e: (tile(e), 0, 0),
                               memory_space=pltpu.SMEM)],
        out_specs=[weight_out(w_up), weight_out(w_down),
                   pl.BlockSpec((MOE_SLOTS,), lambda e: (tile(e),), memory_space=pltpu.SMEM)],
        out_shape=[jax.ShapeDtypeStruct(w_up.shape[1:], BF16), jax.ShapeDtypeStruct(w_down.shape[1:], BF16),
                   jax.ShapeDtypeStruct((N_MOE_TILES * MOE_SLOTS,), jnp.int32)],
        compiler_params=_cparams(("arbitrary",)),
        name="cast_route",
    )(w_up, w_down, dest, meta)


def _moe_kernel(be_ref, nu_ref, nx_ref, slot_ref, gate_ref, bu_ref, bd_ref, h_hbm, wu_hbm, wd_hbm,
                out_hbm, g0_scr, g1_scr, y0_scr, y1_scr, wu_scr, wd_scr, h_scr, out_scr, sem, tile_sem):
    i = pl.program_id(0)
    tm = MOE_BLK
    nused = nu_ref[i]
    base = i * MOE_NBLK

    def weight_copies(e, buf):
        return (pltpu.make_async_copy(wu_hbm.at[e], wu_scr.at[buf], sem.at[0, buf]),
                pltpu.make_async_copy(wd_hbm.at[e], wd_scr.at[buf], sem.at[1, buf]))
    tile_rows = MOE_TILE * ROW_CHUNKS
    h_copy = pltpu.make_async_copy(h_hbm.at[pl.ds(i * tile_rows, tile_rows)], h_scr, tile_sem.at[0])
    out_copy = pltpu.make_async_copy(out_scr, out_hbm.at[i], tile_sem.at[1])

    def row_slice(ii):
        return slice(ii * ROW_CHUNKS, (ii + 1) * ROW_CHUNKS)

    def gather(blk, g_scr, lo, hi):
        codes = slot_ref.at[pl.ds(blk * tm, tm)]
        for ii in range(lo, hi):
            off = pl.multiple_of(codes[ii] & (tile_rows - 1), ROW_CHUNKS)
            g_scr[row_slice(ii), :] = h_scr[pl.ds(off, ROW_CHUNKS), :]

    def scatter(blk, y_scr, lo, hi):
        batch = 8
        codes = slot_ref.at[pl.ds(blk * tm, tm)]
        for i0 in range(lo, hi, batch):
            pending = []
            for ii in range(i0, i0 + batch):
                code = codes[ii]
                gate = gate_ref[code >> CODE_SHIFT]
                off = code & ((1 << CODE_SHIFT) - 1)
                rows = pl.ds(pl.multiple_of(off, ROW_CHUNKS), ROW_CHUNKS)
                pending.append((rows, out_scr[rows, :] + gate * y_scr[row_slice(ii), :]))
            for rows, val in pending:
                out_scr[rows, :] = val

    h_copy.start()
    for copy in weight_copies(be_ref[base], 0):
        copy.start()
    out_scr[...] = jnp.zeros(out_scr.shape, F32)
    y1_scr[...] = jnp.zeros(y1_scr.shape, F32)
    h_copy.wait()
    gather(0, g0_scr, 0, tm)

    def step(j, run, g_cur, g_nxt, y_cur, y_prv):
        e = be_ref[base + j]
        jp = jnp.maximum(j - 1, 0)
        jn = jnp.minimum(j + 1, nused - 1)
        first = jnp.logical_or(j == 0, be_ref[base + jp] != e)
        run = run + jnp.where(jnp.logical_and(first, j > 0), 1, 0)
        buf = run & 1

        @pl.when(first)
        def _():
            for copy in weight_copies(e, buf):
                copy.wait()
            nxt = nx_ref[base + j]

            @pl.when(nxt >= 0)
            def _():
                for copy in weight_copies(nxt, 1 - buf):
                    copy.start()

        def compute(wbuf):
            a = _load_token_major(g_cur, tm).astype(BF16)
            up = jnp.dot(a, wu_scr[wbuf], preferred_element_type=F32) + bu_ref[e]
            scatter(jp, y_prv, 0, tm)
            gather(jn, g_nxt, 0, tm)
            glu = jnp.minimum(up[:, :D_EXP], SWIGLU_LIMIT)
            lin = jnp.clip(up[:, D_EXP:], -SWIGLU_LIMIT, SWIGLU_LIMIT)
            act = glu * _sigmoid(SWIGLU_ALPHA * glu) * (lin + 1.0)
            y = jnp.dot(act.astype(BF16), wd_scr[wbuf], preferred_element_type=F32) + bd_ref[e]
            _store_token_major(y_cur, y, tm)

        for wbuf in range(2):
            pl.when(buf == wbuf)(functools.partial(compute, wbuf))

        @pl.when(j == nused - 1)
        def _():
            scatter(j, y_cur, 0, tm)

        return run

    def pair(jj, run):
        run = step(2 * jj, run, g0_scr, g1_scr, y0_scr, y1_scr)
        return lax.cond(2 * jj + 1 < nused,
                        lambda r: step(2 * jj + 1, r, g1_scr, g0_scr, y1_scr, y0_scr),
                        lambda r: r, run)

    lax.fori_loop(0, (nused + 1) // 2, pair, jnp.int32(0))
    out_copy.start()
    out_copy.wait()


def _moe(be, nu, nx, slot, gates, h_tm, w_up, b_up, w_down, b_down):
    row_buf = pltpu.VMEM((MOE_BLK * ROW_CHUNKS, LANES), F32)
    grid_spec = pltpu.PrefetchScalarGridSpec(
        num_scalar_prefetch=3,
        grid=(N_MOE_TILES,),
        in_specs=[
            pl.BlockSpec((MOE_SLOTS,), lambda i, *_: (i,), memory_space=pltpu.SMEM),
            pl.BlockSpec((TOP_K * MOE_TILE,), lambda i, *_: (i,), memory_space=pltpu.SMEM),
            pl.BlockSpec((N_EXP, 1, 2 * D_EXP), lambda i, *_: (0, 0, 0)),
            pl.BlockSpec((N_EXP, 1, D), lambda i, *_: (0, 0, 0)),
            pl.BlockSpec(memory_space=pl.ANY),
            pl.BlockSpec(memory_space=pl.ANY),
            pl.BlockSpec(memory_space=pl.ANY),
        ],
        out_specs=pl.BlockSpec(memory_space=pl.ANY),
        scratch_shapes=[row_buf, row_buf, row_buf, row_buf,
                        pltpu.VMEM((2, D, 2 * D_EXP), BF16), pltpu.VMEM((2, D_EXP, D), BF16),
                        pltpu.VMEM((MOE_TILE * ROW_CHUNKS, LANES), F32),
                        pltpu.VMEM(((MOE_TILE + 1) * ROW_CHUNKS, LANES), F32),
                        pltpu.SemaphoreType.DMA((2, 2)), pltpu.SemaphoreType.DMA((2,))],
    )
    return pl.pallas_call(
        _moe_kernel,
        grid_spec=grid_spec,
        out_shape=jax.ShapeDtypeStruct((N_MOE_TILES, (MOE_TILE + 1) * ROW_CHUNKS, LANES), F32),
        compiler_params=_cparams(("arbitrary",)),
        name="moe",
    )(be, nu, nx, slot, gates, b_up.reshape(N_EXP, 1, 2 * D_EXP), b_down.reshape(N_EXP, 1, D),
      h_tm, w_up, w_down)


def _moe_layer(layer, ti, tg, h_tm, w_up, b_up, w_down, b_down):
    dest, meta = _route_a(ti)
    w_up_bf16, w_down_bf16, slot = _cast_route(w_up, w_down, layer, dest, meta)
    be, nx = (meta[:, r, :MOE_NBLK].reshape(-1) for r in (0, 3))
    nu = meta[:, 2, 0]
    gates = tg[:TOP_K].reshape(TOP_K, N_MOE_TILES, MOE_TILE).transpose(1, 0, 2).reshape(-1)
    return _moe(be, nu, nx, slot, gates, h_tm, w_up_bf16, b_up[layer], w_down_bf16, b_down[layer])


_MOE_OUT_SPEC = pl.BlockSpec((1, TM * ROW_CHUNKS, LANES),
                             lambda i: (i // (MOE_TILE // TM), i % (MOE_TILE // TM), 0))


def _conv_kernel(x_ref, moe_ref, modp_ref, g1_ref, w1_ref, b1_ref, wdw_ref, bdw_ref, lng_ref, lnb_ref,
                 w2_ref, b2_ref, mod_ref, g2_ref, wr_ref, br_ref,
                 x_out_ref, h_out_ref, ti_ref, tg_ref, pad_scr, conv_scr, shift_scr):
    i = pl.program_id(0)
    mp = modp_ref[0, 0]
    m = mod_ref[0, 0]
    x = x_ref[...] + mp[5:6] * _load_token_major(moe_ref, TM, (0,))
    h = (_rms(x) * (g1_ref[0] * (1.0 + m[1:2])) + m[0:1]).astype(BF16)
    ag = jnp.dot(h, w1_ref[0], preferred_element_type=F32) + b1_ref[0]
    u = ag[:, :D] * _sigmoid(ag[:, D:])
    is_prompt = i < T_PROMPT // TM

    def fill(seg):
        pitch = seg + CONV_PAD
        for s in range(TM // seg):
            pad_scr[s * pitch:s * pitch + CONV_PAD, :] = jnp.zeros((CONV_PAD, D), F32)
            pad_scr[s * pitch + CONV_PAD:(s + 1) * pitch, :] = u[s * seg:(s + 1) * seg, :]
        end = (TM // seg) * pitch
        pad_scr[end:end + CONV_PAD, :] = jnp.zeros((CONV_PAD, D), F32)

    @pl.when(is_prompt)
    def _():
        fill(L_PROMPT)

    @pl.when(jnp.logical_not(is_prompt))
    def _():
        fill(GRID_W)

    per_seq = L_PROMPT // CONV_CHUNK
    halo = CONV_CHUNK + 2 * CONV_PAD

    def conv_chunk(c, carry):
        base = jnp.where(is_prompt, (c // per_seq) * (L_PROMPT + CONV_PAD) + (c % per_seq) * CONV_CHUNK,
                         c * (GRID_W + CONV_PAD))
        base = pl.multiple_of(base, SUBLANES)
        out_rows = pl.ds(pl.multiple_of(c * CONV_CHUNK, CONV_CHUNK), CONV_CHUNK)
        for gl in range(D // LANES):
            lanes = slice(gl * LANES, (gl + 1) * LANES)
            blk = pad_scr[pl.ds(base, halo), lanes]
            span = halo - SUBLANES
            for r in range(SUBLANES):
                shift_scr[gl * SUBLANES + r] = blk[r:r + span, :]
            acc = jnp.zeros((CONV_CHUNK, LANES), F32)
            for tap in range(CONV_W):
                lo = CONV_PAD - CONV_W // 2 + tap
                al = lo // SUBLANES * SUBLANES
                acc = acc + (shift_scr[gl * SUBLANES + lo % SUBLANES, al:al + CONV_CHUNK, :]
                             * wdw_ref[0, tap:tap + 1, lanes])
            conv_scr[out_rows, lanes] = acc + bdw_ref[0, :, lanes]
        return carry

    lax.fori_loop(0, TM // CONV_CHUNK, conv_chunk, 0)
    uc = conv_scr[...]
    mu = jnp.mean(uc, axis=-1, keepdims=True)
    var = jnp.mean(jnp.square(uc - mu), axis=-1, keepdims=True)
    z = (uc - mu) * lax.rsqrt(var + EPS) * lng_ref[0] + lnb_ref[0]
    z = (z * _sigmoid(z)).astype(BF16)
    y = jnp.dot(z, w2_ref[0], preferred_element_type=F32) + b2_ref[0]
    _post_mixer(x, y, m, g2_ref, wr_ref[0], br_ref[0], x_out_ref, h_out_ref, ti_ref, tg_ref, TM)


def _conv_layer(x1, moe0, mod4, g_norm1, w_pw1, b_pw1, w_dw, b_dw, ln_g, ln_b, w_pw2, b_pw2,
                g_norm2, wr_t, br_c):
    def full(shape):
        return pl.BlockSpec(shape, lambda i: (0,) * len(shape))

    pad_rows = (TM // GRID_W) * (GRID_W + CONV_PAD) + CONV_PAD
    return pl.pallas_call(
        _conv_kernel,
        grid=(T // TM,),
        in_specs=[pl.BlockSpec((TM, D), lambda i: (i, 0)),
                  _MOE_OUT_SPEC,
                  pl.BlockSpec((1, 1, 6, D), lambda i: (0, _cond_of_tile(i, TM), 0, 0)),
                  pl.BlockSpec((1, 1, D), lambda i: (1, 0, 0)),
                  full((1, D, 2 * D)), full((1, 1, 2 * D)), full((1, CONV_W, D)), full((1, 1, D)),
                  full((1, 1, D)), full((1, 1, D)), full((1, D, D)), full((1, 1, D))]
        + _post_in_specs(1),
        out_specs=_POST_OUT_SPECS,
        out_shape=_POST_OUT_SHAPES,
        scratch_shapes=[pltpu.VMEM((pad_rows, D), F32), pltpu.VMEM((TM, D), F32),
                        pltpu.VMEM((D // LANES * SUBLANES, CONV_CHUNK + 2 * CONV_PAD - SUBLANES, LANES), F32)],
        compiler_params=_cparams(("arbitrary",)),
        name="conv",
    )(x1, moe0, mod4, g_norm1, w_pw1, b_pw1.reshape(1, 1, 2 * D), w_dw, b_dw.reshape(1, 1, D),
      ln_g.reshape(1, 1, D), ln_b.reshape(1, 1, D), w_pw2, b_pw2.reshape(1, 1, D),
      mod4, g_norm2, wr_t, br_c)


def _final_kernel(x_ref, moe_ref, mod_ref, g_ref, yp_ref, ys_ref):
    i = pl.program_id(0)
    x = x_ref[...] + mod_ref[0, 0][5:6] * _load_token_major(moe_ref, TM, (0,))
    y = _rms(x) * g_ref[...]

    @pl.when(i < T_PROMPT // TM)
    def _():
        yp_ref[...] = y

    @pl.when(i >= T_PROMPT // TM)
    def _():
        ys_ref[...] = y


def _final(x, moe1, mod4, g_final):
    n_p = T_PROMPT // TM
    return pl.pallas_call(
        _final_kernel,
        grid=(T // TM,),
        in_specs=[pl.BlockSpec((TM, D), lambda i: (i, 0)),
                  _MOE_OUT_SPEC,
                  pl.BlockSpec((1, 1, 6, D), lambda i: (1, _cond_of_tile(i, TM), 0, 0)),
                  pl.BlockSpec((1, D), lambda i: (0, 0))],
        out_specs=[pl.BlockSpec((TM, D), lambda i: (jnp.minimum(i, n_p - 1), 0)),
                   pl.BlockSpec((TM, D), lambda i: (jnp.maximum(i - n_p, 0), 0))],
        out_shape=[jax.ShapeDtypeStruct((T_PROMPT, D), F32), jax.ShapeDtypeStruct((T_SAMPLE, D), F32)],
        compiler_params=_cparams(("arbitrary",)),
        name="final",
    )(x, moe1, mod4, g_final)


def kernel(x_prompt, x_sample, c, state_ret_fwd, state_ret_bwd, c_ctx, w_ada, b_ada, g_norm1, g_norm2,
           ret_w_in, ret_decay, ret_w_out, conv_w_pw1, conv_b_pw1, conv_w_dw, conv_b_dw, conv_ln_g,
           conv_ln_b, conv_w_pw2, conv_b_pw2, moe_w_router, moe_b_router, moe_w_up, moe_b_up,
           moe_w_down, moe_b_down, g_final):
    xp = x_prompt.reshape(T_PROMPT, D)
    xs = x_sample.reshape(T_SAMPLE, D)
    cond = jnp.concatenate([c_ctx[None, :], c, jnp.zeros((SUBLANES - 1 - N_SAMPLE, D), F32)], axis=0)
    mod = _ada(cond.T, w_ada, b_ada)
    mod4 = mod.reshape(mod.shape[0], SUBLANES, 6, D)
    wr_t = jnp.swapaxes(moe_w_router, 1, 2)
    br_c = moe_b_router[:, :, None]
    gn1 = g_norm1[:, None, :]
    gn2 = g_norm2[:, None, :]

    proj = _proj(xp, xs, g_norm1[0:1], mod4, ret_w_in[0].astype(BF16))
    og_p, new_f, new_b = _ret_prompt(ret_decay[0], proj)
    og_s = _ret_sample(ret_decay[0], proj, state_ret_fwd, state_ret_bwd)
    x1, h_tm, ti, tg = _ret_out(og_p, og_s, ret_w_out[0].astype(BF16), xp, xs, mod4, gn2, wr_t, br_c)
    moe0 = _moe_layer(0, ti, tg, h_tm, moe_w_up, moe_b_up, moe_w_down, moe_b_down)

    x2, h_tm, ti, tg = _conv_layer(x1, moe0, mod4, gn1, conv_w_pw1.astype(BF16), conv_b_pw1, conv_w_dw,
                                   conv_b_dw, conv_ln_g, conv_ln_b, conv_w_pw2.astype(BF16), conv_b_pw2,
                                   gn2, wr_t, br_c)
    moe1 = _moe_layer(1, ti, tg, h_tm, moe_w_up, moe_b_up, moe_w_down, moe_b_down)

    y_p, y_s = _final(x2, moe1, mod4, g_final[None, :])
    return (y_p.reshape(N_PROMPT, L_PROMPT, D), y_s.reshape(N_SAMPLE, L_SAMPLE, D), new_f, new_b)
```

```python
import functools

import numpy as np
import jax
import jax.numpy as jnp
from jax import lax
from jax.experimental import pallas as pl
from jax.experimental.pallas import tpu as pltpu

F32 = jnp.float32
BF16 = jnp.bfloat16

D = 1024
N_PROMPT, L_PROMPT = 32, 256
N_SAMPLE, L_SAMPLE = 2, 4096
T_PROMPT = N_PROMPT * L_PROMPT
T_SAMPLE = N_SAMPLE * L_SAMPLE
T = T_PROMPT + T_SAMPLE
GRID_W = 64
HEADS, DK, DV = 4, 256, 512
HK, HV = HEADS * DK, HEADS * DV
N_PROJ = 2 * HK + 2 * HV
ROPE_BASE = 10000.0
CONV_W = 31
CONV_PAD = 16
CONV_CHUNK = 64
N_EXP, TOP_K, D_EXP = 32, 4, 1024
SWIGLU_LIMIT, SWIGLU_ALPHA = 7.0, 1.702
EPS = 1e-6

LANES = 128
SUBLANES = 8
ROW_CHUNKS = D // LANES
VMEM_LIMIT = 56 * 1024 * 1024

TM = 512
MOE_TILE = 4096
MOE_BLK = 256
MOE_NBLK = MOE_TILE * TOP_K // MOE_BLK + N_EXP
MOE_SLOTS = MOE_NBLK * MOE_BLK
META_LANES = 2 * LANES
CODE_SHIFT = 16
N_MOE_TILES = T // MOE_TILE
RET_CHUNK = 256

NT_DIMS = (((1,), (1,)), ((), ()))
TN_DIMS = (((0,), (0,)), ((), ()))


def _cparams(sem):
    return pltpu.CompilerParams(dimension_semantics=sem, vmem_limit_bytes=VMEM_LIMIT)


def _cond_of_tile(i, tm):
    return jnp.maximum((i * tm) // L_SAMPLE - 1, 0)


def _sigmoid(x):
    return 0.5 * jnp.tanh(0.5 * x) + 0.5


def _rms(x):
    return x * lax.rsqrt(jnp.mean(x * x, axis=-1, keepdims=True) + EPS)


def _pick_x(i, tm, xp_ref, xs_ref):
    return jnp.where(i < T_PROMPT // tm, xp_ref[...], xs_ref[...])


def _x_specs(tm):
    n_p = T_PROMPT // tm
    return [pl.BlockSpec((tm, D), lambda i, *_: (jnp.minimum(i, n_p - 1), 0)),
            pl.BlockSpec((tm, D), lambda i, *_: (jnp.maximum(i - n_p, 0), 0))]


def _ada_kernel(ct_ref, w_ref, b_ref, o_ref):
    ct = ct_ref[...]
    s = ct * _sigmoid(ct)
    w = w_ref[0]
    rows = [jnp.sum(w * s[:, r:r + 1], axis=0, keepdims=True) for r in range(3)]
    rows.append(jnp.zeros((SUBLANES - 3, w.shape[1]), F32))
    o_ref[0] = jnp.concatenate(rows, axis=0) + b_ref[0]


def _ada(ct, w_ada, b_ada):
    depth, _, n = w_ada.shape
    tn = 1536
    return pl.pallas_call(
        _ada_kernel,
        grid=(depth, n // tn),
        in_specs=[pl.BlockSpec((D, SUBLANES), lambda l, j: (0, 0)),
                  pl.BlockSpec((1, D, tn), lambda l, j: (l, 0, j)),
                  pl.BlockSpec((1, 1, tn), lambda l, j: (l, 0, j))],
        out_specs=pl.BlockSpec((1, SUBLANES, tn), lambda l, j: (l, 0, j)),
        out_shape=jax.ShapeDtypeStruct((depth, SUBLANES, n), F32),
        compiler_params=_cparams(("arbitrary", "arbitrary")),
        name="ada",
    )(ct, w_ada, b_ada.reshape(depth, 1, n))


def _rope_tables():
    t = np.arange(L_SAMPLE)
    row = (t // GRID_W).astype(np.float32)
    col = (t % GRID_W).astype(np.float32)
    nf = DK // 4
    inv = (np.float32(ROPE_BASE) ** (-np.arange(nf, dtype=np.float32) / np.float32(nf))).astype(np.float32)
    cos, sin = [], []
    for pos in (row, col):
        ang = (pos[:, None] * inv[None, :]).astype(np.float32)
        c, s = np.cos(ang).astype(np.float32), np.sin(ang).astype(np.float32)
        cos += [c, c]
        sin += [-s, s]
    return np.concatenate(cos, axis=1), np.concatenate(sin, axis=1)


def _proj_kernel(xp_ref, xs_ref, g_ref, mod_ref, w_ref, cos_ref, sin_ref, o_ref):
    i = pl.program_id(0)
    m = mod_ref[0, 0]
    h = _rms(_pick_x(i, TM, xp_ref, xs_ref)) * (g_ref[...] * (1.0 + m[1:2]))
    h = (h + m[0:1]).astype(BF16)
    is_sample = i >= T_PROMPT // TM
    cos = jnp.where(is_sample, cos_ref[...], 1.0)
    sin = jnp.where(is_sample, sin_ref[...], 0.0)
    for col in range(N_PROJ // D):
        acc = jnp.dot(h, w_ref[:, col * D:(col + 1) * D], preferred_element_type=F32)
        if col == 1:
            acc = acc * DK ** -0.5
        if col < 2:
            for c in range(D // LANES):
                a = acc[:, c * LANES:(c + 1) * LANES]
                p = (c % 2) * LANES
                r = a * cos[:, p:p + LANES] + pltpu.roll(a, LANES // 2, 1) * sin[:, p:p + LANES]
                o_ref[:, col * D + c * LANES:col * D + (c + 1) * LANES] = r.astype(BF16)
        else:
            o_ref[:, col * D:(col + 1) * D] = acc.astype(BF16)


def _proj(xp, xs, g1, mod4, w_in):
    cos, sin = _rope_tables()
    n_p = T_PROMPT // TM
    n_s = L_SAMPLE // TM
    tab_spec = pl.BlockSpec((TM, DK), lambda i: (jnp.maximum(i - n_p, 0) % n_s, 0))
    return pl.pallas_call(
        _proj_kernel,
        grid=(T // TM,),
        in_specs=_x_specs(TM) + [
            pl.BlockSpec((1, D), lambda i: (0, 0)),
            pl.BlockSpec((1, 1, 6, D), lambda i: (0, _cond_of_tile(i, TM), 0, 0)),
            pl.BlockSpec((D, N_PROJ), lambda i: (0, 0)),
            tab_spec, tab_spec],
        out_specs=pl.BlockSpec((TM, N_PROJ), lambda i: (i, 0)),
        out_shape=jax.ShapeDtypeStruct((T, N_PROJ), BF16),
        compiler_params=_cparams(("arbitrary",)),
        name="proj",
    )(xp, xs, g1, mod4, w_in, jnp.asarray(cos), jnp.asarray(sin))


def _log_decays(dec_ref, head):
    out = []
    for direction in range(2):
        d = jnp.full((1, 1), dec_ref[direction, head], F32)
        out.append(jnp.minimum(d, 0.0) - jnp.log(1.0 + jnp.exp(-jnp.abs(d))))
    return out


def _decay_mask(lgf, lgb, c):
    ii = lax.broadcasted_iota(jnp.int32, (c, c), 0)
    jj = lax.broadcasted_iota(jnp.int32, (c, c), 1)
    diff = (ii - jj).astype(F32)
    fwd = jnp.where(diff >= 0, jnp.exp(lgf * jnp.maximum(diff, 0.0)), 0.0)
    bwd = jnp.where(diff <= 0, jnp.exp(lgb * jnp.maximum(-diff, 0.0)), 0.0)
    return fwd + bwd


def _norm_gate(o, g):
    g = g.astype(F32)
    return (_rms(o) * (g * _sigmoid(g))).astype(BF16)


def _ret_prompt_kernel(dec_ref, p_ref, o_ref, sf_ref, sb_ref):
    c = L_PROMPT
    pos = lax.broadcasted_iota(jnp.int32, (c, 1), 0).astype(F32)
    for head in range(HEADS):
        lgf, lgb = _log_decays(dec_ref, head)
        q = p_ref[:, head * DK:(head + 1) * DK]
        k = p_ref[:, HK + head * DK:HK + (head + 1) * DK]
        v = p_ref[:, 2 * HK + head * DV:2 * HK + (head + 1) * DV]
        g = p_ref[:, 2 * HK + HV + head * DV:2 * HK + HV + (head + 1) * DV]
        s = lax.dot_general(q, k, NT_DIMS, preferred_element_type=F32) * _decay_mask(lgf, lgb, c)
        o = jnp.dot(s.astype(BF16), v, preferred_element_type=F32)
        o_ref[:, head * DV:(head + 1) * DV] = _norm_gate(o, g)
        kf = k.astype(F32)
        k_fwd = (kf * jnp.exp(lgf * (c - 1.0 - pos))).astype(BF16)
        k_bwd = (kf * jnp.exp(lgb * pos)).astype(BF16)
        sf_ref[0, 0, head] = lax.dot_general(k_fwd, v, TN_DIMS, preferred_element_type=F32)
        sb_ref[0, 0, head] = lax.dot_general(k_bwd, v, TN_DIMS, preferred_element_type=F32)


def _ret_specs(seq_len, row0):
    r = row0 // seq_len
    return [pl.BlockSpec((seq_len, DK), lambda b, h: (r + b, h)),
            pl.BlockSpec((seq_len, DK), lambda b, h: (r + b, HK // DK + h)),
            pl.BlockSpec((seq_len, DV), lambda b, h: (r + b, 2 * HK // DV + h)),
            pl.BlockSpec((seq_len, DV), lambda b, h: (r + b, (2 * HK + HV) // DV + h))]


def _ret_prompt(decay, proj):
    state = jax.ShapeDtypeStruct((N_PROMPT, 1, HEADS, DK, DV), F32)
    state_spec = pl.BlockSpec((1, 1, HEADS, DK, DV), lambda b: (b, 0, 0, 0, 0))
    return pl.pallas_call(
        _ret_prompt_kernel,
        grid=(N_PROMPT,),
        in_specs=[pl.BlockSpec(memory_space=pltpu.SMEM),
                  pl.BlockSpec((L_PROMPT, N_PROJ), lambda b: (b, 0))],
        out_specs=[pl.BlockSpec((L_PROMPT, HV), lambda b: (b, 0)), state_spec, state_spec],
        out_shape=[jax.ShapeDtypeStruct((T_PROMPT, HV), BF16), state, state],
        compiler_params=_cparams(("arbitrary",)),
        name="ret_prompt",
    )(decay, proj)


def _ret_sample_kernel(dec_ref, q_ref, k_ref, v_ref, g_ref, s0f_ref, s0b_ref, o_ref,
                       of_scr, ob_scr, sf_scr, sb_scr, dm_scr):
    c = RET_CHUNK
    nc = L_SAMPLE // c
    lgf, lgb = _log_decays(dec_ref, pl.program_id(1))
    dm_scr[...] = _decay_mask(lgf, lgb, c)
    pos = lax.broadcasted_iota(jnp.int32, (c, 1), 0).astype(F32)

    def chunk(ci):
        rows = pl.ds(pl.multiple_of(ci * c, c), c)
        return rows, q_ref[rows, :], k_ref[rows, :], v_ref[rows, :]

    def state_update(s_scr, lg, write_pos, kc, vc):
        kw = (kc.astype(F32) * jnp.exp(lg * write_pos)).astype(BF16)
        s_scr[...] = s_scr[...] * jnp.exp(lg * c) + lax.dot_general(kw, vc, TN_DIMS, preferred_element_type=F32)

    def read_state(s_scr, lg, read_pos, qc):
        qr = (qc.astype(F32) * jnp.exp(lg * read_pos)).astype(BF16)
        return jnp.dot(qr, s_scr[...].astype(BF16), preferred_element_type=F32)

    sf_scr[...] = s0f_ref[0, 0, 0]
    sb_scr[...] = s0b_ref[0, 0, 0]

    def scan(step, carry):
        rows, qc, kc, vc = chunk(nc - 1 - step)
        ob_scr[rows, :] = read_state(sb_scr, lgb, c - pos, qc)
        state_update(sb_scr, lgb, pos, kc, vc)
        rows, qc, kc, vc = chunk(step)
        s = lax.dot_general(qc, kc, NT_DIMS, preferred_element_type=F32) * dm_scr[...]
        o = jnp.dot(s.astype(BF16), vc, preferred_element_type=F32)
        of_scr[rows, :] = o + read_state(sf_scr, lgf, pos + 1.0, qc)
        state_update(sf_scr, lgf, c - 1.0 - pos, kc, vc)
        return carry

    lax.fori_loop(0, nc, scan, 0)

    def finish(ci, carry):
        rows = pl.ds(pl.multiple_of(ci * c, c), c)
        o_ref[rows, :] = _norm_gate(of_scr[rows, :] + ob_scr[rows, :], g_ref[rows, :])
        return carry

    lax.fori_loop(0, nc, finish, 0)


def _ret_sample(decay, proj, s0f, s0b):
    state_spec = pl.BlockSpec((1, 1, 1, DK, DV), lambda b, h: (b, 0, h, 0, 0))
    return pl.pallas_call(
        _ret_sample_kernel,
        grid=(N_SAMPLE, HEADS),
        in_specs=[pl.BlockSpec(memory_space=pltpu.SMEM)] + _ret_specs(L_SAMPLE, T_PROMPT)
        + [state_spec, state_spec],
        out_specs=pl.BlockSpec((L_SAMPLE, DV), lambda b, h: (b, h)),
        out_shape=jax.ShapeDtypeStruct((T_SAMPLE, HV), BF16),
        scratch_shapes=[pltpu.VMEM((L_SAMPLE, DV), F32), pltpu.VMEM((L_SAMPLE, DV), F32),
                        pltpu.VMEM((DK, DV), F32), pltpu.VMEM((DK, DV), F32),
                        pltpu.VMEM((RET_CHUNK, RET_CHUNK), F32)],
        compiler_params=_cparams(("arbitrary", "arbitrary")),
        name="ret_sample",
    )(decay, proj, proj, proj, proj, s0f, s0b)


def _store_token_major(ref, val, tm):
    for s in range(ROW_CHUNKS):
        ref[pl.ds(s, tm, stride=ROW_CHUNKS), :] = val[:, s * LANES:(s + 1) * LANES]


def _load_token_major(ref, tm, lead=()):
    return jnp.concatenate([ref[lead + (pl.ds(s, tm, stride=ROW_CHUNKS), slice(None))]
                            for s in range(ROW_CHUNKS)], axis=1)


def _post_mixer(x, y, m, g2_ref, wr_ref, br_ref, x_out_ref, h_out_ref, ti_ref, tg_ref, tm):
    x1 = x + m[2:3] * y
    h = _rms(x1) * (g2_ref[0] * (1.0 + m[4:5])) + m[3:4]
    x_out_ref[...] = x1
    _store_token_major(h_out_ref, h, tm)
    w = wr_ref[...]
    w_hi = w.astype(BF16)
    w_lo = (w - w_hi.astype(F32)).astype(BF16)
    h_hi = h.astype(BF16)
    h_lo = (h - h_hi.astype(F32)).astype(BF16)
    dot = functools.partial(lax.dot_general, dimension_numbers=NT_DIMS, preferred_element_type=F32)
    cur = dot(w_hi, h_hi) + dot(w_hi, h_lo) + dot(w_lo, h_hi) + br_ref[...]
    ie = lax.broadcasted_iota(jnp.int32, (N_EXP, tm), 0).astype(F32)
    vals, idxs = [], []
    for _ in range(TOP_K):
        top = jnp.max(cur, axis=0, keepdims=True)
        idx = jnp.min(jnp.where(cur == top, ie, float(N_EXP)), axis=0, keepdims=True)
        vals.append(top)
        idxs.append(idx)
        cur = jnp.where(ie == idx, -jnp.inf, cur)
    ex = [jnp.exp(v - vals[0]) for v in vals]
    den = ex[0] + ex[1] + ex[2] + ex[3]
    pad = jnp.zeros((SUBLANES - TOP_K, tm), F32)
    ti_ref[...] = jnp.concatenate(idxs + [pad], axis=0).astype(jnp.int32)
    tg_ref[...] = jnp.concatenate([e / den for e in ex] + [pad], axis=0)


def _post_in_specs(layer):
    return [pl.BlockSpec((1, 1, 6, D), lambda i: (layer, _cond_of_tile(i, TM), 0, 0)),
            pl.BlockSpec((1, 1, D), lambda i: (layer, 0, 0)),
            pl.BlockSpec((1, N_EXP, D), lambda i: (layer, 0, 0)),
            pl.BlockSpec((1, N_EXP, 1), lambda i: (layer, 0, 0))]


_POST_OUT_SPECS = [pl.BlockSpec((TM, D), lambda i: (i, 0)),
                   pl.BlockSpec((TM * ROW_CHUNKS, LANES), lambda i: (i, 0)),
                   pl.BlockSpec((SUBLANES, TM), lambda i: (0, i)),
                   pl.BlockSpec((SUBLANES, TM), lambda i: (0, i))]
_POST_OUT_SHAPES = [jax.ShapeDtypeStruct((T, D), F32),
                    jax.ShapeDtypeStruct((T * ROW_CHUNKS, LANES), F32),
                    jax.ShapeDtypeStruct((SUBLANES, T), jnp.int32),
                    jax.ShapeDtypeStruct((SUBLANES, T), F32)]


def _ret_out_kernel(ogp_ref, ogs_ref, w_ref, xp_ref, xs_ref, mod_ref, g2_ref, wr_ref, br_ref,
                    x_out_ref, h_out_ref, ti_ref, tg_ref):
    i = pl.program_id(0)
    og = jnp.where(i < T_PROMPT // TM, ogp_ref[...], ogs_ref[...])
    y = jnp.dot(og, w_ref[...], preferred_element_type=F32)
    _post_mixer(_pick_x(i, TM, xp_ref, xs_ref), y, mod_ref[0, 0], g2_ref, wr_ref[0], br_ref[0],
                x_out_ref, h_out_ref, ti_ref, tg_ref, TM)


def _ret_out(og_p, og_s, w_out, xp, xs, mod4, g_norm2, wr_t, br_c):
    n_p = T_PROMPT // TM
    return pl.pallas_call(
        _ret_out_kernel,
        grid=(T // TM,),
        in_specs=[pl.BlockSpec((TM, HV), lambda i: (jnp.minimum(i, n_p - 1), 0)),
                  pl.BlockSpec((TM, HV), lambda i: (jnp.maximum(i - n_p, 0), 0)),
                  pl.BlockSpec((HV, D), lambda i: (0, 0))]
        + _x_specs(TM) + _post_in_specs(0),
        out_specs=_POST_OUT_SPECS,
        out_shape=_POST_OUT_SHAPES,
        compiler_params=_cparams(("arbitrary",)),
        name="ret_out",
    )(og_p, og_s, w_out, xp, xs, mod4, g_norm2, wr_t, br_c)


def _route_a_kernel(ti_ref, dest_ref, meta_ref):
    tt, tm = MOE_TILE, MOE_BLK
    ti = ti_ref[...]
    ie = lax.broadcasted_iota(jnp.int32, (N_EXP, tt), 0)
    onehots = [(ie == ti[k:k + 1]).astype(F32) for k in range(TOP_K)]
    oh = onehots[0] + onehots[1] + onehots[2] + onehots[3]
    ch = 512
    upper = (lax.broadcasted_iota(jnp.int32, (ch, ch), 0)
             < lax.broadcasted_iota(jnp.int32, (ch, ch), 1)).astype(BF16)
    carry = jnp.zeros((N_EXP, 1), F32)
    cums = []
    for c in range(tt // ch):
        blk = oh[:, c * ch:(c + 1) * ch]
        cums.append(jnp.dot(blk.astype(BF16), upper, preferred_element_type=F32) + carry)
        carry = carry + jnp.sum(blk, axis=1, keepdims=True)
    cum = jnp.concatenate(cums, axis=1)
    cnt = carry
    nb = jnp.floor((cnt + (tm - 1.0)) * (1.0 / tm))
    lower = (lax.broadcasted_iota(jnp.int32, (N_EXP, N_EXP), 1)
             < lax.broadcasted_iota(jnp.int32, (N_EXP, N_EXP), 0)).astype(BF16)
    offb = jnp.dot(lower, jnp.broadcast_to(nb, (N_EXP, LANES)).astype(BF16),
                   preferred_element_type=F32)[:, :1]
    off = offb * tm
    base = off + cum
    dests = [jnp.sum(onehots[k] * base, axis=0, keepdims=True) for k in range(TOP_K)]
    dests.append(jnp.zeros((SUBLANES - TOP_K, tt), F32))
    dest_ref[0] = jnp.concatenate(dests, axis=0).astype(jnp.int32)
    nused = jnp.sum(nb, axis=0, keepdims=True)
    jl = lax.broadcasted_iota(jnp.int32, (N_EXP, META_LANES), 1).astype(F32)
    jc = jnp.minimum(jl, nused - 1.0)
    be = jnp.minimum(jnp.sum(((offb + nb) <= jc).astype(F32), axis=0, keepdims=True), N_EXP - 1.0)
    ief = lax.broadcasted_iota(jnp.int32, (N_EXP, META_LANES), 0).astype(F32)
    end_row = jnp.sum(jnp.where(ief == be, off + cnt, 0.0), axis=0, keepdims=True)
    nvalid = jnp.clip(end_row - jl[:1] * tm, 0.0, float(tm))
    nvalid = jnp.where(jl[:1] < nused, nvalid, 0.0)
    run_end = jnp.sum(jnp.where(ief == be, offb + nb, 0.0), axis=0, keepdims=True)
    nxt = jnp.sum(((offb + nb) <= run_end).astype(F32), axis=0, keepdims=True)
    nxt = jnp.where(run_end < nused, nxt, -1.0)
    meta = jnp.concatenate([be, nvalid, jnp.broadcast_to(nused, (1, META_LANES)), nxt,
                            jnp.zeros((SUBLANES - 4, META_LANES), F32)], axis=0)
    meta_ref[0] = meta.astype(jnp.int32)


def _route_a(ti):
    return pl.pallas_call(
        _route_a_kernel,
        grid=(N_MOE_TILES,),
        in_specs=[pl.BlockSpec((SUBLANES, MOE_TILE), lambda i: (0, i))],
        out_specs=[pl.BlockSpec((1, SUBLANES, MOE_TILE), lambda i: (i, 0, 0)),
                   pl.BlockSpec((1, SUBLANES, META_LANES), lambda i: (i, 0, 0))],
        out_shape=[jax.ShapeDtypeStruct((N_MOE_TILES, SUBLANES, MOE_TILE), jnp.int32),
                   jax.ShapeDtypeStruct((N_MOE_TILES, SUBLANES, META_LANES), jnp.int32)],
        compiler_params=_cparams(("arbitrary",)),
        name="route_a",
    )(ti)


ROUTE_STEPS = N_EXP // N_MOE_TILES


def _cast_route_kernel(wu_ref, wd_ref, dest_ref, meta_ref, ou_ref, od_ref, slot_ref):
    ou_ref[...] = wu_ref[0].astype(BF16)
    od_ref[...] = wd_ref[0].astype(BF16)
    part = pl.program_id(0) % ROUTE_STEPS
    group = SUBLANES

    def per_block(j, carry):
        def pad(s, c2):
            slot_ref[j * MOE_BLK + s] = MOE_TILE * ROW_CHUNKS
            return c2

        def pad_group(gi, c2):
            slots = slot_ref.at[pl.ds(j * MOE_BLK + gi * group, group)]
            for u in range(group):
                slots[u] = MOE_TILE * ROW_CHUNKS
            return c2

        n_real = meta_ref[0, 1, j]
        first_group = (n_real + group - 1) // group
        lax.fori_loop(n_real, first_group * group, pad, 0)
        lax.fori_loop(first_group, MOE_BLK // group, pad_group, 0)
        return carry

    blocks = MOE_NBLK // ROUTE_STEPS
    lax.fori_loop(part * blocks, (part + 1) * blocks, per_block, 0)

    tokens = MOE_TILE // ROUTE_STEPS

    def per_group(tg, carry):
        t0 = part * tokens + tg * group
        rows = [dest_ref.at[0, k, pl.ds(t0, group)] for k in range(TOP_K)]
        tok_code = t0 * ((1 << CODE_SHIFT) + ROW_CHUNKS)
        for u0 in range(0, group, 2):
            loaded = [(k, u, rows[k][u]) for u in range(u0, u0 + 2) for k in range(TOP_K)]
            for k, u, d in loaded:
                slot_ref[d] = tok_code + ((k * MOE_TILE + u) << CODE_SHIFT) + u * ROW_CHUNKS
        return carry

    lax.fori_loop(0, tokens // group, per_group, 0)


def _cast_route(w_up, w_down, layer, dest, meta):
    tile = lambda e: e // ROUTE_STEPS
    weight_in = lambda w: pl.BlockSpec((1, 1) + w.shape[2:], lambda e: (layer, e, 0, 0))
    weight_out = lambda w: pl.BlockSpec((1,) + w.shape[2:], lambda e: (e, 0, 0))
    return pl.pallas_call(
        _cast_route_kernel,
        grid=(N_EXP,),
        in_specs=[weight_in(w_up), weight_in(w_down),
                  pl.BlockSpec((1, SUBLANES, MOE_TILE), lambda e: (tile(e), 0, 0), memory_space=pltpu.SMEM),
                  pl.BlockSpec((1, SUBLANES, META_LANES), lambda e: (tile(e), 0, 0),
                               memory_space=pltpu.SMEM)],
        out_specs=[weight_out(w_up), weight_out(w_down),
                   pl.BlockSpec((MOE_SLOTS,), lambda e: (tile(e),), memory_space=pltpu.SMEM)],
        out_shape=[jax.ShapeDtypeStruct(w_up.shape[1:], BF16), jax.ShapeDtypeStruct(w_down.shape[1:], BF16),
                   jax.ShapeDtypeStruct((N_MOE_TILES * MOE_SLOTS,), jnp.int32)],
        compiler_params=_cparams(("arbitrary",)),
        name="cast_route",
    )(w_up, w_down, dest, meta)


def _moe_kernel(be_ref, nu_ref, nx_ref, slot_ref, gate_ref, bu_ref, bd_ref, h_hbm, wu_hbm, wd_hbm,
                out_hbm, g0_scr, g1_scr, y0_scr, y1_scr, wu_scr, wd_scr, h_scr, out_scr, sem, tile_sem):
    i = pl.program_id(0)
    tm = MOE_BLK
    nused = nu_ref[i]
    base = i * MOE_NBLK

    def weight_copies(e, buf):
        return (pltpu.make_async_copy(wu_hbm.at[e], wu_scr.at[buf], sem.at[0, buf]),
                pltpu.make_async_copy(wd_hbm.at[e], wd_scr.at[buf], sem.at[1, buf]))
    tile_rows = MOE_TILE * ROW_CHUNKS
    h_copy = pltpu.make_async_copy(h_hbm.at[pl.ds(i * tile_rows, tile_rows)], h_scr, tile_sem.at[0])
    out_copy = pltpu.make_async_copy(out_scr, out_hbm.at[i], tile_sem.at[1])

    def row_slice(ii):
        return slice(ii * ROW_CHUNKS, (ii + 1) * ROW_CHUNKS)

    def gather(blk, g_scr, lo, hi):
        codes = slot_ref.at[pl.ds(blk * tm, tm)]
        for ii in range(lo, hi):
            off = pl.multiple_of(codes[ii] & (tile_rows - 1), ROW_CHUNKS)
            g_scr[row_slice(ii), :] = h_scr[pl.ds(off, ROW_CHUNKS), :]

    def scatter(blk, y_scr, lo, hi):
        batch = 8
        codes = slot_ref.at[pl.ds(blk * tm, tm)]
        for i0 in range(lo, hi, batch):
            pending = []
            for ii in range(i0, i0 + batch):
                code = codes[ii]
                gate = gate_ref[code >> CODE_SHIFT]
                off = code & ((1 << CODE_SHIFT) - 1)
                rows = pl.ds(pl.multiple_of(off, ROW_CHUNKS), ROW_CHUNKS)
                pending.append((rows, out_scr[rows, :] + gate * y_scr[row_slice(ii), :]))
            for rows, val in pending:
                out_scr[rows, :] = val

    h_copy.start()
    for copy in weight_copies(be_ref[base], 0):
        copy.start()
    out_scr[...] = jnp.zeros(out_scr.shape, F32)
    y1_scr[...] = jnp.zeros(y1_scr.shape, F32)
    h_copy.wait()
    gather(0, g0_scr, 0, tm)

    def step(j, run, g_cur, g_nxt, y_cur, y_prv):
        e = be_ref[base + j]
        jp = jnp.maximum(j - 1, 0)
        jn = jnp.minimum(j + 1, nused - 1)
        first = jnp.logical_or(j == 0, be_ref[base + jp] != e)
        run = run + jnp.where(jnp.logical_and(first, j > 0), 1, 0)
        buf = run & 1

        @pl.when(first)
        def _():
            for copy in weight_copies(e, buf):
                copy.wait()
            nxt = nx_ref[base + j]

            @pl.when(nxt >= 0)
            def _():
                for copy in weight_copies(nxt, 1 - buf):
                    copy.start()

        def compute(wbuf):
            a = _load_token_major(g_cur, tm).astype(BF16)
            up = jnp.dot(a, wu_scr[wbuf], preferred_element_type=F32) + bu_ref[e]
            scatter(jp, y_prv, 0, tm)
            gather(jn, g_nxt, 0, tm)
            glu = jnp.minimum(up[:, :D_EXP], SWIGLU_LIMIT)
            lin = jnp.clip(up[:, D_EXP:], -SWIGLU_LIMIT, SWIGLU_LIMIT)
            act = glu * _sigmoid(SWIGLU_ALPHA * glu) * (lin + 1.0)
            y = jnp.dot(act.astype(BF16), wd_scr[wbuf], preferred_element_type=F32) + bd_ref[e]
            _store_token_major(y_cur, y, tm)

        for wbuf in range(2):
            pl.when(buf == wbuf)(functools.partial(compute, wbuf))

        @pl.when(j == nused - 1)
        def _():
            scatter(j, y_cur, 0, tm)

        return run

    def pair(jj, run):
        run = step(2 * jj, run, g0_scr, g1_scr, y0_scr, y1_scr)
        return lax.cond(2 * jj + 1 < nused,
                        lambda r: step(2 * jj + 1, r, g1_scr, g0_scr, y1_scr, y0_scr),
                        lambda r: r, run)

    lax.fori_loop(0, (nused + 1) // 2, pair, jnp.int32(0))
    out_copy.start()
    out_copy.wait()


def _moe(be, nu, nx, slot, gates, h_tm, w_up, b_up, w_down, b_down):
    row_buf = pltpu.VMEM((MOE_BLK * ROW_CHUNKS, LANES), F32)
    grid_spec = pltpu.PrefetchScalarGridSpec(
        num_scalar_prefetch=3,
        grid=(N_MOE_TILES,),
        in_specs=[
            pl.BlockSpec((MOE_SLOTS,), lambda i, *_: (i,), memory_space=pltpu.SMEM),
            pl.BlockSpec((TOP_K * MOE_TILE,), lambda i, *_: (i,), memory_space=pltpu.SMEM),
            pl.BlockSpec((N_EXP, 1, 2 * D_EXP), lambda i, *_: (0, 0, 0)),
            pl.BlockSpec((N_EXP, 1, D), lambda i, *_: (0, 0, 0)),
            pl.BlockSpec(memory_space=pl.ANY),
            pl.BlockSpec(memory_space=pl.ANY),
            pl.BlockSpec(memory_space=pl.ANY),
        ],
        out_specs=pl.BlockSpec(memory_space=pl.ANY),
        scratch_shapes=[row_buf, row_buf, row_buf, row_buf,
                        pltpu.VMEM((2, D, 2 * D_EXP), BF16), pltpu.VMEM((2, D_EXP, D), BF16),
                        pltpu.VMEM((MOE_TILE * ROW_CHUNKS, LANES), F32),
                        pltpu.VMEM(((MOE_TILE + 1) * ROW_CHUNKS, LANES), F32),
                        pltpu.SemaphoreType.DMA((2, 2)), pltpu.SemaphoreType.DMA((2,))],
    )
    return pl.pallas_call(
        _moe_kernel,
        grid_spec=grid_spec,
        out_shape=jax.ShapeDtypeStruct((N_MOE_TILES, (MOE_TILE + 1) * ROW_CHUNKS, LANES), F32),
        compiler_params=_cparams(("arbitrary",)),
        name="moe",
    )(be, nu, nx, slot, gates, b_up.reshape(N_EXP, 1, 2 * D_EXP), b_down.reshape(N_EXP, 1, D),
      h_tm, w_up, w_down)


def _moe_layer(layer, ti, tg, h_tm, w_up, b_up, w_down, b_down):
    dest, meta = _route_a(ti)
    w_up_bf16, w_down_bf16, slot = _cast_route(w_up, w_down, layer, dest, meta)
    be, nx = (meta[:, r, :MOE_NBLK].reshape(-1) for r in (0, 3))
    nu = meta[:, 2, 0]
    gates = tg[:TOP_K].reshape(TOP_K, N_MOE_TILES, MOE_TILE).transpose(1, 0, 2).reshape(-1)
    return _moe(be, nu, nx, slot, gates, h_tm, w_up_bf16, b_up[layer], w_down_bf16, b_down[layer])


_MOE_OUT_SPEC = pl.BlockSpec((1, TM * ROW_CHUNKS, LANES),
                             lambda i: (i // (MOE_TILE // TM), i % (MOE_TILE // TM), 0))


def _conv_kernel(x_ref, moe_ref, modp_ref, g1_ref, w1_ref, b1_ref, wdw_ref, bdw_ref, lng_ref, lnb_ref,
                 w2_ref, b2_ref, mod_ref, g2_ref, wr_ref, br_ref,
                 x_out_ref, h_out_ref, ti_ref, tg_ref, pad_scr, conv_scr, shift_scr):
    i = pl.program_id(0)
    mp = modp_ref[0, 0]
    m = mod_ref[0, 0]
    x = x_ref[...] + mp[5:6] * _load_token_major(moe_ref, TM, (0,))
    h = (_rms(x) * (g1_ref[0] * (1.0 + m[1:2])) + m[0:1]).astype(BF16)
    ag = jnp.dot(h, w1_ref[0], preferred_element_type=F32) + b1_ref[0]
    u = ag[:, :D] * _sigmoid(ag[:, D:])
    is_prompt = i < T_PROMPT // TM

    def fill(seg):
        pitch = seg + CONV_PAD
        for s in range(TM // seg):
            pad_scr[s * pitch:s * pitch + CONV_PAD, :] = jnp.zeros((CONV_PAD, D), F32)
            pad_scr[s * pitch + CONV_PAD:(s + 1) * pitch, :] = u[s * seg:(s + 1) * seg, :]
        end = (TM // seg) * pitch
        pad_scr[end:end + CONV_PAD, :] = jnp.zeros((CONV_PAD, D), F32)

    @pl.when(is_prompt)
    def _():
        fill(L_PROMPT)

    @pl.when(jnp.logical_not(is_prompt))
    def _():
        fill(GRID_W)

    per_seq = L_PROMPT // CONV_CHUNK
    halo = CONV_CHUNK + 2 * CONV_PAD

    def conv_chunk(c, carry):
        base = jnp.where(is_prompt, (c // per_seq) * (L_PROMPT + CONV_PAD) + (c % per_seq) * CONV_CHUNK,
                         c * (GRID_W + CONV_PAD))
        base = pl.multiple_of(base, SUBLANES)
        out_rows = pl.ds(pl.multiple_of(c * CONV_CHUNK, CONV_CHUNK), CONV_CHUNK)
        for gl in range(D // LANES):
            lanes = slice(gl * LANES, (gl + 1) * LANES)
            blk = pad_scr[pl.ds(base, halo), lanes]
            span = halo - SUBLANES
            for r in range(SUBLANES):
                shift_scr[gl * SUBLANES + r] = blk[r:r + span, :]
            acc = jnp.zeros((CONV_CHUNK, LANES), F32)
            for tap in range(CONV_W):
                lo = CONV_PAD - CONV_W // 2 + tap
                al = lo // SUBLANES * SUBLANES
                acc = acc + (shift_scr[gl * SUBLANES + lo % SUBLANES, al:al + CONV_CHUNK, :]
                             * wdw_ref[0, tap:tap + 1, lanes])
            conv_scr[out_rows, lanes] = acc + bdw_ref[0, :, lanes]
        return carry

    lax.fori_loop(0, TM // CONV_CHUNK, conv_chunk, 0)
    uc = conv_scr[...]
    mu = jnp.mean(uc, axis=-1, keepdims=True)
    var = jnp.mean(jnp.square(uc - mu), axis=-1, keepdims=True)
    z = (uc - mu) * lax.rsqrt(var + EPS) * lng_ref[0] + lnb_ref[0]
    z = (z * _sigmoid(z)).astype(BF16)
    y = jnp.dot(z, w2_ref[0], preferred_element_type=F32) + b2_ref[0]
    _post_mixer(x, y, m, g2_ref, wr_ref[0], br_ref[0], x_out_ref, h_out_ref, ti_ref, tg_ref, TM)


def _conv_layer(x1, moe0, mod4, g_norm1, w_pw1, b_pw1, w_dw, b_dw, ln_g, ln_b, w_pw2, b_pw2,
                g_norm2, wr_t, br_c):
    def full(shape):
        return pl.BlockSpec(shape, lambda i: (0,) * len(shape))

    pad_rows = (TM // GRID_W) * (GRID_W + CONV_PAD) + CONV_PAD
    return pl.pallas_call(
        _conv_kernel,
        grid=(T // TM,),
        in_specs=[pl.BlockSpec((TM, D), lambda i: (i, 0)),
                  _MOE_OUT_SPEC,
                  pl.BlockSpec((1, 1, 6, D), lambda i: (0, _cond_of_tile(i, TM), 0, 0)),
                  pl.BlockSpec((1, 1, D), lambda i: (1, 0, 0)),
                  full((1, D, 2 * D)), full((1, 1, 2 * D)), full((1, CONV_W, D)), full((1, 1, D)),
                  full((1, 1, D)), full((1, 1, D)), full((1, D, D)), full((1, 1, D))]
        + _post_in_specs(1),
        out_specs=_POST_OUT_SPECS,
        out_shape=_POST_OUT_SHAPES,
        scratch_shapes=[pltpu.VMEM((pad_rows, D), F32), pltpu.VMEM((TM, D), F32),
                        pltpu.VMEM((D // LANES * SUBLANES, CONV_CHUNK + 2 * CONV_PAD - SUBLANES, LANES), F32)],
        compiler_params=_cparams(("arbitrary",)),
        name="conv",
    )(x1, moe0, mod4, g_norm1, w_pw1, b_pw1.reshape(1, 1, 2 * D), w_dw, b_dw.reshape(1, 1, D),
      ln_g.reshape(1, 1, D), ln_b.reshape(1, 1, D), w_pw2, b_pw2.reshape(1, 1, D),
      mod4, g_norm2, wr_t, br_c)


def _final_kernel(x_ref, moe_ref, mod_ref, g_ref, yp_ref, ys_ref):
    i = pl.program_id(0)
    x = x_ref[...] + mod_ref[0, 0][5:6] * _load_token_major(moe_ref, TM, (0,))
    y = _rms(x) * g_ref[...]

    @pl.when(i < T_PROMPT // TM)
    def _():
        yp_ref[...] = y

    @pl.when(i >= T_PROMPT // TM)
    def _():
        ys_ref[...] = y


def _final(x, moe1, mod4, g_final):
    n_p = T_PROMPT // TM
    return pl.pallas_call(
        _final_kernel,
        grid=(T // TM,),
        in_specs=[pl.BlockSpec((TM, D), lambda i: (i, 0)),
                  _MOE_OUT_SPEC,
                  pl.BlockSpec((1, 1, 6, D), lambda i: (1, _cond_of_tile(i, TM), 0, 0)),
                  pl.BlockSpec((1, D), lambda i: (0, 0))],
        out_specs=[pl.BlockSpec((TM, D), lambda i: (jnp.minimum(i, n_p - 1), 0)),
                   pl.BlockSpec((TM, D), lambda i: (jnp.maximum(i - n_p, 0), 0))],
        out_shape=[jax.ShapeDtypeStruct((T_PROMPT, D), F32), jax.ShapeDtypeStruct((T_SAMPLE, D), F32)],
        compiler_params=_cparams(("arbitrary",)),
        name="final",
    )(x, moe1, mod4, g_final)


def kernel(x_prompt, x_sample, c, state_ret_fwd, state_ret_bwd, c_ctx, w_ada, b_ada, g_norm1, g_norm2,
           ret_w_in, ret_decay, ret_w_out, conv_w_pw1, conv_b_pw1, conv_w_dw, conv_b_dw, conv_ln_g,
           conv_ln_b, conv_w_pw2, conv_b_pw2, moe_w_router, moe_b_router, moe_w_up, moe_b_up,
           moe_w_down, moe_b_down, g_final):
    xp = x_prompt.reshape(T_PROMPT, D)
    xs = x_sample.reshape(T_SAMPLE, D)
    cond = jnp.concatenate([c_ctx[None, :], c, jnp.zeros((SUBLANES - 1 - N_SAMPLE, D), F32)], axis=0)
    mod = _ada(cond.T, w_ada, b_ada)
    mod4 = mod.reshape(mod.shape[0], SUBLANES, 6, D)
    wr_t = jnp.swapaxes(moe_w_router, 1, 2)
    br_c = moe_b_router[:, :, None]
    gn1 = g_norm1[:, None, :]
    gn2 = g_norm2[:, None, :]

    proj = _proj(xp, xs, g_norm1[0:1], mod4, ret_w_in[0].astype(BF16))
    og_p, new_f, new_b = _ret_prompt(ret_decay[0], proj)
    og_s = _ret_sample(ret_decay[0], proj, state_ret_fwd, state_ret_bwd)
    x1, h_tm, ti, tg = _ret_out(og_p, og_s, ret_w_out[0].astype(BF16), xp, xs, mod4, gn2, wr_t, br_c)
    moe0 = _moe_layer(0, ti, tg, h_tm, moe_w_up, moe_b_up, moe_w_down, moe_b_down)

    x2, h_tm, ti, tg = _conv_layer(x1, moe0, mod4, gn1, conv_w_pw1.astype(BF16), conv_b_pw1, conv_w_dw,
                                   conv_b_dw, conv_ln_g, conv_ln_b, conv_w_pw2.astype(BF16), conv_b_pw2,
                                   gn2, wr_t, br_c)
    moe1 = _moe_layer(1, ti, tg, h_tm, moe_w_up, moe_b_up, moe_w_down, moe_b_down)

    y_p, y_s = _final(x2, moe1, mod4, g_final[None, :])
    return (y_p.reshape(N_PROMPT, L_PROMPT, D), y_s.reshape(N_SAMPLE, L_SAMPLE, D), new_f, new_b)
```

```python
import functools

import numpy as np
import jax
import jax.numpy as jnp
from jax import lax
from jax.experimental import pallas as pl
from jax.experimental.pallas import tpu as pltpu

F32 = jnp.float32
BF16 = jnp.bfloat16

D = 1024
N_PROMPT, L_PROMPT = 32, 256
N_SAMPLE, L_SAMPLE = 2, 4096
T_PROMPT = N_PROMPT * L_PROMPT
T_SAMPLE = N_SAMPLE * L_SAMPLE
T = T_PROMPT + T_SAMPLE
GRID_W = 64
HEADS, DK, DV = 4, 256, 512
HK, HV = HEADS * DK, HEADS * DV
N_PROJ = 2 * HK + 2 * HV
ROPE_BASE = 10000.0
CONV_W = 31
CONV_PAD = 16
CONV_CHUNK = 64
N_EXP, TOP_K, D_EXP = 32, 4, 1024
SWIGLU_LIMIT, SWIGLU_ALPHA = 7.0, 1.702
EPS = 1e-6

LANES = 128
SUBLANES = 8
ROW_CHUNKS = D // LANES
VMEM_LIMIT = 56 * 1024 * 1024

TM = 512
MOE_TILE = 4096
MOE_BLK = 256
MOE_NBLK = MOE_TILE * TOP_K // MOE_BLK + N_EXP
MOE_SLOTS = MOE_NBLK * MOE_BLK
META_LANES = 2 * LANES
CODE_SHIFT = 16
N_MOE_TILES = T // MOE_TILE
RET_CHUNK = 256

NT_DIMS = (((1,), (1,)), ((), ()))
TN_DIMS = (((0,), (0,)), ((), ()))


def _cparams(sem):
    return pltpu.CompilerParams(dimension_semantics=sem, vmem_limit_bytes=VMEM_LIMIT)


def _cond_of_tile(i, tm):
    return jnp.maximum((i * tm) // L_SAMPLE - 1, 0)


def _sigmoid(x):
    return 0.5 * jnp.tanh(0.5 * x) + 0.5


def _rms(x):
    return x * lax.rsqrt(jnp.mean(x * x, axis=-1, keepdims=True) + EPS)


def _pick_x(i, tm, xp_ref, xs_ref):
    return jnp.where(i < T_PROMPT // tm, xp_ref[...], xs_ref[...])


def _x_specs(tm):
    n_p = T_PROMPT // tm
    return [pl.BlockSpec((tm, D), lambda i, *_: (jnp.minimum(i, n_p - 1), 0)),
            pl.BlockSpec((tm, D), lambda i, *_: (jnp.maximum(i - n_p, 0), 0))]


def _ada_kernel(ct_ref, w_ref, b_ref, o_ref):
    ct = ct_ref[...]
    s = ct * _sigmoid(ct)
    w = w_ref[0]
    rows = [jnp.sum(w * s[:, r:r + 1], axis=0, keepdims=True) for r in range(3)]
    rows.append(jnp.zeros((SUBLANES - 3, w.shape[1]), F32))
    o_ref[0] = jnp.concatenate(rows, axis=0) + b_ref[0]


def _ada(ct, w_ada, b_ada):
    depth, _, n = w_ada.shape
    tn = 1536
    return pl.pallas_call(
        _ada_kernel,
        grid=(depth, n // tn),
        in_specs=[pl.BlockSpec((D, SUBLANES), lambda l, j: (0, 0)),
                  pl.BlockSpec((1, D, tn), lambda l, j: (l, 0, j)),
                  pl.BlockSpec((1, 1, tn), lambda l, j: (l, 0, j))],
        out_specs=pl.BlockSpec((1, SUBLANES, tn), lambda l, j: (l, 0, j)),
        out_shape=jax.ShapeDtypeStruct((depth, SUBLANES, n), F32),
        compiler_params=_cparams(("arbitrary", "arbitrary")),
        name="ada",
    )(ct, w_ada, b_ada.reshape(depth, 1, n))


def _rope_tables():
    t = np.arange(L_SAMPLE)
    row = (t // GRID_W).astype(np.float32)
    col = (t % GRID_W).astype(np.float32)
    nf = DK // 4
    inv = (np.float32(ROPE_BASE) ** (-np.arange(nf, dtype=np.float32) / np.float32(nf))).astype(np.float32)
    cos, sin = [], []
    for pos in (row, col):
        ang = (pos[:, None] * inv[None, :]).astype(np.float32)
        c, s = np.cos(ang).astype(np.float32), np.sin(ang).astype(np.float32)
        cos += [c, c]
        sin += [-s, s]
    return np.concatenate(cos, axis=1), np.concatenate(sin, axis=1)


def _proj_kernel(xp_ref, xs_ref, g_ref, mod_ref, w_ref, cos_ref, sin_ref, o_ref):
    i = pl.program_id(0)
    m = mod_ref[0, 0]
    h = _rms(_pick_x(i, TM, xp_ref, xs_ref)) * (g_ref[...] * (1.0 + m[1:2]))
    h = (h + m[0:1]).astype(BF16)
    is_sample = i >= T_PROMPT // TM
    cos = jnp.where(is_sample, cos_ref[...], 1.0)
    sin = jnp.where(is_sample, sin_ref[...], 0.0)
    for col in range(N_PROJ // D):
        acc = jnp.dot(h, w_ref[:, col * D:(col + 1) * D], preferred_element_type=F32)
        if col == 1:
            acc = acc * DK ** -0.5
        if col < 2:
            for c in range(D // LANES):
                a = acc[:, c * LANES:(c + 1) * LANES]
                p = (c % 2) * LANES
                r = a * cos[:, p:p + LANES] + pltpu.roll(a, LANES // 2, 1) * sin[:, p:p + LANES]
                o_ref[:, col * D + c * LANES:col * D + (c + 1) * LANES] = r.astype(BF16)
        else:
            o_ref[:, col * D:(col + 1) * D] = acc.astype(BF16)


def _proj(xp, xs, g1, mod4, w_in):
    cos, sin = _rope_tables()
    n_p = T_PROMPT // TM
    n_s = L_SAMPLE // TM
    tab_spec = pl.BlockSpec((TM, DK), lambda i: (jnp.maximum(i - n_p, 0) % n_s, 0))
    return pl.pallas_call(
        _proj_kernel,
        grid=(T // TM,),
        in_specs=_x_specs(TM) + [
            pl.BlockSpec((1, D), lambda i: (0, 0)),
            pl.BlockSpec((1, 1, 6, D), lambda i: (0, _cond_of_tile(i, TM), 0, 0)),
            pl.BlockSpec((D, N_PROJ), lambda i: (0, 0)),
            tab_spec, tab_spec],
        out_specs=pl.BlockSpec((TM, N_PROJ), lambda i: (i, 0)),
        out_shape=jax.ShapeDtypeStruct((T, N_PROJ), BF16),
        compiler_params=_cparams(("arbitrary",)),
        name="proj",
    )(xp, xs, g1, mod4, w_in, jnp.asarray(cos), jnp.asarray(sin))


def _log_decays(dec_ref, head):
    out = []
    for direction in range(2):
        d = jnp.full((1, 1), dec_ref[direction, head], F32)
        out.append(jnp.minimum(d, 0.0) - jnp.log(1.0 + jnp.exp(-jnp.abs(d))))
    return out


def _decay_mask(lgf, lgb, c):
    ii = lax.broadcasted_iota(jnp.int32, (c, c), 0)
    jj = lax.broadcasted_iota(jnp.int32, (c, c), 1)
    diff = (ii - jj).astype(F32)
    fwd = jnp.where(diff >= 0, jnp.exp(lgf * jnp.maximum(diff, 0.0)), 0.0)
    bwd = jnp.where(diff <= 0, jnp.exp(lgb * jnp.maximum(-diff, 0.0)), 0.0)
    return fwd + bwd


def _norm_gate(o, g):
    g = g.astype(F32)
    return (_rms(o) * (g * _sigmoid(g))).astype(BF16)


def _ret_prompt_kernel(dec_ref, p_ref, o_ref, sf_ref, sb_ref):
    c = L_PROMPT
    pos = lax.broadcasted_iota(jnp.int32, (c, 1), 0).astype(F32)
    for head in range(HEADS):
        lgf, lgb = _log_decays(dec_ref, head)
        q = p_ref[:, head * DK:(head + 1) * DK]
        k = p_ref[:, HK + head * DK:HK + (head + 1) * DK]
        v = p_ref[:, 2 * HK + head * DV:2 * HK + (head + 1) * DV]
        g = p_ref[:, 2 * HK + HV + head * DV:2 * HK + HV + (head + 1) * DV]
        s = lax.dot_general(q, k, NT_DIMS, preferred_element_type=F32) * _decay_mask(lgf, lgb, c)
        o = jnp.dot(s.astype(BF16), v, preferred_element_type=F32)
        o_ref[:, head * DV:(head + 1) * DV] = _norm_gate(o, g)
        kf = k.astype(F32)
        k_fwd = (kf * jnp.exp(lgf * (c - 1.0 - pos))).astype(BF16)
        k_bwd = (kf * jnp.exp(lgb * pos)).astype(BF16)
        sf_ref[0, 0, head] = lax.dot_general(k_fwd, v, TN_DIMS, preferred_element_type=F32)
        sb_ref[0, 0, head] = lax.dot_general(k_bwd, v, TN_DIMS, preferred_element_type=F32)


def _ret_specs(seq_len, row0):
    r = row0 // seq_len
    return [pl.BlockSpec((seq_len, DK), lambda b, h: (r + b, h)),
            pl.BlockSpec((seq_len, DK), lambda b, h: (r + b, HK // DK + h)),
            pl.BlockSpec((seq_len, DV), lambda b, h: (r + b, 2 * HK // DV + h)),
            pl.BlockSpec((seq_len, DV), lambda b, h: (r + b, (2 * HK + HV) // DV + h))]


def _ret_prompt(decay, proj):
    state = jax.ShapeDtypeStruct((N_PROMPT, 1, HEADS, DK, DV), F32)
    state_spec = pl.BlockSpec((1, 1, HEADS, DK, DV), lambda b: (b, 0, 0, 0, 0))
    return pl.pallas_call(
        _ret_prompt_kernel,
        grid=(N_PROMPT,),
        in_specs=[pl.BlockSpec(memory_space=pltpu.SMEM),
                  pl.BlockSpec((L_PROMPT, N_PROJ), lambda b: (b, 0))],
        out_specs=[pl.BlockSpec((L_PROMPT, HV), lambda b: (b, 0)), state_spec, state_spec],
        out_shape=[jax.ShapeDtypeStruct((T_PROMPT, HV), BF16), state, state],
        compiler_params=_cparams(("arbitrary",)),
        name="ret_prompt",
    )(decay, proj)


def _ret_sample_kernel(dec_ref, q_ref, k_ref, v_ref, g_ref, s0f_ref, s0b_ref, o_ref,
                       of_scr, ob_scr, sf_scr, sb_scr, dm_scr):
    c = RET_CHUNK
    nc = L_SAMPLE // c
    lgf, lgb = _log_decays(dec_ref, pl.program_id(1))
    dm_scr[...] = _decay_mask(lgf, lgb, c)
    pos = lax.broadcasted_iota(jnp.int32, (c, 1), 0).astype(F32)

    def chunk(ci):
        rows = pl.ds(pl.multiple_of(ci * c, c), c)
        return rows, q_ref[rows, :], k_ref[rows, :], v_ref[rows, :]

    def state_update(s_scr, lg, write_pos, kc, vc):
        kw = (kc.astype(F32) * jnp.exp(lg * write_pos)).astype(BF16)
        s_scr[...] = s_scr[...] * jnp.exp(lg * c) + lax.dot_general(kw, vc, TN_DIMS, preferred_element_type=F32)

    def read_state(s_scr, lg, read_pos, qc):
        qr = (qc.astype(F32) * jnp.exp(lg * read_pos)).astype(BF16)
        return jnp.dot(qr, s_scr[...].astype(BF16), preferred_element_type=F32)

    sf_scr[...] = s0f_ref[0, 0, 0]
    sb_scr[...] = s0b_ref[0, 0, 0]

    def scan(step, carry):
        rows, qc, kc, vc = chunk(nc - 1 - step)
        ob_scr[rows, :] = read_state(sb_scr, lgb, c - pos, qc)
        state_update(sb_scr, lgb, pos, kc, vc)
        rows, qc, kc, vc = chunk(step)
        s = lax.dot_general(qc, kc, NT_DIMS, preferred_element_type=F32) * dm_scr[...]
        o = jnp.dot(s.astype(BF16), vc, preferred_element_type=F32)
        of_scr[rows, :] = o + read_state(sf_scr, lgf, pos + 1.0, qc)
        state_update(sf_scr, lgf, c - 1.0 - pos, kc, vc)
        return carry

    lax.fori_loop(0, nc, scan, 0)

    def finish(ci, carry):
        rows = pl.ds(pl.multiple_of(ci * c, c), c)
        o_ref[rows, :] = _norm_gate(of_scr[rows, :] + ob_scr[rows, :], g_ref[rows, :])
        return carry

    lax.fori_loop(0, nc, finish, 0)


def _ret_sample(decay, proj, s0f, s0b):
    state_spec = pl.BlockSpec((1, 1, 1, DK, DV), lambda b, h: (b, 0, h, 0, 0))
    return pl.pallas_call(
        _ret_sample_kernel,
        grid=(N_SAMPLE, HEADS),
        in_specs=[pl.BlockSpec(memory_space=pltpu.SMEM)] + _ret_specs(L_SAMPLE, T_PROMPT)
        + [state_spec, state_spec],
        out_specs=pl.BlockSpec((L_SAMPLE, DV), lambda b, h: (b, h)),
        out_shape=jax.ShapeDtypeStruct((T_SAMPLE, HV), BF16),
        scratch_shapes=[pltpu.VMEM((L_SAMPLE, DV), F32), pltpu.VMEM((L_SAMPLE, DV), F32),
                        pltpu.VMEM((DK, DV), F32), pltpu.VMEM((DK, DV), F32),
                        pltpu.VMEM((RET_CHUNK, RET_CHUNK), F32)],
        compiler_params=_cparams(("arbitrary", "arbitrary")),
        name="ret_sample",
    )(decay, proj, proj, proj, proj, s0f, s0b)


def _store_token_major(ref, val, tm):
    for s in range(ROW_CHUNKS):
        ref[pl.ds(s, tm, stride=ROW_CHUNKS), :] = val[:, s * LANES:(s + 1) * LANES]


def _load_token_major(ref, tm, lead=()):
    return jnp.concatenate([ref[lead + (pl.ds(s, tm, stride=ROW_CHUNKS), slice(None))]
                            for s in range(ROW_CHUNKS)], axis=1)


def _post_mixer(x, y, m, g2_ref, wr_ref, br_ref, x_out_ref, h_out_ref, ti_ref, tg_ref, tm):
    x1 = x + m[2:3] * y
    h = _rms(x1) * (g2_ref[0] * (1.0 + m[4:5])) + m[3:4]
    x_out_ref[...] = x1
    _store_token_major(h_out_ref, h, tm)
    w = wr_ref[...]
    w_hi = w.astype(BF16)
    w_lo = (w - w_hi.astype(F32)).astype(BF16)
    h_hi = h.astype(BF16)
    h_lo = (h - h_hi.astype(F32)).astype(BF16)
    dot = functools.partial(lax.dot_general, dimension_numbers=NT_DIMS, preferred_element_type=F32)
    cur = dot(w_hi, h_hi) + dot(w_hi, h_lo) + dot(w_lo, h_hi) + br_ref[...]
    ie = lax.broadcasted_iota(jnp.int32, (N_EXP, tm), 0).astype(F32)
    vals, idxs = [], []
    for _ in range(TOP_K):
        top = jnp.max(cur, axis=0, keepdims=True)
        idx = jnp.min(jnp.where(cur == top, ie, float(N_EXP)), axis=0, keepdims=True)
        vals.append(top)
        idxs.append(idx)
        cur = jnp.where(ie == idx, -jnp.inf, cur)
    ex = [jnp.exp(v - vals[0]) for v in vals]
    den = ex[0] + ex[1] + ex[2] + ex[3]
    pad = jnp.zeros((SUBLANES - TOP_K, tm), F32)
    ti_ref[...] = jnp.concatenate(idxs + [pad], axis=0).astype(jnp.int32)
    tg_ref[...] = jnp.concatenate([e / den for e in ex] + [pad], axis=0)


def _post_in_specs(layer):
    return [pl.BlockSpec((1, 1, 6, D), lambda i: (layer, _cond_of_tile(i, TM), 0, 0)),
            pl.BlockSpec((1, 1, D), lambda i: (layer, 0, 0)),
            pl.BlockSpec((1, N_EXP, D), lambda i: (layer, 0, 0)),
            pl.BlockSpec((1, N_EXP, 1), lambda i: (layer, 0, 0))]


_POST_OUT_SPECS = [pl.BlockSpec((TM, D), lambda i: (i, 0)),
                   pl.BlockSpec((TM * ROW_CHUNKS, LANES), lambda i: (i, 0)),
                   pl.BlockSpec((SUBLANES, TM), lambda i: (0, i)),
                   pl.BlockSpec((SUBLANES, TM), lambda i: (0, i))]
_POST_OUT_SHAPES = [jax.ShapeDtypeStruct((T, D), F32),
                    jax.ShapeDtypeStruct((T * ROW_CHUNKS, LANES), F32),
                    jax.ShapeDtypeStruct((SUBLANES, T), jnp.int32),
                    jax.ShapeDtypeStruct((SUBLANES, T), F32)]


def _ret_out_kernel(ogp_ref, ogs_ref, w_ref, xp_ref, xs_ref, mod_ref, g2_ref, wr_ref, br_ref, wd_ref,
                    x_out_ref, h_out_ref, ti_ref, tg_ref, wd_out_ref):
    wd_out_ref[...] = wd_ref[0].astype(BF16)
    i = pl.program_id(0)
    og = jnp.where(i < T_PROMPT // TM, ogp_ref[...], ogs_ref[...])
    y = jnp.dot(og, w_ref[...], preferred_element_type=F32)
    _post_mixer(_pick_x(i, TM, xp_ref, xs_ref), y, mod_ref[0, 0], g2_ref, wr_ref[0], br_ref[0],
                x_out_ref, h_out_ref, ti_ref, tg_ref, TM)


assert T // TM == N_EXP
_WD_OUT_SPEC = pl.BlockSpec((1, D_EXP, D), lambda i: (i, 0, 0))
_WD_OUT_SHAPE = jax.ShapeDtypeStruct((N_EXP, D_EXP, D), BF16)


def _wd_in_spec(layer):
    return pl.BlockSpec((1, 1, D_EXP, D), lambda i: (layer, i, 0, 0))


def _ret_out(og_p, og_s, w_out, xp, xs, mod4, g_norm2, wr_t, br_c, moe_w_down):
    n_p = T_PROMPT // TM
    return pl.pallas_call(
        _ret_out_kernel,
        grid=(T // TM,),
        in_specs=[pl.BlockSpec((TM, HV), lambda i: (jnp.minimum(i, n_p - 1), 0)),
                  pl.BlockSpec((TM, HV), lambda i: (jnp.maximum(i - n_p, 0), 0)),
                  pl.BlockSpec((HV, D), lambda i: (0, 0))]
        + _x_specs(TM) + _post_in_specs(0) + [_wd_in_spec(0)],
        out_specs=_POST_OUT_SPECS + [_WD_OUT_SPEC],
        out_shape=_POST_OUT_SHAPES + [_WD_OUT_SHAPE],
        compiler_params=_cparams(("arbitrary",)),
        name="ret_out",
    )(og_p, og_s, w_out, xp, xs, mod4, g_norm2, wr_t, br_c, moe_w_down)


def _route_a_kernel(ti_ref, dest_ref, meta_ref):
    tt, tm = MOE_TILE, MOE_BLK
    ti = ti_ref[...]
    ie = lax.broadcasted_iota(jnp.int32, (N_EXP, tt), 0)
    onehots = [(ie == ti[k:k + 1]).astype(F32) for k in range(TOP_K)]
    oh = onehots[0] + onehots[1] + onehots[2] + onehots[3]
    ch = 512
    upper = (lax.broadcasted_iota(jnp.int32, (ch, ch), 0)
             < lax.broadcasted_iota(jnp.int32, (ch, ch), 1)).astype(BF16)
    carry = jnp.zeros((N_EXP, 1), F32)
    cums = []
    for c in range(tt // ch):
        blk = oh[:, c * ch:(c + 1) * ch]
        cums.append(jnp.dot(blk.astype(BF16), upper, preferred_element_type=F32) + carry)
        carry = carry + jnp.sum(blk, axis=1, keepdims=True)
    cum = jnp.concatenate(cums, axis=1)
    cnt = carry
    nb = jnp.floor((cnt + (tm - 1.0)) * (1.0 / tm))
    lower = (lax.broadcasted_iota(jnp.int32, (N_EXP, N_EXP), 1)
             < lax.broadcasted_iota(jnp.int32, (N_EXP, N_EXP), 0)).astype(BF16)
    offb = jnp.dot(lower, jnp.broadcast_to(nb, (N_EXP, LANES)).astype(BF16),
                   preferred_element_type=F32)[:, :1]
    off = offb * tm
    base = off + cum
    dests = [jnp.sum(onehots[k] * base, axis=0, keepdims=True) for k in range(TOP_K)]
    dests.append(jnp.zeros((SUBLANES - TOP_K, tt), F32))
    dest_ref[0] = jnp.concatenate(dests, axis=0).astype(jnp.int32)
    nused = jnp.sum(nb, axis=0, keepdims=True)
    jl = lax.broadcasted_iota(jnp.int32, (N_EXP, META_LANES), 1).astype(F32)
    jc = jnp.minimum(jl, nused - 1.0)
    be = jnp.minimum(jnp.sum(((offb + nb) <= jc).astype(F32), axis=0, keepdims=True), N_EXP - 1.0)
    ief = lax.broadcasted_iota(jnp.int32, (N_EXP, META_LANES), 0).astype(F32)
    end_row = jnp.sum(jnp.where(ief == be, off + cnt, 0.0), axis=0, keepdims=True)
    nvalid = jnp.clip(end_row - jl[:1] * tm, 0.0, float(tm))
    nvalid = jnp.where(jl[:1] < nused, nvalid, 0.0)
    run_end = jnp.sum(jnp.where(ief == be, offb + nb, 0.0), axis=0, keepdims=True)
    nxt = jnp.sum(((offb + nb) <= run_end).astype(F32), axis=0, keepdims=True)
    nxt = jnp.where(run_end < nused, nxt, -1.0)
    meta = jnp.concatenate([be, nvalid, jnp.broadcast_to(nused, (1, META_LANES)), nxt,
                            jnp.zeros((SUBLANES - 4, META_LANES), F32)], axis=0)
    meta_ref[0] = meta.astype(jnp.int32)


def _route_a(ti):
    return pl.pallas_call(
        _route_a_kernel,
        grid=(N_MOE_TILES,),
        in_specs=[pl.BlockSpec((SUBLANES, MOE_TILE), lambda i: (0, i))],
        out_specs=[pl.BlockSpec((1, SUBLANES, MOE_TILE), lambda i: (i, 0, 0)),
                   pl.BlockSpec((1, SUBLANES, META_LANES), lambda i: (i, 0, 0))],
        out_shape=[jax.ShapeDtypeStruct((N_MOE_TILES, SUBLANES, MOE_TILE), jnp.int32),
                   jax.ShapeDtypeStruct((N_MOE_TILES, SUBLANES, META_LANES), jnp.int32)],
        compiler_params=_cparams(("arbitrary",)),
        name="route_a",
    )(ti)


ROUTE_STEPS = N_EXP // N_MOE_TILES


def _cast_route_kernel(wu_ref, dest_ref, meta_ref, ou_ref, slot_ref):
    ou_ref[...] = wu_ref[0].astype(BF16)
    part = pl.program_id(0) % ROUTE_STEPS
    group = SUBLANES

    def per_block(j, carry):
        def pad(s, c2):
            slot_ref[j * MOE_BLK + s] = MOE_TILE * ROW_CHUNKS
            return c2

        def pad_group(gi, c2):
            slots = slot_ref.at[pl.ds(j * MOE_BLK + gi * group, group)]
            for u in range(group):
                slots[u] = MOE_TILE * ROW_CHUNKS
            return c2

        n_real = meta_ref[0, 1, j]
        first_group = (n_real + group - 1) // group
        lax.fori_loop(n_real, first_group * group, pad, 0)
        lax.fori_loop(first_group, MOE_BLK // group, pad_group, 0)
        return carry

    blocks = MOE_NBLK // ROUTE_STEPS
    lax.fori_loop(part * blocks, (part + 1) * blocks, per_block, 0)

    tokens = MOE_TILE // ROUTE_STEPS

    def per_group(tg, carry):
        t0 = part * tokens + tg * group
        rows = [dest_ref.at[0, k, pl.ds(t0, group)] for k in range(TOP_K)]
        tok_code = t0 * ((1 << CODE_SHIFT) + ROW_CHUNKS)
        for u0 in range(0, group, 2):
            loaded = [(k, u, rows[k][u]) for u in range(u0, u0 + 2) for k in range(TOP_K)]
            for k, u, d in loaded:
                slot_ref[d] = tok_code + ((k * MOE_TILE + u) << CODE_SHIFT) + u * ROW_CHUNKS
        return carry

    lax.fori_loop(0, tokens // group, per_group, 0)


def _cast_route(w_up, layer, dest, meta):
    tile = lambda e: e // ROUTE_STEPS
    return pl.pallas_call(
        _cast_route_kernel,
        grid=(N_EXP,),
        in_specs=[pl.BlockSpec((1, 1) + w_up.shape[2:], lambda e: (layer, e, 0, 0)),
                  pl.BlockSpec((1, SUBLANES, MOE_TILE), lambda e: (tile(e), 0, 0), memory_space=pltpu.SMEM),
                  pl.BlockSpec((1, SUBLANES, META_LANES), lambda e: (tile(e), 0, 0),
                               memory_space=pltpu.SMEM)],
        out_specs=[pl.BlockSpec((1,) + w_up.shape[2:], lambda e: (e, 0, 0)),
                   pl.BlockSpec((MOE_SLOTS,), lambda e: (tile(e),), memory_space=pltpu.SMEM)],
        out_shape=[jax.ShapeDtypeStruct(w_up.shape[1:], BF16),
                   jax.ShapeDtypeStruct((N_MOE_TILES * MOE_SLOTS,), jnp.int32)],
        compiler_params=_cparams(("arbitrary",)),
        name="cast_route",
    )(w_up, dest, meta)


def _moe_kernel(be_ref, nu_ref, nx_ref, slot_ref, gate_ref, bu_ref, bd_ref, h_hbm, wu_hbm, wd_hbm,
                out_hbm, g0_scr, g1_scr, y0_scr, y1_scr, wu_scr, wd_scr, h_scr, out_scr, sem, tile_sem):
    i = pl.program_id(0)
    tm = MOE_BLK
    nused = nu_ref[i]
    base = i * MOE_NBLK

    def weight_copies(e, buf):
        return (pltpu.make_async_copy(wu_hbm.at[e], wu_scr.at[buf], sem.at[0, buf]),
                pltpu.make_async_copy(wd_hbm.at[e], wd_scr.at[buf], sem.at[1, buf]))
    tile_rows = MOE_TILE * ROW_CHUNKS
    h_copy = pltpu.make_async_copy(h_hbm.at[pl.ds(i * tile_rows, tile_rows)], h_scr, tile_sem.at[0])
    out_copy = pltpu.make_async_copy(out_scr, out_hbm.at[i], tile_sem.at[1])

    def row_slice(ii):
        return slice(ii * ROW_CHUNKS, (ii + 1) * ROW_CHUNKS)

    def gather(blk, g_scr, lo, hi):
        codes = slot_ref.at[pl.ds(blk * tm, tm)]
        for ii in range(lo, hi):
            off = pl.multiple_of(codes[ii] & (tile_rows - 1), ROW_CHUNKS)
            g_scr[row_slice(ii), :] = h_scr[pl.ds(off, ROW_CHUNKS), :]

    def scatter(blk, y_scr, lo, hi):
        batch = 8
        codes = slot_ref.at[pl.ds(blk * tm, tm)]
        for i0 in range(lo, hi, batch):
            pending = []
            for ii in range(i0, i0 + batch):
                code = codes[ii]
                gate = gate_ref[code >> CODE_SHIFT]
                off = code & ((1 << CODE_SHIFT) - 1)
                rows = pl.ds(pl.multiple_of(off, ROW_CHUNKS), ROW_CHUNKS)
                pending.append((rows, out_scr[rows, :] + gate * y_scr[row_slice(ii), :]))
            for rows, val in pending:
                out_scr[rows, :] = val

    h_copy.start()
    for copy in weight_copies(be_ref[base], 0):
        copy.start()
    out_scr[...] = jnp.zeros(out_scr.shape, F32)
    y1_scr[...] = jnp.zeros(y1_scr.shape, F32)
    h_copy.wait()
    gather(0, g0_scr, 0, tm)

    def step(j, run, g_cur, g_nxt, y_cur, y_prv):
        e = be_ref[base + j]
        jp = jnp.maximum(j - 1, 0)
        jn = jnp.minimum(j + 1, nused - 1)
        first = jnp.logical_or(j == 0, be_ref[base + jp] != e)
        run = run + jnp.where(jnp.logical_and(first, j > 0), 1, 0)
        buf = run & 1

        @pl.when(first)
        def _():
            for copy in weight_copies(e, buf):
                copy.wait()
            nxt = nx_ref[base + j]

            @pl.when(nxt >= 0)
            def _():
                for copy in weight_copies(nxt, 1 - buf):
                    copy.start()

        def compute(wbuf):
            a = _load_token_major(g_cur, tm).astype(BF16)
            up = jnp.dot(a, wu_scr[wbuf], preferred_element_type=F32) + bu_ref[e]
            scatter(jp, y_prv, 0, tm)
            gather(jn, g_nxt, 0, tm)
            glu = jnp.minimum(up[:, :D_EXP], SWIGLU_LIMIT)
            lin = jnp.clip(up[:, D_EXP:], -SWIGLU_LIMIT, SWIGLU_LIMIT)
            act = glu * _sigmoid(SWIGLU_ALPHA * glu) * (lin + 1.0)
            y = jnp.dot(act.astype(BF16), wd_scr[wbuf], preferred_element_type=F32) + bd_ref[e]
            _store_token_major(y_cur, y, tm)

        for wbuf in range(2):
            pl.when(buf == wbuf)(functools.partial(compute, wbuf))

        @pl.when(j == nused - 1)
        def _():
            scatter(j, y_cur, 0, tm)

        return run

    def pair(jj, run):
        run = step(2 * jj, run, g0_scr, g1_scr, y0_scr, y1_scr)
        return lax.cond(2 * jj + 1 < nused,
                        lambda r: step(2 * jj + 1, r, g1_scr, g0_scr, y1_scr, y0_scr),
                        lambda r: r, run)

    lax.fori_loop(0, (nused + 1) // 2, pair, jnp.int32(0))
    out_copy.start()
    out_copy.wait()


def _moe(be, nu, nx, slot, gates, h_tm, w_up, b_up, w_down, b_down):
    row_buf = pltpu.VMEM((MOE_BLK * ROW_CHUNKS, LANES), F32)
    grid_spec = pltpu.PrefetchScalarGridSpec(
        num_scalar_prefetch=3,
        grid=(N_MOE_TILES,),
        in_specs=[
            pl.BlockSpec((MOE_SLOTS,), lambda i, *_: (i,), memory_space=pltpu.SMEM),
            pl.BlockSpec((TOP_K * MOE_TILE,), lambda i, *_: (i,), memory_space=pltpu.SMEM),
            pl.BlockSpec((N_EXP, 1, 2 * D_EXP), lambda i, *_: (0, 0, 0)),
            pl.BlockSpec((N_EXP, 1, D), lambda i, *_: (0, 0, 0)),
            pl.BlockSpec(memory_space=pl.ANY),
            pl.BlockSpec(memory_space=pl.ANY),
            pl.BlockSpec(memory_space=pl.ANY),
        ],
        out_specs=pl.BlockSpec(memory_space=pl.ANY),
        scratch_shapes=[row_buf, row_buf, row_buf, row_buf,
                        pltpu.VMEM((2, D, 2 * D_EXP), BF16), pltpu.VMEM((2, D_EXP, D), BF16),
                        pltpu.VMEM((MOE_TILE * ROW_CHUNKS, LANES), F32),
                        pltpu.VMEM(((MOE_TILE + 1) * ROW_CHUNKS, LANES), F32),
                        pltpu.SemaphoreType.DMA((2, 2)), pltpu.SemaphoreType.DMA((2,))],
    )
    return pl.pallas_call(
        _moe_kernel,
        grid_spec=grid_spec,
        out_shape=jax.ShapeDtypeStruct((N_MOE_TILES, (MOE_TILE + 1) * ROW_CHUNKS, LANES), F32),
        compiler_params=_cparams(("arbitrary",)),
        name="moe",
    )(be, nu, nx, slot, gates, b_up.reshape(N_EXP, 1, 2 * D_EXP), b_down.reshape(N_EXP, 1, D),
      h_tm, w_up, w_down)


def _moe_layer(layer, ti, tg, h_tm, w_up, b_up, w_down_bf16, b_down):
    dest, meta = _route_a(ti)
    w_up_bf16, slot = _cast_route(w_up, layer, dest, meta)
    be, nx = (meta[:, r, :MOE_NBLK].reshape(-1) for r in (0, 3))
    nu = meta[:, 2, 0]
    gates = tg[:TOP_K].reshape(TOP_K, N_MOE_TILES, MOE_TILE).transpose(1, 0, 2).reshape(-1)
    return _moe(be, nu, nx, slot, gates, h_tm, w_up_bf16, b_up[layer], w_down_bf16, b_down[layer])


_MOE_OUT_SPEC = pl.BlockSpec((1, TM * ROW_CHUNKS, LANES),
                             lambda i: (i // (MOE_TILE // TM), i % (MOE_TILE // TM), 0))


def _conv_kernel(x_ref, moe_ref, modp_ref, g1_ref, w1_ref, b1_ref, wdw_ref, bdw_ref, lng_ref, lnb_ref,
                 w2_ref, b2_ref, mod_ref, g2_ref, wr_ref, br_ref, wd_ref,
                 x_out_ref, h_out_ref, ti_ref, tg_ref, wd_out_ref, pad_scr, conv_scr, shift_scr):
    wd_out_ref[...] = wd_ref[0].astype(BF16)
    i = pl.program_id(0)
    mp = modp_ref[0, 0]
    m = mod_ref[0, 0]
    x = x_ref[...] + mp[5:6] * _load_token_major(moe_ref, TM, (0,))
    h = (_rms(x) * (g1_ref[0] * (1.0 + m[1:2])) + m[0:1]).astype(BF16)
    ag = jnp.dot(h, w1_ref[0], preferred_element_type=F32) + b1_ref[0]
    u = ag[:, :D] * _sigmoid(ag[:, D:])
    is_prompt = i < T_PROMPT // TM

    def fill(seg):
        pitch = seg + CONV_PAD
        for s in range(TM // seg):
            pad_scr[s * pitch:s * pitch + CONV_PAD, :] = jnp.zeros((CONV_PAD, D), F32)
            pad_scr[s * pitch + CONV_PAD:(s + 1) * pitch, :] = u[s * seg:(s + 1) * seg, :]
        end = (TM // seg) * pitch
        pad_scr[end:end + CONV_PAD, :] = jnp.zeros((CONV_PAD, D), F32)

    @pl.when(is_prompt)
    def _():
        fill(L_PROMPT)

    @pl.when(jnp.logical_not(is_prompt))
    def _():
        fill(GRID_W)

    per_seq = L_PROMPT // CONV_CHUNK
    halo = CONV_CHUNK + 2 * CONV_PAD

    def conv_chunk(c, carry):
        base = jnp.where(is_prompt, (c // per_seq) * (L_PROMPT + CONV_PAD) + (c % per_seq) * CONV_CHUNK,
                         c * (GRID_W + CONV_PAD))
        base = pl.multiple_of(base, SUBLANES)
        out_rows = pl.ds(pl.multiple_of(c * CONV_CHUNK, CONV_CHUNK), CONV_CHUNK)
        for gl in range(D // LANES):
            lanes = slice(gl * LANES, (gl + 1) * LANES)
            blk = pad_scr[pl.ds(base, halo), lanes]
            span = halo - SUBLANES
            for r in range(SUBLANES):
                shift_scr[gl * SUBLANES + r] = blk[r:r + span, :]
            acc = jnp.zeros((CONV_CHUNK, LANES), F32)
            for tap in range(CONV_W):
                lo = CONV_PAD - CONV_W // 2 + tap
                al = lo // SUBLANES * SUBLANES
                acc = acc + (shift_scr[gl * SUBLANES + lo % SUBLANES, al:al + CONV_CHUNK, :]
                             * wdw_ref[0, tap:tap + 1, lanes])
            conv_scr[out_rows, lanes] = acc + bdw_ref[0, :, lanes]
        return carry

    lax.fori_loop(0, TM // CONV_CHUNK, conv_chunk, 0)
    uc = conv_scr[...]
    mu = jnp.mean(uc, axis=-1, keepdims=True)
    var = jnp.mean(jnp.square(uc - mu), axis=-1, keepdims=True)
    z = (uc - mu) * lax.rsqrt(var + EPS) * lng_ref[0] + lnb_ref[0]
    z = (z * _sigmoid(z)).astype(BF16)
    y = jnp.dot(z, w2_ref[0], preferred_element_type=F32) + b2_ref[0]
    _post_mixer(x, y, m, g2_ref, wr_ref[0], br_ref[0], x_out_ref, h_out_ref, ti_ref, tg_ref, TM)


def _conv_layer(x1, moe0, mod4, g_norm1, w_pw1, b_pw1, w_dw, b_dw, ln_g, ln_b, w_pw2, b_pw2,
                g_norm2, wr_t, br_c, moe_w_down):
    def full(shape):
        return pl.BlockSpec(shape, lambda i: (0,) * len(shape))

    pad_rows = (TM // GRID_W) * (GRID_W + CONV_PAD) + CONV_PAD
    return pl.pallas_call(
        _conv_kernel,
        grid=(T // TM,),
        in_specs=[pl.BlockSpec((TM, D), lambda i: (i, 0)),
                  _MOE_OUT_SPEC,
                  pl.BlockSpec((1, 1, 6, D), lambda i: (0, _cond_of_tile(i, TM), 0, 0)),
                  pl.BlockSpec((1, 1, D), lambda i: (1, 0, 0)),
                  full((1, D, 2 * D)), full((1, 1, 2 * D)), full((1, CONV_W, D)), full((1, 1, D)),
                  full((1, 1, D)), full((1, 1, D)), full((1, D, D)), full((1, 1, D))]
        + _post_in_specs(1) + [_wd_in_spec(1)],
        out_specs=_POST_OUT_SPECS + [_WD_OUT_SPEC],
        out_shape=_POST_OUT_SHAPES + [_WD_OUT_SHAPE],
        scratch_shapes=[pltpu.VMEM((pad_rows, D), F32), pltpu.VMEM((TM, D), F32),
                        pltpu.VMEM((D // LANES * SUBLANES, CONV_CHUNK + 2 * CONV_PAD - SUBLANES, LANES), F32)],
        compiler_params=_cparams(("arbitrary",)),
        name="conv",
    )(x1, moe0, mod4, g_norm1, w_pw1, b_pw1.reshape(1, 1, 2 * D), w_dw, b_dw.reshape(1, 1, D),
      ln_g.reshape(1, 1, D), ln_b.reshape(1, 1, D), w_pw2, b_pw2.reshape(1, 1, D),
      mod4, g_norm2, wr_t, br_c, moe_w_down)


def _final_kernel(x_ref, moe_ref, mod_ref, g_ref, yp_ref, ys_ref):
    i = pl.program_id(0)
    x = x_ref[...] + mod_ref[0, 0][5:6] * _load_token_major(moe_ref, TM, (0,))
    y = _rms(x) * g_ref[...]

    @pl.when(i < T_PROMPT // TM)
    def _():
        yp_ref[...] = y

    @pl.when(i >= T_PROMPT // TM)
    def _():
        ys_ref[...] = y


def _final(x, moe1, mod4, g_final):
    n_p = T_PROMPT // TM
    return pl.pallas_call(
        _final_kernel,
        grid=(T // TM,),
        in_specs=[pl.BlockSpec((TM, D), lambda i: (i, 0)),
                  _MOE_OUT_SPEC,
                  pl.BlockSpec((1, 1, 6, D), lambda i: (1, _cond_of_tile(i, TM), 0, 0)),
                  pl.BlockSpec((1, D), lambda i: (0, 0))],
        out_specs=[pl.BlockSpec((TM, D), lambda i: (jnp.minimum(i, n_p - 1), 0)),
                   pl.BlockSpec((TM, D), lambda i: (jnp.maximum(i - n_p, 0), 0))],
        out_shape=[jax.ShapeDtypeStruct((T_PROMPT, D), F32), jax.ShapeDtypeStruct((T_SAMPLE, D), F32)],
        compiler_params=_cparams(("arbitrary",)),
        name="final",
    )(x, moe1, mod4, g_final)


def kernel(x_prompt, x_sample, c, state_ret_fwd, state_ret_bwd, c_ctx, w_ada, b_ada, g_norm1, g_norm2,
           ret_w_in, ret_decay, ret_w_out, conv_w_pw1, conv_b_pw1, conv_w_dw, conv_b_dw, conv_ln_g,
           conv_ln_b, conv_w_pw2, conv_b_pw2, moe_w_router, moe_b_router, moe_w_up, moe_b_up,
           moe_w_down, moe_b_down, g_final):
    xp = x_prompt.reshape(T_PROMPT, D)
    xs = x_sample.reshape(T_SAMPLE, D)
    cond = jnp.concatenate([c_ctx[None, :], c, jnp.zeros((SUBLANES - 1 - N_SAMPLE, D), F32)], axis=0)
    mod = _ada(cond.T, w_ada, b_ada)
    mod4 = mod.reshape(mod.shape[0], SUBLANES, 6, D)
    wr_t = jnp.swapaxes(moe_w_router, 1, 2)
    br_c = moe_b_router[:, :, None]
    gn1 = g_norm1[:, None, :]
    gn2 = g_norm2[:, None, :]

    proj = _proj(xp, xs, g_norm1[0:1], mod4, ret_w_in[0].astype(BF16))
    og_p, new_f, new_b = _ret_prompt(ret_decay[0], proj)
    og_s = _ret_sample(ret_decay[0], proj, state_ret_fwd, state_ret_bwd)
    x1, h_tm, ti, tg, w_down_bf16 = _ret_out(og_p, og_s, ret_w_out[0].astype(BF16), xp, xs, mod4, gn2,
                                             wr_t, br_c, moe_w_down)
    moe0 = _moe_layer(0, ti, tg, h_tm, moe_w_up, moe_b_up, w_down_bf16, moe_b_down)

    x2, h_tm, ti, tg, w_down_bf16 = _conv_layer(
        x1, moe0, mod4, gn1, conv_w_pw1.astype(BF16), conv_b_pw1, conv_w_dw, conv_b_dw, conv_ln_g,
        conv_ln_b, conv_w_pw2.astype(BF16), conv_b_pw2, gn2, wr_t, br_c, moe_w_down)
    moe1 = _moe_layer(1, ti, tg, h_tm, moe_w_up, moe_b_up, w_down_bf16, moe_b_down)

    y_p, y_s = _final(x2, moe1, mod4, g_final[None, :])
    return (y_p.reshape(N_PROMPT, L_PROMPT, D), y_s.reshape(N_SAMPLE, L_SAMPLE, D), new_f, new_b)
```

```python
import functools

import numpy as np
import jax
import jax.numpy as jnp
from jax import lax
from jax.experimental import pallas as pl
from jax.experimental.pallas import tpu as pltpu

F32 = jnp.float32
BF16 = jnp.bfloat16

D = 1024
N_PROMPT, L_PROMPT = 32, 256
N_SAMPLE, L_SAMPLE = 2, 4096
T_PROMPT = N_PROMPT * L_PROMPT
T_SAMPLE = N_SAMPLE * L_SAMPLE
T = T_PROMPT + T_SAMPLE
GRID_W = 64
HEADS, DK, DV = 4, 256, 512
HK, HV = HEADS * DK, HEADS * DV
N_PROJ = 2 * HK + 2 * HV
ROPE_BASE = 10000.0
CONV_W = 31
CONV_PAD = 16
CONV_CHUNK = 64
N_EXP, TOP_K, D_EXP = 32, 4, 1024
SWIGLU_LIMIT, SWIGLU_ALPHA = 7.0, 1.702
EPS = 1e-6

LANES = 128
SUBLANES = 8
ROW_CHUNKS = D // LANES
VMEM_LIMIT = 56 * 1024 * 1024

TM = 512
MOE_TILE = 4096
MOE_BLK = 256
MOE_NBLK = MOE_TILE * TOP_K // MOE_BLK + N_EXP
MOE_SLOTS = MOE_NBLK * MOE_BLK
MOE_SPLITS = 2
META_LANES = 2 * LANES
CODE_SHIFT = 16
N_MOE_TILES = T // MOE_TILE
RET_CHUNK = 256

NT_DIMS = (((1,), (1,)), ((), ()))
TN_DIMS = (((0,), (0,)), ((), ()))


def _cparams(sem):
    return pltpu.CompilerParams(dimension_semantics=sem, vmem_limit_bytes=VMEM_LIMIT)


def _cond_of_tile(i, tm):
    return jnp.maximum((i * tm) // L_SAMPLE - 1, 0)


def _sigmoid(x):
    return 0.5 * jnp.tanh(0.5 * x) + 0.5


def _rms(x):
    return x * lax.rsqrt(jnp.mean(x * x, axis=-1, keepdims=True) + EPS)


def _pick_x(i, tm, xp_ref, xs_ref):
    return jnp.where(i < T_PROMPT // tm, xp_ref[...], xs_ref[...])


def _x_specs(tm):
    n_p = T_PROMPT // tm
    return [pl.BlockSpec((tm, D), lambda i, *_: (jnp.minimum(i, n_p - 1), 0)),
            pl.BlockSpec((tm, D), lambda i, *_: (jnp.maximum(i - n_p, 0), 0))]


def _ada_kernel(ct_ref, w_ref, b_ref, o_ref):
    ct = ct_ref[...]
    s = ct * _sigmoid(ct)
    w = w_ref[0]
    rows = [jnp.sum(w * s[:, r:r + 1], axis=0, keepdims=True) for r in range(3)]
    rows.append(jnp.zeros((SUBLANES - 3, w.shape[1]), F32))
    o_ref[0] = jnp.concatenate(rows, axis=0) + b_ref[0]


def _ada(ct, w_ada, b_ada):
    depth, _, n = w_ada.shape
    tn = 1536
    return pl.pallas_call(
        _ada_kernel,
        grid=(depth, n // tn),
        in_specs=[pl.BlockSpec((D, SUBLANES), lambda l, j: (0, 0)),
                  pl.BlockSpec((1, D, tn), lambda l, j: (l, 0, j)),
                  pl.BlockSpec((1, 1, tn), lambda l, j: (l, 0, j))],
        out_specs=pl.BlockSpec((1, SUBLANES, tn), lambda l, j: (l, 0, j)),
        out_shape=jax.ShapeDtypeStruct((depth, SUBLANES, n), F32),
        compiler_params=_cparams(("arbitrary", "arbitrary")),
        name="ada",
    )(ct, w_ada, b_ada.reshape(depth, 1, n))


def _rope_tables():
    t = np.arange(L_SAMPLE)
    row = (t // GRID_W).astype(np.float32)
    col = (t % GRID_W).astype(np.float32)
    nf = DK // 4
    inv = (np.float32(ROPE_BASE) ** (-np.arange(nf, dtype=np.float32) / np.float32(nf))).astype(np.float32)
    cos, sin = [], []
    for pos in (row, col):
        ang = (pos[:, None] * inv[None, :]).astype(np.float32)
        c, s = np.cos(ang).astype(np.float32), np.sin(ang).astype(np.float32)
        cos += [c, c]
        sin += [-s, s]
    return np.concatenate(cos, axis=1), np.concatenate(sin, axis=1)


def _proj_kernel(xp_ref, xs_ref, g_ref, mod_ref, w_ref, cos_ref, sin_ref, o_ref):
    i = pl.program_id(0)
    m = mod_ref[0, 0]
    h = _rms(_pick_x(i, TM, xp_ref, xs_ref)) * (g_ref[...] * (1.0 + m[1:2]))
    h = (h + m[0:1]).astype(BF16)
    is_sample = i >= T_PROMPT // TM
    cos = jnp.where(is_sample, cos_ref[...], 1.0)
    sin = jnp.where(is_sample, sin_ref[...], 0.0)
    for col in range(N_PROJ // D):
        acc = jnp.dot(h, w_ref[:, col * D:(col + 1) * D], preferred_element_type=F32)
        if col == 1:
            acc = acc * DK ** -0.5
        if col < 2:
            for c in range(D // LANES):
                a = acc[:, c * LANES:(c + 1) * LANES]
                p = (c % 2) * LANES
                r = a * cos[:, p:p + LANES] + pltpu.roll(a, LANES // 2, 1) * sin[:, p:p + LANES]
                o_ref[:, col * D + c * LANES:col * D + (c + 1) * LANES] = r.astype(BF16)
        else:
            o_ref[:, col * D:(col + 1) * D] = acc.astype(BF16)


def _proj(xp, xs, g1, mod4, w_in):
    cos, sin = _rope_tables()
    n_p = T_PROMPT // TM
    n_s = L_SAMPLE // TM
    tab_spec = pl.BlockSpec((TM, DK), lambda i: (jnp.maximum(i - n_p, 0) % n_s, 0))
    return pl.pallas_call(
        _proj_kernel,
        grid=(T // TM,),
        in_specs=_x_specs(TM) + [
            pl.BlockSpec((1, D), lambda i: (0, 0)),
            pl.BlockSpec((1, 1, 6, D), lambda i: (0, _cond_of_tile(i, TM), 0, 0)),
            pl.BlockSpec((D, N_PROJ), lambda i: (0, 0)),
            tab_spec, tab_spec],
        out_specs=pl.BlockSpec((TM, N_PROJ), lambda i: (i, 0)),
        out_shape=jax.ShapeDtypeStruct((T, N_PROJ), BF16),
        compiler_params=_cparams(("arbitrary",)),
        name="proj",
    )(xp, xs, g1, mod4, w_in, jnp.asarray(cos), jnp.asarray(sin))


def _log_decays(dec_ref, head):
    out = []
    for direction in range(2):
        d = jnp.full((1, 1), dec_ref[direction, head], F32)
        out.append(jnp.minimum(d, 0.0) - jnp.log(1.0 + jnp.exp(-jnp.abs(d))))
    return out


def _decay_mask(lgf, lgb, c):
    ii = lax.broadcasted_iota(jnp.int32, (c, c), 0)
    jj = lax.broadcasted_iota(jnp.int32, (c, c), 1)
    diff = (ii - jj).astype(F32)
    fwd = jnp.where(diff >= 0, jnp.exp(lgf * jnp.maximum(diff, 0.0)), 0.0)
    bwd = jnp.where(diff <= 0, jnp.exp(lgb * jnp.maximum(-diff, 0.0)), 0.0)
    return fwd + bwd


def _norm_gate(o, g):
    g = g.astype(F32)
    return (_rms(o) * (g * _sigmoid(g))).astype(BF16)


def _ret_prompt_kernel(dec_ref, p_ref, o_ref, sf_ref, sb_ref):
    c = L_PROMPT
    pos = lax.broadcasted_iota(jnp.int32, (c, 1), 0).astype(F32)
    for head in range(HEADS):
        lgf, lgb = _log_decays(dec_ref, head)
        q = p_ref[:, head * DK:(head + 1) * DK]
        k = p_ref[:, HK + head * DK:HK + (head + 1) * DK]
        v = p_ref[:, 2 * HK + head * DV:2 * HK + (head + 1) * DV]
        g = p_ref[:, 2 * HK + HV + head * DV:2 * HK + HV + (head + 1) * DV]
        s = lax.dot_general(q, k, NT_DIMS, preferred_element_type=F32) * _decay_mask(lgf, lgb, c)
        o = jnp.dot(s.astype(BF16), v, preferred_element_type=F32)
        o_ref[:, head * DV:(head + 1) * DV] = _norm_gate(o, g)
        kf = k.astype(F32)
        k_fwd = (kf * jnp.exp(lgf * (c - 1.0 - pos))).astype(BF16)
        k_bwd = (kf * jnp.exp(lgb * pos)).astype(BF16)
        sf_ref[0, 0, head] = lax.dot_general(k_fwd, v, TN_DIMS, preferred_element_type=F32)
        sb_ref[0, 0, head] = lax.dot_general(k_bwd, v, TN_DIMS, preferred_element_type=F32)


def _ret_specs(seq_len, row0):
    r = row0 // seq_len
    return [pl.BlockSpec((seq_len, DK), lambda b, h: (r + b, h)),
            pl.BlockSpec((seq_len, DK), lambda b, h: (r + b, HK // DK + h)),
            pl.BlockSpec((seq_len, DV), lambda b, h: (r + b, 2 * HK // DV + h)),
            pl.BlockSpec((seq_len, DV), lambda b, h: (r + b, (2 * HK + HV) // DV + h))]


def _ret_prompt(decay, proj):
    state = jax.ShapeDtypeStruct((N_PROMPT, 1, HEADS, DK, DV), F32)
    state_spec = pl.BlockSpec((1, 1, HEADS, DK, DV), lambda b: (b, 0, 0, 0, 0))
    return pl.pallas_call(
        _ret_prompt_kernel,
        grid=(N_PROMPT,),
        in_specs=[pl.BlockSpec(memory_space=pltpu.SMEM),
                  pl.BlockSpec((L_PROMPT, N_PROJ), lambda b: (b, 0))],
        out_specs=[pl.BlockSpec((L_PROMPT, HV), lambda b: (b, 0)), state_spec, state_spec],
        out_shape=[jax.ShapeDtypeStruct((T_PROMPT, HV), BF16), state, state],
        compiler_params=_cparams(("arbitrary",)),
        name="ret_prompt",
    )(decay, proj)


def _ret_sample_kernel(dec_ref, q_ref, k_ref, v_ref, g_ref, s0f_ref, s0b_ref, o_ref,
                       of_scr, ob_scr, sf_scr, sb_scr, dm_scr):
    c = RET_CHUNK
    nc = L_SAMPLE // c
    lgf, lgb = _log_decays(dec_ref, pl.program_id(1))
    dm_scr[...] = _decay_mask(lgf, lgb, c)
    pos = lax.broadcasted_iota(jnp.int32, (c, 1), 0).astype(F32)

    def chunk(ci):
        rows = pl.ds(pl.multiple_of(ci * c, c), c)
        return rows, q_ref[rows, :], k_ref[rows, :], v_ref[rows, :]

    def state_update(s_scr, lg, write_pos, kc, vc):
        kw = (kc.astype(F32) * jnp.exp(lg * write_pos)).astype(BF16)
        s_scr[...] = s_scr[...] * jnp.exp(lg * c) + lax.dot_general(kw, vc, TN_DIMS, preferred_element_type=F32)

    def read_state(s_scr, lg, read_pos, qc):
        qr = (qc.astype(F32) * jnp.exp(lg * read_pos)).astype(BF16)
        return jnp.dot(qr, s_scr[...].astype(BF16), preferred_element_type=F32)

    sf_scr[...] = s0f_ref[0, 0, 0]
    sb_scr[...] = s0b_ref[0, 0, 0]

    def scan(step, carry):
        rows, qc, kc, vc = chunk(nc - 1 - step)
        ob_scr[rows, :] = read_state(sb_scr, lgb, c - pos, qc)
        state_update(sb_scr, lgb, pos, kc, vc)
        rows, qc, kc, vc = chunk(step)
        s = lax.dot_general(qc, kc, NT_DIMS, preferred_element_type=F32) * dm_scr[...]
        o = jnp.dot(s.astype(BF16), vc, preferred_element_type=F32)
        of_scr[rows, :] = o + read_state(sf_scr, lgf, pos + 1.0, qc)
        state_update(sf_scr, lgf, c - 1.0 - pos, kc, vc)
        return carry

    lax.fori_loop(0, nc, scan, 0)

    def finish(ci, carry):
        rows = pl.ds(pl.multiple_of(ci * c, c), c)
        o_ref[rows, :] = _norm_gate(of_scr[rows, :] + ob_scr[rows, :], g_ref[rows, :])
        return carry

    lax.fori_loop(0, nc, finish, 0)


def _ret_sample(decay, proj, s0f, s0b):
    state_spec = pl.BlockSpec((1, 1, 1, DK, DV), lambda b, h: (b, 0, h, 0, 0))
    return pl.pallas_call(
        _ret_sample_kernel,
        grid=(N_SAMPLE, HEADS),
        in_specs=[pl.BlockSpec(memory_space=pltpu.SMEM)] + _ret_specs(L_SAMPLE, T_PROMPT)
        + [state_spec, state_spec],
        out_specs=pl.BlockSpec((L_SAMPLE, DV), lambda b, h: (b, h)),
        out_shape=jax.ShapeDtypeStruct((T_SAMPLE, HV), BF16),
        scratch_shapes=[pltpu.VMEM((L_SAMPLE, DV), F32), pltpu.VMEM((L_SAMPLE, DV), F32),
                        pltpu.VMEM((DK, DV), F32), pltpu.VMEM((DK, DV), F32),
                        pltpu.VMEM((RET_CHUNK, RET_CHUNK), F32)],
        compiler_params=_cparams(("arbitrary", "arbitrary")),
        name="ret_sample",
    )(decay, proj, proj, proj, proj, s0f, s0b)


def _store_token_major(ref, val, tm):
    for s in range(ROW_CHUNKS):
        ref[pl.ds(s, tm, stride=ROW_CHUNKS), :] = val[:, s * LANES:(s + 1) * LANES]


def _load_token_major(ref, tm, lead=()):
    return jnp.concatenate([ref[lead + (pl.ds(s, tm, stride=ROW_CHUNKS), slice(None))]
                            for s in range(ROW_CHUNKS)], axis=1)


def _post_mixer(x, y, m, g2_ref, wr_ref, br_ref, x_out_ref, h_out_ref, ti_ref, tg_ref, tm):
    x1 = x + m[2:3] * y
    h = _rms(x1) * (g2_ref[0] * (1.0 + m[4:5])) + m[3:4]
    x_out_ref[...] = x1
    _store_token_major(h_out_ref, h, tm)
    w = wr_ref[...]
    w_hi = w.astype(BF16)
    w_lo = (w - w_hi.astype(F32)).astype(BF16)
    h_hi = h.astype(BF16)
    h_lo = (h - h_hi.astype(F32)).astype(BF16)
    dot = functools.partial(lax.dot_general, dimension_numbers=NT_DIMS, preferred_element_type=F32)
    cur = dot(w_hi, h_hi) + dot(w_hi, h_lo) + dot(w_lo, h_hi) + br_ref[...]
    ie = lax.broadcasted_iota(jnp.int32, (N_EXP, tm), 0).astype(F32)
    vals, idxs = [], []
    for _ in range(TOP_K):
        top = jnp.max(cur, axis=0, keepdims=True)
        idx = jnp.min(jnp.where(cur == top, ie, float(N_EXP)), axis=0, keepdims=True)
        vals.append(top)
        idxs.append(idx)
        cur = jnp.where(ie == idx, -jnp.inf, cur)
    ex = [jnp.exp(v - vals[0]) for v in vals]
    den = ex[0] + ex[1] + ex[2] + ex[3]
    pad = jnp.zeros((SUBLANES - TOP_K, tm), F32)
    ti_ref[...] = jnp.concatenate(idxs + [pad], axis=0).astype(jnp.int32)
    tg_ref[...] = jnp.concatenate([e / den for e in ex] + [pad], axis=0)


def _post_in_specs(layer):
    return [pl.BlockSpec((1, 1, 6, D), lambda i: (layer, _cond_of_tile(i, TM), 0, 0)),
            pl.BlockSpec((1, 1, D), lambda i: (layer, 0, 0)),
            pl.BlockSpec((1, N_EXP, D), lambda i: (layer, 0, 0)),
            pl.BlockSpec((1, N_EXP, 1), lambda i: (layer, 0, 0))]


_POST_OUT_SPECS = [pl.BlockSpec((TM, D), lambda i: (i, 0)),
                   pl.BlockSpec((TM * ROW_CHUNKS, LANES), lambda i: (i, 0)),
                   pl.BlockSpec((SUBLANES, TM), lambda i: (0, i)),
                   pl.BlockSpec((SUBLANES, TM), lambda i: (0, i))]
_POST_OUT_SHAPES = [jax.ShapeDtypeStruct((T, D), F32),
                    jax.ShapeDtypeStruct((T * ROW_CHUNKS, LANES), F32),
                    jax.ShapeDtypeStruct((SUBLANES, T), jnp.int32),
                    jax.ShapeDtypeStruct((SUBLANES, T), F32)]


def _ret_out_kernel(ogp_ref, ogs_ref, w_ref, xp_ref, xs_ref, mod_ref, g2_ref, wr_ref, br_ref, wd_ref,
                    x_out_ref, h_out_ref, ti_ref, tg_ref, wd_out_ref):
    wd_out_ref[...] = wd_ref[0].astype(BF16)
    i = pl.program_id(0)
    og = jnp.where(i < T_PROMPT // TM, ogp_ref[...], ogs_ref[...])
    y = jnp.dot(og, w_ref[...], preferred_element_type=F32)
    _post_mixer(_pick_x(i, TM, xp_ref, xs_ref), y, mod_ref[0, 0], g2_ref, wr_ref[0], br_ref[0],
                x_out_ref, h_out_ref, ti_ref, tg_ref, TM)


assert T // TM == N_EXP
_WD_OUT_SPEC = pl.BlockSpec((1, D_EXP, D), lambda i: (i, 0, 0))
_WD_OUT_SHAPE = jax.ShapeDtypeStruct((N_EXP, D_EXP, D), BF16)


def _wd_in_spec(layer):
    return pl.BlockSpec((1, 1, D_EXP, D), lambda i: (layer, i, 0, 0))


def _ret_out(og_p, og_s, w_out, xp, xs, mod4, g_norm2, wr_t, br_c, moe_w_down):
    n_p = T_PROMPT // TM
    return pl.pallas_call(
        _ret_out_kernel,
        grid=(T // TM,),
        in_specs=[pl.BlockSpec((TM, HV), lambda i: (jnp.minimum(i, n_p - 1), 0)),
                  pl.BlockSpec((TM, HV), lambda i: (jnp.maximum(i - n_p, 0), 0)),
                  pl.BlockSpec((HV, D), lambda i: (0, 0))]
        + _x_specs(TM) + _post_in_specs(0) + [_wd_in_spec(0)],
        out_specs=_POST_OUT_SPECS + [_WD_OUT_SPEC],
        out_shape=_POST_OUT_SHAPES + [_WD_OUT_SHAPE],
        compiler_params=_cparams(("arbitrary",)),
        name="ret_out",
    )(og_p, og_s, w_out, xp, xs, mod4, g_norm2, wr_t, br_c, moe_w_down)


def _route_a_kernel(ti_ref, dest_ref, meta_ref):
    tt, tm = MOE_TILE, MOE_BLK
    ti = ti_ref[...]
    ie = lax.broadcasted_iota(jnp.int32, (N_EXP, tt), 0)
    onehots = [(ie == ti[k:k + 1]).astype(F32) for k in range(TOP_K)]
    oh = onehots[0] + onehots[1] + onehots[2] + onehots[3]
    ch = 512
    upper = (lax.broadcasted_iota(jnp.int32, (ch, ch), 0)
             < lax.broadcasted_iota(jnp.int32, (ch, ch), 1)).astype(BF16)
    carry = jnp.zeros((N_EXP, 1), F32)
    cums = []
    for c in range(tt // ch):
        blk = oh[:, c * ch:(c + 1) * ch]
        cums.append(jnp.dot(blk.astype(BF16), upper, preferred_element_type=F32) + carry)
        carry = carry + jnp.sum(blk, axis=1, keepdims=True)
    cum = jnp.concatenate(cums, axis=1)
    cnt = carry
    nb = jnp.floor((cnt + (tm - 1.0)) * (1.0 / tm))
    lower = (lax.broadcasted_iota(jnp.int32, (N_EXP, N_EXP), 1)
             < lax.broadcasted_iota(jnp.int32, (N_EXP, N_EXP), 0)).astype(BF16)
    offb = jnp.dot(lower, jnp.broadcast_to(nb, (N_EXP, LANES)).astype(BF16),
                   preferred_element_type=F32)[:, :1]
    off = offb * tm
    base = off + cum
    dests = [jnp.sum(onehots[k] * base, axis=0, keepdims=True) for k in range(TOP_K)]
    dests.append(jnp.zeros((SUBLANES - TOP_K, tt), F32))
    dest_ref[0] = jnp.concatenate(dests, axis=0).astype(jnp.int32)
    nused = jnp.sum(nb, axis=0, keepdims=True)
    jl = lax.broadcasted_iota(jnp.int32, (N_EXP, META_LANES), 1).astype(F32)
    jc = jnp.minimum(jl, nused - 1.0)
    be = jnp.minimum(jnp.sum(((offb + nb) <= jc).astype(F32), axis=0, keepdims=True), N_EXP - 1.0)
    ief = lax.broadcasted_iota(jnp.int32, (N_EXP, META_LANES), 0).astype(F32)
    end_row = jnp.sum(jnp.where(ief == be, off + cnt, 0.0), axis=0, keepdims=True)
    nvalid = jnp.clip(end_row - jl[:1] * tm, 0.0, float(tm))
    nvalid = jnp.where(jl[:1] < nused, nvalid, 0.0)
    run_end = jnp.sum(jnp.where(ief == be, offb + nb, 0.0), axis=0, keepdims=True)
    nxt = jnp.sum(((offb + nb) <= run_end).astype(F32), axis=0, keepdims=True)
    nxt = jnp.where(run_end < nused, nxt, -1.0)
    meta = jnp.concatenate([be, nvalid, jnp.broadcast_to(nused, (1, META_LANES)), nxt,
                            jnp.zeros((SUBLANES - 4, META_LANES), F32)], axis=0)
    meta_ref[0] = meta.astype(jnp.int32)


def _route_a(ti):
    return pl.pallas_call(
        _route_a_kernel,
        grid=(N_MOE_TILES,),
        in_specs=[pl.BlockSpec((SUBLANES, MOE_TILE), lambda i: (0, i))],
        out_specs=[pl.BlockSpec((1, SUBLANES, MOE_TILE), lambda i: (i, 0, 0)),
                   pl.BlockSpec((1, SUBLANES, META_LANES), lambda i: (i, 0, 0))],
        out_shape=[jax.ShapeDtypeStruct((N_MOE_TILES, SUBLANES, MOE_TILE), jnp.int32),
                   jax.ShapeDtypeStruct((N_MOE_TILES, SUBLANES, META_LANES), jnp.int32)],
        compiler_params=_cparams(("arbitrary",)),
        name="route_a",
    )(ti)


ROUTE_STEPS = N_EXP // N_MOE_TILES


def _cast_route_kernel(wu_ref, dest_ref, meta_ref, ou_ref, slot_ref):
    ou_ref[...] = wu_ref[0].astype(BF16)
    part = pl.program_id(0) % ROUTE_STEPS
    group = SUBLANES

    def per_block(j, carry):
        def pad(s, c2):
            slot_ref[j * MOE_BLK + s] = MOE_TILE * ROW_CHUNKS
            return c2

        def pad_group(gi, c2):
            slots = slot_ref.at[pl.ds(j * MOE_BLK + gi * group, group)]
            for u in range(group):
                slots[u] = MOE_TILE * ROW_CHUNKS
            return c2

        n_real = meta_ref[0, 1, j]
        first_group = (n_real + group - 1) // group
        lax.fori_loop(n_real, first_group * group, pad, 0)
        lax.fori_loop(first_group, MOE_BLK // group, pad_group, 0)
        return carry

    blocks = MOE_NBLK // ROUTE_STEPS
    lax.fori_loop(part * blocks, (part + 1) * blocks, per_block, 0)

    tokens = MOE_TILE // ROUTE_STEPS

    def per_group(tg, carry):
        t0 = part * tokens + tg * group
        rows = [dest_ref.at[0, k, pl.ds(t0, group)] for k in range(TOP_K)]
        tok_code = t0 * ((1 << CODE_SHIFT) + ROW_CHUNKS)
        for u0 in range(0, group, 2):
            loaded = [(k, u, rows[k][u]) for u in range(u0, u0 + 2) for k in range(TOP_K)]
            for k, u, d in loaded:
                slot_ref[d] = tok_code + ((k * MOE_TILE + u) << CODE_SHIFT) + u * ROW_CHUNKS
        return carry

    lax.fori_loop(0, tokens // group, per_group, 0)


def _cast_route(w_up, layer, dest, meta):
    tile = lambda e: e // ROUTE_STEPS
    return pl.pallas_call(
        _cast_route_kernel,
        grid=(N_EXP,),
        in_specs=[pl.BlockSpec((1, 1) + w_up.shape[2:], lambda e: (layer, e, 0, 0)),
                  pl.BlockSpec((1, SUBLANES, MOE_TILE), lambda e: (tile(e), 0, 0), memory_space=pltpu.SMEM),
                  pl.BlockSpec((1, SUBLANES, META_LANES), lambda e: (tile(e), 0, 0),
                               memory_space=pltpu.SMEM)],
        out_specs=[pl.BlockSpec((1,) + w_up.shape[2:], lambda e: (e, 0, 0)),
                   pl.BlockSpec((MOE_SLOTS,), lambda e: (tile(e),), memory_space=pltpu.SMEM)],
        out_shape=[jax.ShapeDtypeStruct(w_up.shape[1:], BF16),
                   jax.ShapeDtypeStruct((N_MOE_TILES * MOE_SLOTS,), jnp.int32)],
        compiler_params=_cparams(("arbitrary",)),
        name="cast_route",
    )(w_up, dest, meta)


def _moe_kernel(be_ref, nu_ref, nx_ref, slot_ref, gate_ref, bu_ref, bd_ref, h_hbm, wu_hbm, wd_hbm,
                out_hbm, g0_scr, g1_scr, y0_scr, y1_scr, wu_scr, wd_scr, h_scr, out_scr, sem, tile_sem):
    i = pl.program_id(0)
    tm = MOE_BLK
    nused = nu_ref[i]
    base = i * MOE_NBLK

    def weight_copies(e, buf):
        return (pltpu.make_async_copy(wu_hbm.at[e], wu_scr.at[buf], sem.at[0, buf]),
                pltpu.make_async_copy(wd_hbm.at[e], wd_scr.at[buf], sem.at[1, buf]))
    tile_rows = MOE_TILE * ROW_CHUNKS
    h_copy = pltpu.make_async_copy(h_hbm.at[pl.ds(i * tile_rows, tile_rows)], h_scr, tile_sem.at[0])
    out_copy = pltpu.make_async_copy(out_scr, out_hbm.at[i], tile_sem.at[1])

    def row_slice(ii):
        return slice(ii * ROW_CHUNKS, (ii + 1) * ROW_CHUNKS)

    def gather(blk, g_scr, lo, hi):
        codes = slot_ref.at[pl.ds(blk * tm, tm)]
        for ii in range(lo, hi):
            off = pl.multiple_of(codes[ii] & (tile_rows - 1), ROW_CHUNKS)
            g_scr[row_slice(ii), :] = h_scr[pl.ds(off, ROW_CHUNKS), :]

    def scatter(blk, y_scr, lo, hi):
        batch = 8
        codes = slot_ref.at[pl.ds(blk * tm, tm)]
        for i0 in range(lo, hi, batch):
            pending = []
            for ii in range(i0, i0 + batch):
                code = codes[ii]
                gate = gate_ref[code >> CODE_SHIFT]
                off = code & ((1 << CODE_SHIFT) - 1)
                rows = pl.ds(pl.multiple_of(off, ROW_CHUNKS), ROW_CHUNKS)
                pending.append((rows, out_scr[rows, :] + gate * y_scr[row_slice(ii), :]))
            for rows, val in pending:
                out_scr[rows, :] = val

    h_copy.start()
    for copy in weight_copies(be_ref[base], 0):
        copy.start()
    out_scr[...] = jnp.zeros(out_scr.shape, F32)
    y1_scr[...] = jnp.zeros(y1_scr.shape, F32)
    h_copy.wait()
    gather(0, g0_scr, 0, tm)

    def step(j, run, g_cur, g_nxt, y_cur, y_prv):
        e = be_ref[base + j]
        jp = jnp.maximum(j - 1, 0)
        jn = jnp.minimum(j + 1, nused - 1)
        first = jnp.logical_or(j == 0, be_ref[base + jp] != e)
        run = run + jnp.where(jnp.logical_and(first, j > 0), 1, 0)
        buf = run & 1

        @pl.when(first)
        def _():
            for copy in weight_copies(e, buf):
                copy.wait()
            nxt = nx_ref[base + j]

            @pl.when(nxt >= 0)
            def _():
                for copy in weight_copies(nxt, 1 - buf):
                    copy.start()

        def compute(wbuf):
            a = _load_token_major(g_cur, tm).astype(BF16)
            width = D_EXP // MOE_SPLITS
            ups = []
            for c in range(MOE_SPLITS):
                g_cols = slice(c * width, (c + 1) * width)
                l_cols = slice(D_EXP + c * width, D_EXP + (c + 1) * width)
                ups.append((jnp.dot(a, wu_scr[wbuf, :, g_cols], preferred_element_type=F32)
                            + bu_ref[e][:, g_cols],
                            jnp.dot(a, wu_scr[wbuf, :, l_cols], preferred_element_type=F32)
                            + bu_ref[e][:, l_cols]))
            scatter(jp, y_prv, 0, tm)
            gather(jn, g_nxt, 0, tm)
            y = bd_ref[e]
            for c, (glu, lin) in enumerate(ups):
                glu = jnp.minimum(glu, SWIGLU_LIMIT)
                lin = jnp.clip(lin, -SWIGLU_LIMIT, SWIGLU_LIMIT)
                act = glu * _sigmoid(SWIGLU_ALPHA * glu) * (lin + 1.0)
                y = y + jnp.dot(act.astype(BF16), wd_scr[wbuf, c * width:(c + 1) * width, :],
                                preferred_element_type=F32)
            _store_token_major(y_cur, y, tm)

        for wbuf in range(2):
            pl.when(buf == wbuf)(functools.partial(compute, wbuf))

        @pl.when(j == nused - 1)
        def _():
            scatter(j, y_cur, 0, tm)

        return run

    def pair(jj, run):
        run = step(2 * jj, run, g0_scr, g1_scr, y0_scr, y1_scr)
        return lax.cond(2 * jj + 1 < nused,
                        lambda r: step(2 * jj + 1, r, g1_scr, g0_scr, y1_scr, y0_scr),
                        lambda r: r, run)

    lax.fori_loop(0, (nused + 1) // 2, pair, jnp.int32(0))
    out_copy.start()
    out_copy.wait()


def _moe(be, nu, nx, slot, gates, h_tm, w_up, b_up, w_down, b_down):
    row_buf = pltpu.VMEM((MOE_BLK * ROW_CHUNKS, LANES), F32)
    grid_spec = pltpu.PrefetchScalarGridSpec(
        num_scalar_prefetch=3,
        grid=(N_MOE_TILES,),
        in_specs=[
            pl.BlockSpec((MOE_SLOTS,), lambda i, *_: (i,), memory_space=pltpu.SMEM),
            pl.BlockSpec((TOP_K * MOE_TILE,), lambda i, *_: (i,), memory_space=pltpu.SMEM),
            pl.BlockSpec((N_EXP, 1, 2 * D_EXP), lambda i, *_: (0, 0, 0)),
            pl.BlockSpec((N_EXP, 1, D), lambda i, *_: (0, 0, 0)),
            pl.BlockSpec(memory_space=pl.ANY),
            pl.BlockSpec(memory_space=pl.ANY),
            pl.BlockSpec(memory_space=pl.ANY),
        ],
        out_specs=pl.BlockSpec(memory_space=pl.ANY),
        scratch_shapes=[row_buf, row_buf, row_buf, row_buf,
                        pltpu.VMEM((2, D, 2 * D_EXP), BF16), pltpu.VMEM((2, D_EXP, D), BF16),
                        pltpu.VMEM((MOE_TILE * ROW_CHUNKS, LANES), F32),
                        pltpu.VMEM(((MOE_TILE + 1) * ROW_CHUNKS, LANES), F32),
                        pltpu.SemaphoreType.DMA((2, 2)), pltpu.SemaphoreType.DMA((2,))],
    )
    return pl.pallas_call(
        _moe_kernel,
        grid_spec=grid_spec,
        out_shape=jax.ShapeDtypeStruct((N_MOE_TILES, (MOE_TILE + 1) * ROW_CHUNKS, LANES), F32),
        compiler_params=_cparams(("arbitrary",)),
        name="moe",
    )(be, nu, nx, slot, gates, b_up.reshape(N_EXP, 1, 2 * D_EXP), b_down.reshape(N_EXP, 1, D),
      h_tm, w_up, w_down)


def _moe_layer(layer, ti, tg, h_tm, w_up, b_up, w_down_bf16, b_down):
    dest, meta = _route_a(ti)
    w_up_bf16, slot = _cast_route(w_up, layer, dest, meta)
    be, nx = (meta[:, r, :MOE_NBLK].reshape(-1) for r in (0, 3))
    nu = meta[:, 2, 0]
    gates = tg[:TOP_K].reshape(TOP_K, N_MOE_TILES, MOE_TILE).transpose(1, 0, 2).reshape(-1)
    return _moe(be, nu, nx, slot, gates, h_tm, w_up_bf16, b_up[layer], w_down_bf16, b_down[layer])


_MOE_OUT_SPEC = pl.BlockSpec((1, TM * ROW_CHUNKS, LANES),
                             lambda i: (i // (MOE_TILE // TM), i % (MOE_TILE // TM), 0))


def _conv_kernel(x_ref, moe_ref, modp_ref, g1_ref, w1_ref, b1_ref, wdw_ref, bdw_ref, lng_ref, lnb_ref,
                 w2_ref, b2_ref, mod_ref, g2_ref, wr_ref, br_ref, wd_ref,
                 x_out_ref, h_out_ref, ti_ref, tg_ref, wd_out_ref, pad_scr, conv_scr, shift_scr):
    wd_out_ref[...] = wd_ref[0].astype(BF16)
    i = pl.program_id(0)
    mp = modp_ref[0, 0]
    m = mod_ref[0, 0]
    x = x_ref[...] + mp[5:6] * _load_token_major(moe_ref, TM, (0,))
    h = (_rms(x) * (g1_ref[0] * (1.0 + m[1:2])) + m[0:1]).astype(BF16)
    ag = jnp.dot(h, w1_ref[0], preferred_element_type=F32) + b1_ref[0]
    u = ag[:, :D] * _sigmoid(ag[:, D:])
    is_prompt = i < T_PROMPT // TM

    def fill(seg):
        pitch = seg + CONV_PAD
        for s in range(TM // seg):
            pad_scr[s * pitch:s * pitch + CONV_PAD, :] = jnp.zeros((CONV_PAD, D), F32)
            pad_scr[s * pitch + CONV_PAD:(s + 1) * pitch, :] = u[s * seg:(s + 1) * seg, :]
        end = (TM // seg) * pitch
        pad_scr[end:end + CONV_PAD, :] = jnp.zeros((CONV_PAD, D), F32)

    @pl.when(is_prompt)
    def _():
        fill(L_PROMPT)

    @pl.when(jnp.logical_not(is_prompt))
    def _():
        fill(GRID_W)

    per_seq = L_PROMPT // CONV_CHUNK
    halo = CONV_CHUNK + 2 * CONV_PAD

    def conv_chunk(c, carry):
        base = jnp.where(is_prompt, (c // per_seq) * (L_PROMPT + CONV_PAD) + (c % per_seq) * CONV_CHUNK,
                         c * (GRID_W + CONV_PAD))
        base = pl.multiple_of(base, SUBLANES)
        out_rows = pl.ds(pl.multiple_of(c * CONV_CHUNK, CONV_CHUNK), CONV_CHUNK)
        for gl in range(D // LANES):
            lanes = slice(gl * LANES, (gl + 1) * LANES)
            blk = pad_scr[pl.ds(base, halo), lanes]
            span = halo - SUBLANES
            for r in range(SUBLANES):
                shift_scr[gl * SUBLANES + r] = blk[r:r + span, :]
            acc = jnp.zeros((CONV_CHUNK, LANES), F32)
            for tap in range(CONV_W):
                lo = CONV_PAD - CONV_W // 2 + tap
                al = lo // SUBLANES * SUBLANES
                acc = acc + (shift_scr[gl * SUBLANES + lo % SUBLANES, al:al + CONV_CHUNK, :]
                             * wdw_ref[0, tap:tap + 1, lanes])
            conv_scr[out_rows, lanes] = acc + bdw_ref[0, :, lanes]
        return carry

    lax.fori_loop(0, TM // CONV_CHUNK, conv_chunk, 0)
    uc = conv_scr[...]
    mu = jnp.mean(uc, axis=-1, keepdims=True)
    var = jnp.mean(jnp.square(uc - mu), axis=-1, keepdims=True)
    z = (uc - mu) * lax.rsqrt(var + EPS) * lng_ref[0] + lnb_ref[0]
    z = (z * _sigmoid(z)).astype(BF16)
    y = jnp.dot(z, w2_ref[0], preferred_element_type=F32) + b2_ref[0]
    _post_mixer(x, y, m, g2_ref, wr_ref[0], br_ref[0], x_out_ref, h_out_ref, ti_ref, tg_ref, TM)


def _conv_layer(x1, moe0, mod4, g_norm1, w_pw1, b_pw1, w_dw, b_dw, ln_g, ln_b, w_pw2, b_pw2,
                g_norm2, wr_t, br_c, moe_w_down):
    def full(shape):
        return pl.BlockSpec(shape, lambda i: (0,) * len(shape))

    pad_rows = (TM // GRID_W) * (GRID_W + CONV_PAD) + CONV_PAD
    return pl.pallas_call(
        _conv_kernel,
        grid=(T // TM,),
        in_specs=[pl.BlockSpec((TM, D), lambda i: (i, 0)),
                  _MOE_OUT_SPEC,
                  pl.BlockSpec((1, 1, 6, D), lambda i: (0, _cond_of_tile(i, TM), 0, 0)),
                  pl.BlockSpec((1, 1, D), lambda i: (1, 0, 0)),
                  full((1, D, 2 * D)), full((1, 1, 2 * D)), full((1, CONV_W, D)), full((1, 1, D)),
                  full((1, 1, D)), full((1, 1, D)), full((1, D, D)), full((1, 1, D))]
        + _post_in_specs(1) + [_wd_in_spec(1)],
        out_specs=_POST_OUT_SPECS + [_WD_OUT_SPEC],
        out_shape=_POST_OUT_SHAPES + [_WD_OUT_SHAPE],
        scratch_shapes=[pltpu.VMEM((pad_rows, D), F32), pltpu.VMEM((TM, D), F32),
                        pltpu.VMEM((D // LANES * SUBLANES, CONV_CHUNK + 2 * CONV_PAD - SUBLANES, LANES), F32)],
        compiler_params=_cparams(("arbitrary",)),
        name="conv",
    )(x1, moe0, mod4, g_norm1, w_pw1, b_pw1.reshape(1, 1, 2 * D), w_dw, b_dw.reshape(1, 1, D),
      ln_g.reshape(1, 1, D), ln_b.reshape(1, 1, D), w_pw2, b_pw2.reshape(1, 1, D),
      mod4, g_norm2, wr_t, br_c, moe_w_down)


def _final_kernel(x_ref, moe_ref, mod_ref, g_ref, yp_ref, ys_ref):
    i = pl.program_id(0)
    x = x_ref[...] + mod_ref[0, 0][5:6] * _load_token_major(moe_ref, TM, (0,))
    y = _rms(x) * g_ref[...]

    @pl.when(i < T_PROMPT // TM)
    def _():
        yp_ref[...] = y

    @pl.when(i >= T_PROMPT // TM)
    def _():
        ys_ref[...] = y


def _final(x, moe1, mod4, g_final):
    n_p = T_PROMPT // TM
    return pl.pallas_call(
        _final_kernel,
        grid=(T // TM,),
        in_specs=[pl.BlockSpec((TM, D), lambda i: (i, 0)),
                  _MOE_OUT_SPEC,
                  pl.BlockSpec((1, 1, 6, D), lambda i: (1, _cond_of_tile(i, TM), 0, 0)),
                  pl.BlockSpec((1, D), lambda i: (0, 0))],
        out_specs=[pl.BlockSpec((TM, D), lambda i: (jnp.minimum(i, n_p - 1), 0)),
                   pl.BlockSpec((TM, D), lambda i: (jnp.maximum(i - n_p, 0), 0))],
        out_shape=[jax.ShapeDtypeStruct((T_PROMPT, D), F32), jax.ShapeDtypeStruct((T_SAMPLE, D), F32)],
        compiler_params=_cparams(("arbitrary",)),
        name="final",
    )(x, moe1, mod4, g_final)


def kernel(x_prompt, x_sample, c, state_ret_fwd, state_ret_bwd, c_ctx, w_ada, b_ada, g_norm1, g_norm2,
           ret_w_in, ret_decay, ret_w_out, conv_w_pw1, conv_b_pw1, conv_w_dw, conv_b_dw, conv_ln_g,
           conv_ln_b, conv_w_pw2, conv_b_pw2, moe_w_router, moe_b_router, moe_w_up, moe_b_up,
           moe_w_down, moe_b_down, g_final):
    xp = x_prompt.reshape(T_PROMPT, D)
    xs = x_sample.reshape(T_SAMPLE, D)
    cond = jnp.concatenate([c_ctx[None, :], c, jnp.zeros((SUBLANES - 1 - N_SAMPLE, D), F32)], axis=0)
    mod = _ada(cond.T, w_ada, b_ada)
    mod4 = mod.reshape(mod.shape[0], SUBLANES, 6, D)
    wr_t = jnp.swapaxes(moe_w_router, 1, 2)
    br_c = moe_b_router[:, :, None]
    gn1 = g_norm1[:, None, :]
    gn2 = g_norm2[:, None, :]

    proj = _proj(xp, xs, g_norm1[0:1], mod4, ret_w_in[0].astype(BF16))
    og_p, new_f, new_b = _ret_prompt(ret_decay[0], proj)
    og_s = _ret_sample(ret_decay[0], proj, state_ret_fwd, state_ret_bwd)
    x1, h_tm, ti, tg, w_down_bf16 = _ret_out(og_p, og_s, ret_w_out[0].astype(BF16), xp, xs, mod4, gn2,
                                             wr_t, br_c, moe_w_down)
    moe0 = _moe_layer(0, ti, tg, h_tm, moe_w_up, moe_b_up, w_down_bf16, moe_b_down)

    x2, h_tm, ti, tg, w_down_bf16 = _conv_layer(
        x1, moe0, mod4, gn1, conv_w_pw1.astype(BF16), conv_b_pw1, conv_w_dw, conv_b_dw, conv_ln_g,
        conv_ln_b, conv_w_pw2.astype(BF16), conv_b_pw2, gn2, wr_t, br_c, moe_w_down)
    moe1 = _moe_layer(1, ti, tg, h_tm, moe_w_up, moe_b_up, w_down_bf16, moe_b_down)

    y_p, y_s = _final(x2, moe1, mod4, g_final[None, :])
    return (y_p.reshape(N_PROMPT, L_PROMPT, D), y_s.reshape(N_SAMPLE, L_SAMPLE, D), new_f, new_b)
```

```python
import functools

import numpy as np
import jax
import jax.numpy as jnp
from jax import lax
from jax.experimental import pallas as pl
from jax.experimental.pallas import tpu as pltpu

F32 = jnp.float32
BF16 = jnp.bfloat16

D = 1024
N_PROMPT, L_PROMPT = 32, 256
N_SAMPLE, L_SAMPLE = 2, 4096
T_PROMPT = N_PROMPT * L_PROMPT
T_SAMPLE = N_SAMPLE * L_SAMPLE
T = T_PROMPT + T_SAMPLE
GRID_W = 64
HEADS, DK, DV = 4, 256, 512
HK, HV = HEADS * DK, HEADS * DV
N_PROJ = 2 * HK + 2 * HV
ROPE_BASE = 10000.0
CONV_W = 31
CONV_PAD = 16
CONV_CHUNK = 64
N_EXP, TOP_K, D_EXP = 32, 4, 1024
SWIGLU_LIMIT, SWIGLU_ALPHA = 7.0, 1.702
EPS = 1e-6

LANES = 128
SUBLANES = 8
ROW_CHUNKS = D // LANES
VMEM_LIMIT = 56 * 1024 * 1024

TM = 512
MOE_TILE = 4096
MOE_BLK = 192
MOE_NBLK = 128
assert MOE_NBLK * MOE_BLK >= MOE_TILE * TOP_K + N_EXP * (MOE_BLK - 1) and (MOE_NBLK * MOE_BLK) % 1024 == 0
MOE_SLOTS = MOE_NBLK * MOE_BLK
MOE_SPLITS = 2
META_LANES = 2 * LANES
CODE_SHIFT = 16
N_MOE_TILES = T // MOE_TILE
RET_CHUNK = 256

NT_DIMS = (((1,), (1,)), ((), ()))
TN_DIMS = (((0,), (0,)), ((), ()))


def _cparams(sem):
    return pltpu.CompilerParams(dimension_semantics=sem, vmem_limit_bytes=VMEM_LIMIT)


def _cond_of_tile(i, tm):
    return jnp.maximum((i * tm) // L_SAMPLE - 1, 0)


def _sigmoid(x):
    return 0.5 * jnp.tanh(0.5 * x) + 0.5


def _rms(x):
    return x * lax.rsqrt(jnp.mean(x * x, axis=-1, keepdims=True) + EPS)


def _pick_x(i, tm, xp_ref, xs_ref):
    return jnp.where(i < T_PROMPT // tm, xp_ref[...], xs_ref[...])


def _x_specs(tm):
    n_p = T_PROMPT // tm
    return [pl.BlockSpec((tm, D), lambda i, *_: (jnp.minimum(i, n_p - 1), 0)),
            pl.BlockSpec((tm, D), lambda i, *_: (jnp.maximum(i - n_p, 0), 0))]


def _ada_kernel(ct_ref, w_ref, b_ref, o_ref):
    ct = ct_ref[...]
    s = ct * _sigmoid(ct)
    w = w_ref[0]
    rows = [jnp.sum(w * s[:, r:r + 1], axis=0, keepdims=True) for r in range(3)]
    rows.append(jnp.zeros((SUBLANES - 3, w.shape[1]), F32))
    o_ref[0] = jnp.concatenate(rows, axis=0) + b_ref[0]


def _ada(ct, w_ada, b_ada):
    depth, _, n = w_ada.shape
    tn = 1536
    return pl.pallas_call(
        _ada_kernel,
        grid=(depth, n // tn),
        in_specs=[pl.BlockSpec((D, SUBLANES), lambda l, j: (0, 0)),
                  pl.BlockSpec((1, D, tn), lambda l, j: (l, 0, j)),
                  pl.BlockSpec((1, 1, tn), lambda l, j: (l, 0, j))],
        out_specs=pl.BlockSpec((1, SUBLANES, tn), lambda l, j: (l, 0, j)),
        out_shape=jax.ShapeDtypeStruct((depth, SUBLANES, n), F32),
        compiler_params=_cparams(("arbitrary", "arbitrary")),
        name="ada",
    )(ct, w_ada, b_ada.reshape(depth, 1, n))


def _rope_tables():
    t = np.arange(L_SAMPLE)
    row = (t // GRID_W).astype(np.float32)
    col = (t % GRID_W).astype(np.float32)
    nf = DK // 4
    inv = (np.float32(ROPE_BASE) ** (-np.arange(nf, dtype=np.float32) / np.float32(nf))).astype(np.float32)
    cos, sin = [], []
    for pos in (row, col):
        ang = (pos[:, None] * inv[None, :]).astype(np.float32)
        c, s = np.cos(ang).astype(np.float32), np.sin(ang).astype(np.float32)
        cos += [c, c]
        sin += [-s, s]
    return np.concatenate(cos, axis=1), np.concatenate(sin, axis=1)


def _proj_kernel(xp_ref, xs_ref, g_ref, mod_ref, w_ref, cos_ref, sin_ref, o_ref):
    i = pl.program_id(0)
    m = mod_ref[0, 0]
    h = _rms(_pick_x(i, TM, xp_ref, xs_ref)) * (g_ref[...] * (1.0 + m[1:2]))
    h = (h + m[0:1]).astype(BF16)
    is_sample = i >= T_PROMPT // TM
    cos = jnp.where(is_sample, cos_ref[...], 1.0)
    sin = jnp.where(is_sample, sin_ref[...], 0.0)
    for col in range(N_PROJ // D):
        acc = jnp.dot(h, w_ref[:, col * D:(col + 1) * D], preferred_element_type=F32)
        if col == 1:
            acc = acc * DK ** -0.5
        if col < 2:
            for c in range(D // LANES):
                a = acc[:, c * LANES:(c + 1) * LANES]
                p = (c % 2) * LANES
                r = a * cos[:, p:p + LANES] + pltpu.roll(a, LANES // 2, 1) * sin[:, p:p + LANES]
                o_ref[:, col * D + c * LANES:col * D + (c + 1) * LANES] = r.astype(BF16)
        else:
            o_ref[:, col * D:(col + 1) * D] = acc.astype(BF16)


def _proj(xp, xs, g1, mod4, w_in):
    cos, sin = _rope_tables()
    n_p = T_PROMPT // TM
    n_s = L_SAMPLE // TM
    tab_spec = pl.BlockSpec((TM, DK), lambda i: (jnp.maximum(i - n_p, 0) % n_s, 0))
    return pl.pallas_call(
        _proj_kernel,
        grid=(T // TM,),
        in_specs=_x_specs(TM) + [
            pl.BlockSpec((1, D), lambda i: (0, 0)),
            pl.BlockSpec((1, 1, 6, D), lambda i: (0, _cond_of_tile(i, TM), 0, 0)),
            pl.BlockSpec((D, N_PROJ), lambda i: (0, 0)),
            tab_spec, tab_spec],
        out_specs=pl.BlockSpec((TM, N_PROJ), lambda i: (i, 0)),
        out_shape=jax.ShapeDtypeStruct((T, N_PROJ), BF16),
        compiler_params=_cparams(("arbitrary",)),
        name="proj",
    )(xp, xs, g1, mod4, w_in, jnp.asarray(cos), jnp.asarray(sin))


def _log_decays(dec_ref, head):
    out = []
    for direction in range(2):
        d = jnp.full((1, 1), dec_ref[direction, head], F32)
        out.append(jnp.minimum(d, 0.0) - jnp.log(1.0 + jnp.exp(-jnp.abs(d))))
    return out


def _decay_mask(lgf, lgb, c):
    ii = lax.broadcasted_iota(jnp.int32, (c, c), 0)
    jj = lax.broadcasted_iota(jnp.int32, (c, c), 1)
    diff = (ii - jj).astype(F32)
    fwd = jnp.where(diff >= 0, jnp.exp(lgf * jnp.maximum(diff, 0.0)), 0.0)
    bwd = jnp.where(diff <= 0, jnp.exp(lgb * jnp.maximum(-diff, 0.0)), 0.0)
    return fwd + bwd


def _norm_gate(o, g):
    g = g.astype(F32)
    return (_rms(o) * (g * _sigmoid(g))).astype(BF16)


def _ret_prompt_kernel(dec_ref, p_ref, o_ref, sf_ref, sb_ref):
    c = L_PROMPT
    pos = lax.broadcasted_iota(jnp.int32, (c, 1), 0).astype(F32)
    for head in range(HEADS):
        lgf, lgb = _log_decays(dec_ref, head)
        q = p_ref[:, head * DK:(head + 1) * DK]
        k = p_ref[:, HK + head * DK:HK + (head + 1) * DK]
        v = p_ref[:, 2 * HK + head * DV:2 * HK + (head + 1) * DV]
        g = p_ref[:, 2 * HK + HV + head * DV:2 * HK + HV + (head + 1) * DV]
        s = lax.dot_general(q, k, NT_DIMS, preferred_element_type=F32) * _decay_mask(lgf, lgb, c)
        o = jnp.dot(s.astype(BF16), v, preferred_element_type=F32)
        o_ref[:, head * DV:(head + 1) * DV] = _norm_gate(o, g)
        kf = k.astype(F32)
        k_fwd = (kf * jnp.exp(lgf * (c - 1.0 - pos))).astype(BF16)
        k_bwd = (kf * jnp.exp(lgb * pos)).astype(BF16)
        sf_ref[0, 0, head] = lax.dot_general(k_fwd, v, TN_DIMS, preferred_element_type=F32)
        sb_ref[0, 0, head] = lax.dot_general(k_bwd, v, TN_DIMS, preferred_element_type=F32)


def _ret_specs(seq_len, row0):
    r = row0 // seq_len
    return [pl.BlockSpec((seq_len, DK), lambda b, h: (r + b, h)),
            pl.BlockSpec((seq_len, DK), lambda b, h: (r + b, HK // DK + h)),
            pl.BlockSpec((seq_len, DV), lambda b, h: (r + b, 2 * HK // DV + h)),
            pl.BlockSpec((seq_len, DV), lambda b, h: (r + b, (2 * HK + HV) // DV + h))]


def _ret_prompt(decay, proj):
    state = jax.ShapeDtypeStruct((N_PROMPT, 1, HEADS, DK, DV), F32)
    state_spec = pl.BlockSpec((1, 1, HEADS, DK, DV), lambda b: (b, 0, 0, 0, 0))
    return pl.pallas_call(
        _ret_prompt_kernel,
        grid=(N_PROMPT,),
        in_specs=[pl.BlockSpec(memory_space=pltpu.SMEM),
                  pl.BlockSpec((L_PROMPT, N_PROJ), lambda b: (b, 0))],
        out_specs=[pl.BlockSpec((L_PROMPT, HV), lambda b: (b, 0)), state_spec, state_spec],
        out_shape=[jax.ShapeDtypeStruct((T_PROMPT, HV), BF16), state, state],
        compiler_params=_cparams(("arbitrary",)),
        name="ret_prompt",
    )(decay, proj)


def _ret_sample_kernel(dec_ref, q_ref, k_ref, v_ref, g_ref, s0f_ref, s0b_ref, o_ref,
                       of_scr, ob_scr, sf_scr, sb_scr, dm_scr):
    c = RET_CHUNK
    nc = L_SAMPLE // c
    lgf, lgb = _log_decays(dec_ref, pl.program_id(1))
    dm_scr[...] = _decay_mask(lgf, lgb, c)
    pos = lax.broadcasted_iota(jnp.int32, (c, 1), 0).astype(F32)

    def chunk(ci):
        rows = pl.ds(pl.multiple_of(ci * c, c), c)
        return rows, q_ref[rows, :], k_ref[rows, :], v_ref[rows, :]

    def state_update(s_scr, lg, write_pos, kc, vc):
        kw = (kc.astype(F32) * jnp.exp(lg * write_pos)).astype(BF16)
        s_scr[...] = s_scr[...] * jnp.exp(lg * c) + lax.dot_general(kw, vc, TN_DIMS, preferred_element_type=F32)

    def read_state(s_scr, lg, read_pos, qc):
        qr = (qc.astype(F32) * jnp.exp(lg * read_pos)).astype(BF16)
        return jnp.dot(qr, s_scr[...].astype(BF16), preferred_element_type=F32)

    sf_scr[...] = s0f_ref[0, 0, 0]
    sb_scr[...] = s0b_ref[0, 0, 0]

    def scan(step, carry):
        rows, qc, kc, vc = chunk(nc - 1 - step)
        ob_scr[rows, :] = read_state(sb_scr, lgb, c - pos, qc)
        state_update(sb_scr, lgb, pos, kc, vc)
        rows, qc, kc, vc = chunk(step)
        s = lax.dot_general(qc, kc, NT_DIMS, preferred_element_type=F32) * dm_scr[...]
        o = jnp.dot(s.astype(BF16), vc, preferred_element_type=F32)
        of_scr[rows, :] = o + read_state(sf_scr, lgf, pos + 1.0, qc)
        state_update(sf_scr, lgf, c - 1.0 - pos, kc, vc)
        return carry

    lax.fori_loop(0, nc, scan, 0)

    def finish(ci, carry):
        rows = pl.ds(pl.multiple_of(ci * c, c), c)
        o_ref[rows, :] = _norm_gate(of_scr[rows, :] + ob_scr[rows, :], g_ref[rows, :])
        return carry

    lax.fori_loop(0, nc, finish, 0)


def _ret_sample(decay, proj, s0f, s0b):
    state_spec = pl.BlockSpec((1, 1, 1, DK, DV), lambda b, h: (b, 0, h, 0, 0))
    return pl.pallas_call(
        _ret_sample_kernel,
        grid=(N_SAMPLE, HEADS),
        in_specs=[pl.BlockSpec(memory_space=pltpu.SMEM)] + _ret_specs(L_SAMPLE, T_PROMPT)
        + [state_spec, state_spec],
        out_specs=pl.BlockSpec((L_SAMPLE, DV), lambda b, h: (b, h)),
        out_shape=jax.ShapeDtypeStruct((T_SAMPLE, HV), BF16),
        scratch_shapes=[pltpu.VMEM((L_SAMPLE, DV), F32), pltpu.VMEM((L_SAMPLE, DV), F32),
                        pltpu.VMEM((DK, DV), F32), pltpu.VMEM((DK, DV), F32),
                        pltpu.VMEM((RET_CHUNK, RET_CHUNK), F32)],
        compiler_params=_cparams(("arbitrary", "arbitrary")),
        name="ret_sample",
    )(decay, proj, proj, proj, proj, s0f, s0b)


def _store_token_major(ref, val, tm):
    for s in range(ROW_CHUNKS):
        ref[pl.ds(s, tm, stride=ROW_CHUNKS), :] = val[:, s * LANES:(s + 1) * LANES]


def _load_token_major(ref, tm, lead=()):
    return jnp.concatenate([ref[lead + (pl.ds(s, tm, stride=ROW_CHUNKS), slice(None))]
                            for s in range(ROW_CHUNKS)], axis=1)


def _post_mixer(x, y, m, g2_ref, wr_ref, br_ref, x_out_ref, h_out_ref, ti_ref, tg_ref, tm):
    x1 = x + m[2:3] * y
    h = _rms(x1) * (g2_ref[0] * (1.0 + m[4:5])) + m[3:4]
    x_out_ref[...] = x1
    _store_token_major(h_out_ref, h, tm)
    w = wr_ref[...]
    w_hi = w.astype(BF16)
    w_lo = (w - w_hi.astype(F32)).astype(BF16)
    h_hi = h.astype(BF16)
    h_lo = (h - h_hi.astype(F32)).astype(BF16)
    dot = functools.partial(lax.dot_general, dimension_numbers=NT_DIMS, preferred_element_type=F32)
    cur = dot(w_hi, h_hi) + dot(w_hi, h_lo) + dot(w_lo, h_hi) + br_ref[...]
    ie = lax.broadcasted_iota(jnp.int32, (N_EXP, tm), 0).astype(F32)
    vals, idxs = [], []
    for _ in range(TOP_K):
        top = jnp.max(cur, axis=0, keepdims=True)
        idx = jnp.min(jnp.where(cur == top, ie, float(N_EXP)), axis=0, keepdims=True)
        vals.append(top)
        idxs.append(idx)
        cur = jnp.where(ie == idx, -jnp.inf, cur)
    ex = [jnp.exp(v - vals[0]) for v in vals]
    den = ex[0] + ex[1] + ex[2] + ex[3]
    pad = jnp.zeros((SUBLANES - TOP_K, tm), F32)
    ti_ref[...] = jnp.concatenate(idxs + [pad], axis=0).astype(jnp.int32)
    tg_ref[...] = jnp.concatenate([e / den for e in ex] + [pad], axis=0)


def _post_in_specs(layer):
    return [pl.BlockSpec((1, 1, 6, D), lambda i: (layer, _cond_of_tile(i, TM), 0, 0)),
            pl.BlockSpec((1, 1, D), lambda i: (layer, 0, 0)),
            pl.BlockSpec((1, N_EXP, D), lambda i: (layer, 0, 0)),
            pl.BlockSpec((1, N_EXP, 1), lambda i: (layer, 0, 0))]


_POST_OUT_SPECS = [pl.BlockSpec((TM, D), lambda i: (i, 0)),
                   pl.BlockSpec((TM * ROW_CHUNKS, LANES), lambda i: (i, 0)),
                   pl.BlockSpec((SUBLANES, TM), lambda i: (0, i)),
                   pl.BlockSpec((SUBLANES, TM), lambda i: (0, i))]
_POST_OUT_SHAPES = [jax.ShapeDtypeStruct((T, D), F32),
                    jax.ShapeDtypeStruct((T * ROW_CHUNKS, LANES), F32),
                    jax.ShapeDtypeStruct((SUBLANES, T), jnp.int32),
                    jax.ShapeDtypeStruct((SUBLANES, T), F32)]


def _ret_out_kernel(ogp_ref, ogs_ref, w_ref, xp_ref, xs_ref, mod_ref, g2_ref, wr_ref, br_ref, wd_ref,
                    x_out_ref, h_out_ref, ti_ref, tg_ref, wd_out_ref):
    wd_out_ref[...] = wd_ref[0].astype(BF16)
    i = pl.program_id(0)
    og = jnp.where(i < T_PROMPT // TM, ogp_ref[...], ogs_ref[...])
    y = jnp.dot(og, w_ref[...], preferred_element_type=F32)
    _post_mixer(_pick_x(i, TM, xp_ref, xs_ref), y, mod_ref[0, 0], g2_ref, wr_ref[0], br_ref[0],
                x_out_ref, h_out_ref, ti_ref, tg_ref, TM)


assert T // TM == N_EXP
_WD_OUT_SPEC = pl.BlockSpec((1, D_EXP, D), lambda i: (i, 0, 0))
_WD_OUT_SHAPE = jax.ShapeDtypeStruct((N_EXP, D_EXP, D), BF16)


def _wd_in_spec(layer):
    return pl.BlockSpec((1, 1, D_EXP, D), lambda i: (layer, i, 0, 0))


def _ret_out(og_p, og_s, w_out, xp, xs, mod4, g_norm2, wr_t, br_c, moe_w_down):
    n_p = T_PROMPT // TM
    return pl.pallas_call(
        _ret_out_kernel,
        grid=(T // TM,),
        in_specs=[pl.BlockSpec((TM, HV), lambda i: (jnp.minimum(i, n_p - 1), 0)),
                  pl.BlockSpec((TM, HV), lambda i: (jnp.maximum(i - n_p, 0), 0)),
                  pl.BlockSpec((HV, D), lambda i: (0, 0))]
        + _x_specs(TM) + _post_in_specs(0) + [_wd_in_spec(0)],
        out_specs=_POST_OUT_SPECS + [_WD_OUT_SPEC],
        out_shape=_POST_OUT_SHAPES + [_WD_OUT_SHAPE],
        compiler_params=_cparams(("arbitrary",)),
        name="ret_out",
    )(og_p, og_s, w_out, xp, xs, mod4, g_norm2, wr_t, br_c, moe_w_down)


def _route_a_kernel(ti_ref, dest_ref, meta_ref):
    tt, tm = MOE_TILE, MOE_BLK
    ti = ti_ref[...]
    ie = lax.broadcasted_iota(jnp.int32, (N_EXP, tt), 0)
    onehots = [(ie == ti[k:k + 1]).astype(F32) for k in range(TOP_K)]
    oh = onehots[0] + onehots[1] + onehots[2] + onehots[3]
    ch = 512
    upper = (lax.broadcasted_iota(jnp.int32, (ch, ch), 0)
             < lax.broadcasted_iota(jnp.int32, (ch, ch), 1)).astype(BF16)
    carry = jnp.zeros((N_EXP, 1), F32)
    cums = []
    for c in range(tt // ch):
        blk = oh[:, c * ch:(c + 1) * ch]
        cums.append(jnp.dot(blk.astype(BF16), upper, preferred_element_type=F32) + carry)
        carry = carry + jnp.sum(blk, axis=1, keepdims=True)
    cum = jnp.concatenate(cums, axis=1)
    cnt = carry
    nb = jnp.floor((cnt + (tm - 0.5)) * (1.0 / tm))
    lower = (lax.broadcasted_iota(jnp.int32, (N_EXP, N_EXP), 1)
             < lax.broadcasted_iota(jnp.int32, (N_EXP, N_EXP), 0)).astype(BF16)
    offb = jnp.dot(lower, jnp.broadcast_to(nb, (N_EXP, LANES)).astype(BF16),
                   preferred_element_type=F32)[:, :1]
    off = offb * tm
    base = off + cum
    dests = [jnp.sum(onehots[k] * base, axis=0, keepdims=True) for k in range(TOP_K)]
    dests.append(jnp.zeros((SUBLANES - TOP_K, tt), F32))
    dest_ref[0] = jnp.concatenate(dests, axis=0).astype(jnp.int32)
    nused = jnp.sum(nb, axis=0, keepdims=True)
    jl = lax.broadcasted_iota(jnp.int32, (N_EXP, META_LANES), 1).astype(F32)
    jc = jnp.minimum(jl, nused - 1.0)
    be = jnp.minimum(jnp.sum(((offb + nb) <= jc).astype(F32), axis=0, keepdims=True), N_EXP - 1.0)
    ief = lax.broadcasted_iota(jnp.int32, (N_EXP, META_LANES), 0).astype(F32)
    end_row = jnp.sum(jnp.where(ief == be, off + cnt, 0.0), axis=0, keepdims=True)
    nvalid = jnp.clip(end_row - jl[:1] * tm, 0.0, float(tm))
    nvalid = jnp.where(jl[:1] < nused, nvalid, 0.0)
    run_end = jnp.sum(jnp.where(ief == be, offb + nb, 0.0), axis=0, keepdims=True)
    nxt = jnp.sum(((offb + nb) <= run_end).astype(F32), axis=0, keepdims=True)
    nxt = jnp.where(run_end < nused, nxt, -1.0)
    meta = jnp.concatenate([be, nvalid, jnp.broadcast_to(nused, (1, META_LANES)), nxt,
                            jnp.zeros((SUBLANES - 4, META_LANES), F32)], axis=0)
    meta_ref[0] = meta.astype(jnp.int32)


def _route_a(ti):
    return pl.pallas_call(
        _route_a_kernel,
        grid=(N_MOE_TILES,),
        in_specs=[pl.BlockSpec((SUBLANES, MOE_TILE), lambda i: (0, i))],
        out_specs=[pl.BlockSpec((1, SUBLANES, MOE_TILE), lambda i: (i, 0, 0)),
                   pl.BlockSpec((1, SUBLANES, META_LANES), lambda i: (i, 0, 0))],
        out_shape=[jax.ShapeDtypeStruct((N_MOE_TILES, SUBLANES, MOE_TILE), jnp.int32),
                   jax.ShapeDtypeStruct((N_MOE_TILES, SUBLANES, META_LANES), jnp.int32)],
        compiler_params=_cparams(("arbitrary",)),
        name="route_a",
    )(ti)


ROUTE_STEPS = N_EXP // N_MOE_TILES


def _cast_route_kernel(wu_ref, dest_ref, meta_ref, ou_ref, slot_ref):
    ou_ref[...] = wu_ref[0].astype(BF16)
    part = pl.program_id(0) % ROUTE_STEPS
    group = SUBLANES

    def per_block(j, carry):
        def pad(s, c2):
            slot_ref[j * MOE_BLK + s] = MOE_TILE * ROW_CHUNKS
            return c2

        def pad_group(gi, c2):
            slots = slot_ref.at[pl.ds(j * MOE_BLK + gi * group, group)]
            for u in range(group):
                slots[u] = MOE_TILE * ROW_CHUNKS
            return c2

        n_real = meta_ref[0, 1, j]
        first_group = (n_real + group - 1) // group
        lax.fori_loop(n_real, first_group * group, pad, 0)
        lax.fori_loop(first_group, MOE_BLK // group, pad_group, 0)
        return carry

    blocks = MOE_NBLK // ROUTE_STEPS
    lax.fori_loop(part * blocks, (part + 1) * blocks, per_block, 0)

    tokens = MOE_TILE // ROUTE_STEPS

    def per_group(tg, carry):
        t0 = part * tokens + tg * group
        rows = [dest_ref.at[0, k, pl.ds(t0, group)] for k in range(TOP_K)]
        tok_code = t0 * ((1 << CODE_SHIFT) + ROW_CHUNKS)
        for u0 in range(0, group, 2):
            loaded = [(k, u, rows[k][u]) for u in range(u0, u0 + 2) for k in range(TOP_K)]
            for k, u, d in loaded:
                slot_ref[d] = tok_code + ((k * MOE_TILE + u) << CODE_SHIFT) + u * ROW_CHUNKS
        return carry

    lax.fori_loop(0, tokens // group, per_group, 0)


def _cast_route(w_up, layer, dest, meta):
    tile = lambda e: e // ROUTE_STEPS
    return pl.pallas_call(
        _cast_route_kernel,
        grid=(N_EXP,),
        in_specs=[pl.BlockSpec((1, 1) + w_up.shape[2:], lambda e: (layer, e, 0, 0)),
                  pl.BlockSpec((1, SUBLANES, MOE_TILE), lambda e: (tile(e), 0, 0), memory_space=pltpu.SMEM),
                  pl.BlockSpec((1, SUBLANES, META_LANES), lambda e: (tile(e), 0, 0),
                               memory_space=pltpu.SMEM)],
        out_specs=[pl.BlockSpec((1,) + w_up.shape[2:], lambda e: (e, 0, 0)),
                   pl.BlockSpec((MOE_SLOTS,), lambda e: (tile(e),), memory_space=pltpu.SMEM)],
        out_shape=[jax.ShapeDtypeStruct(w_up.shape[1:], BF16),
                   jax.ShapeDtypeStruct((N_MOE_TILES * MOE_SLOTS,), jnp.int32)],
        compiler_params=_cparams(("arbitrary",)),
        name="cast_route",
    )(w_up, dest, meta)


def _moe_kernel(be_ref, nu_ref, nx_ref, slot_ref, gate_ref, bu_ref, bd_ref, h_hbm, wu_hbm, wd_hbm,
                out_hbm, g0_scr, g1_scr, y0_scr, y1_scr, wu_scr, wd_scr, h_scr, out_scr, sem, tile_sem):
    i = pl.program_id(0)
    tm = MOE_BLK
    nused = nu_ref[i]
    base = i * MOE_NBLK

    def weight_copies(e, buf):
        return (pltpu.make_async_copy(wu_hbm.at[e], wu_scr.at[buf], sem.at[0, buf]),
                pltpu.make_async_copy(wd_hbm.at[e], wd_scr.at[buf], sem.at[1, buf]))
    tile_rows = MOE_TILE * ROW_CHUNKS
    h_copy = pltpu.make_async_copy(h_hbm.at[pl.ds(i * tile_rows, tile_rows)], h_scr, tile_sem.at[0])
    out_copy = pltpu.make_async_copy(out_scr, out_hbm.at[i], tile_sem.at[1])

    def row_slice(ii):
        return slice(ii * ROW_CHUNKS, (ii + 1) * ROW_CHUNKS)

    def gather(blk, g_scr, lo, hi):
        for ii in range(lo, hi):
            off = pl.multiple_of(slot_ref[blk * tm + ii] & (tile_rows - 1), ROW_CHUNKS)
            g_scr[row_slice(ii), :] = h_scr[pl.ds(off, ROW_CHUNKS), :]

    def scatter(blk, y_scr, lo, hi):
        batch = 8
        for i0 in range(lo, hi, batch):
            pending = []
            for ii in range(i0, i0 + batch):
                code = slot_ref[blk * tm + ii]
                gate = gate_ref[code >> CODE_SHIFT]
                off = code & ((1 << CODE_SHIFT) - 1)
                rows = pl.ds(pl.multiple_of(off, ROW_CHUNKS), ROW_CHUNKS)
                pending.append((rows, out_scr[rows, :] + gate * y_scr[row_slice(ii), :]))
            for rows, val in pending:
                out_scr[rows, :] = val

    h_copy.start()
    for copy in weight_copies(be_ref[base], 0):
        copy.start()
    out_scr[...] = jnp.zeros(out_scr.shape, F32)
    y1_scr[...] = jnp.zeros(y1_scr.shape, F32)
    h_copy.wait()
    gather(0, g0_scr, 0, tm)

    def step(j, run, g_cur, g_nxt, y_cur, y_prv):
        e = be_ref[base + j]
        jp = jnp.maximum(j - 1, 0)
        jn = jnp.minimum(j + 1, nused - 1)
        first = jnp.logical_or(j == 0, be_ref[base + jp] != e)
        run = run + jnp.where(jnp.logical_and(first, j > 0), 1, 0)
        buf = run & 1

        @pl.when(first)
        def _():
            for copy in weight_copies(e, buf):
                copy.wait()
            nxt = nx_ref[base + j]

            @pl.when(nxt >= 0)
            def _():
                for copy in weight_copies(nxt, 1 - buf):
                    copy.start()

        def compute(wbuf):
            a = _load_token_major(g_cur, tm).astype(BF16)
            width = D_EXP // MOE_SPLITS
            ups = []
            for c in range(MOE_SPLITS):
                g_cols = slice(c * width, (c + 1) * width)
                l_cols = slice(D_EXP + c * width, D_EXP + (c + 1) * width)
                ups.append((jnp.dot(a, wu_scr[wbuf, :, g_cols], preferred_element_type=F32)
                            + bu_ref[e][:, g_cols],
                            jnp.dot(a, wu_scr[wbuf, :, l_cols], preferred_element_type=F32)
                            + bu_ref[e][:, l_cols]))
            scatter(jp, y_prv, 0, tm)
            gather(jn, g_nxt, 0, tm)
            y = bd_ref[e]
            for c, (glu, lin) in enumerate(ups):
                glu = jnp.minimum(glu, SWIGLU_LIMIT)
                lin = jnp.clip(lin, -SWIGLU_LIMIT, SWIGLU_LIMIT)
                act = glu * _sigmoid(SWIGLU_ALPHA * glu) * (lin + 1.0)
                y = y + jnp.dot(act.astype(BF16), wd_scr[wbuf, c * width:(c + 1) * width, :],
                                preferred_element_type=F32)
            _store_token_major(y_cur, y, tm)

        for wbuf in range(2):
            pl.when(buf == wbuf)(functools.partial(compute, wbuf))

        @pl.when(j == nused - 1)
        def _():
            scatter(j, y_cur, 0, tm)

        return run

    def pair(jj, run):
        run = step(2 * jj, run, g0_scr, g1_scr, y0_scr, y1_scr)
        return lax.cond(2 * jj + 1 < nused,
                        lambda r: step(2 * jj + 1, r, g1_scr, g0_scr, y1_scr, y0_scr),
                        lambda r: r, run)

    lax.fori_loop(0, (nused + 1) // 2, pair, jnp.int32(0))
    out_copy.start()
    out_copy.wait()


def _moe(be, nu, nx, slot, gates, h_tm, w_up, b_up, w_down, b_down):
    row_buf = pltpu.VMEM((MOE_BLK * ROW_CHUNKS, LANES), F32)
    grid_spec = pltpu.PrefetchScalarGridSpec(
        num_scalar_prefetch=3,
        grid=(N_MOE_TILES,),
        in_specs=[
            pl.BlockSpec((MOE_SLOTS,), lambda i, *_: (i,), memory_space=pltpu.SMEM),
            pl.BlockSpec((TOP_K * MOE_TILE,), lambda i, *_: (i,), memory_space=pltpu.SMEM),
            pl.BlockSpec((N_EXP, 1, 2 * D_EXP), lambda i, *_: (0, 0, 0)),
            pl.BlockSpec((N_EXP, 1, D), lambda i, *_: (0, 0, 0)),
            pl.BlockSpec(memory_space=pl.ANY),
            pl.BlockSpec(memory_space=pl.ANY),
            pl.BlockSpec(memory_space=pl.ANY),
        ],
        out_specs=pl.BlockSpec(memory_space=pl.ANY),
        scratch_shapes=[row_buf, row_buf, row_buf, row_buf,
                        pltpu.VMEM((2, D, 2 * D_EXP), BF16), pltpu.VMEM((2, D_EXP, D), BF16),
                        pltpu.VMEM((MOE_TILE * ROW_CHUNKS, LANES), F32),
                        pltpu.VMEM(((MOE_TILE + 1) * ROW_CHUNKS, LANES), F32),
                        pltpu.SemaphoreType.DMA((2, 2)), pltpu.SemaphoreType.DMA((2,))],
    )
    return pl.pallas_call(
        _moe_kernel,
        grid_spec=grid_spec,
        out_shape=jax.ShapeDtypeStruct((N_MOE_TILES, (MOE_TILE + 1) * ROW_CHUNKS, LANES), F32),
        compiler_params=_cparams(("arbitrary",)),
        name="moe",
    )(be, nu, nx, slot, gates, b_up.reshape(N_EXP, 1, 2 * D_EXP), b_down.reshape(N_EXP, 1, D),
      h_tm, w_up, w_down)


def _moe_layer(layer, ti, tg, h_tm, w_up, b_up, w_down_bf16, b_down):
    dest, meta = _route_a(ti)
    w_up_bf16, slot = _cast_route(w_up, layer, dest, meta)
    be, nx = (meta[:, r, :MOE_NBLK].reshape(-1) for r in (0, 3))
    nu = meta[:, 2, 0]
    gates = tg[:TOP_K].reshape(TOP_K, N_MOE_TILES, MOE_TILE).transpose(1, 0, 2).reshape(-1)
    return _moe(be, nu, nx, slot, gates, h_tm, w_up_bf16, b_up[layer], w_down_bf16, b_down[layer])


_MOE_OUT_SPEC = pl.BlockSpec((1, TM * ROW_CHUNKS, LANES),
                             lambda i: (i // (MOE_TILE // TM), i % (MOE_TILE // TM), 0))


def _conv_kernel(x_ref, moe_ref, modp_ref, g1_ref, w1_ref, b1_ref, wdw_ref, bdw_ref, lng_ref, lnb_ref,
                 w2_ref, b2_ref, mod_ref, g2_ref, wr_ref, br_ref, wd_ref,
                 x_out_ref, h_out_ref, ti_ref, tg_ref, wd_out_ref, pad_scr, conv_scr, shift_scr):
    wd_out_ref[...] = wd_ref[0].astype(BF16)
    i = pl.program_id(0)
    mp = modp_ref[0, 0]
    m = mod_ref[0, 0]
    x = x_ref[...] + mp[5:6] * _load_token_major(moe_ref, TM, (0,))
    h = (_rms(x) * (g1_ref[0] * (1.0 + m[1:2])) + m[0:1]).astype(BF16)
    ag = jnp.dot(h, w1_ref[0], preferred_element_type=F32) + b1_ref[0]
    u = ag[:, :D] * _sigmoid(ag[:, D:])
    is_prompt = i < T_PROMPT // TM

    def fill(seg):
        pitch = seg + CONV_PAD
        for s in range(TM // seg):
            pad_scr[s * pitch:s * pitch + CONV_PAD, :] = jnp.zeros((CONV_PAD, D), F32)
            pad_scr[s * pitch + CONV_PAD:(s + 1) * pitch, :] = u[s * seg:(s + 1) * seg, :]
        end = (TM // seg) * pitch
        pad_scr[end:end + CONV_PAD, :] = jnp.zeros((CONV_PAD, D), F32)

    @pl.when(is_prompt)
    def _():
        fill(L_PROMPT)

    @pl.when(jnp.logical_not(is_prompt))
    def _():
        fill(GRID_W)

    per_seq = L_PROMPT // CONV_CHUNK
    halo = CONV_CHUNK + 2 * CONV_PAD

    def conv_chunk(c, carry):
        base = jnp.where(is_prompt, (c // per_seq) * (L_PROMPT + CONV_PAD) + (c % per_seq) * CONV_CHUNK,
                         c * (GRID_W + CONV_PAD))
        base = pl.multiple_of(base, SUBLANES)
        out_rows = pl.ds(pl.multiple_of(c * CONV_CHUNK, CONV_CHUNK), CONV_CHUNK)
        for gl in range(D // LANES):
            lanes = slice(gl * LANES, (gl + 1) * LANES)
            blk = pad_scr[pl.ds(base, halo), lanes]
            span = halo - SUBLANES
            for r in range(SUBLANES):
                shift_scr[gl * SUBLANES + r] = blk[r:r + span, :]
            acc = jnp.zeros((CONV_CHUNK, LANES), F32)
            for tap in range(CONV_W):
                lo = CONV_PAD - CONV_W // 2 + tap
                al = lo // SUBLANES * SUBLANES
                acc = acc + (shift_scr[gl * SUBLANES + lo % SUBLANES, al:al + CONV_CHUNK, :]
                             * wdw_ref[0, tap:tap + 1, lanes])
            conv_scr[out_rows, lanes] = acc + bdw_ref[0, :, lanes]
        return carry

    lax.fori_loop(0, TM // CONV_CHUNK, conv_chunk, 0)
    uc = conv_scr[...]
    mu = jnp.mean(uc, axis=-1, keepdims=True)
    var = jnp.mean(jnp.square(uc - mu), axis=-1, keepdims=True)
    z = (uc - mu) * lax.rsqrt(var + EPS) * lng_ref[0] + lnb_ref[0]
    z = (z * _sigmoid(z)).astype(BF16)
    y = jnp.dot(z, w2_ref[0], preferred_element_type=F32) + b2_ref[0]
    _post_mixer(x, y, m, g2_ref, wr_ref[0], br_ref[0], x_out_ref, h_out_ref, ti_ref, tg_ref, TM)


def _conv_layer(x1, moe0, mod4, g_norm1, w_pw1, b_pw1, w_dw, b_dw, ln_g, ln_b, w_pw2, b_pw2,
                g_norm2, wr_t, br_c, moe_w_down):
    def full(shape):
        return pl.BlockSpec(shape, lambda i: (0,) * len(shape))

    pad_rows = (TM // GRID_W) * (GRID_W + CONV_PAD) + CONV_PAD
    return pl.pallas_call(
        _conv_kernel,
        grid=(T // TM,),
        in_specs=[pl.BlockSpec((TM, D), lambda i: (i, 0)),
                  _MOE_OUT_SPEC,
                  pl.BlockSpec((1, 1, 6, D), lambda i: (0, _cond_of_tile(i, TM), 0, 0)),
                  pl.BlockSpec((1, 1, D), lambda i: (1, 0, 0)),
                  full((1, D, 2 * D)), full((1, 1, 2 * D)), full((1, CONV_W, D)), full((1, 1, D)),
                  full((1, 1, D)), full((1, 1, D)), full((1, D, D)), full((1, 1, D))]
        + _post_in_specs(1) + [_wd_in_spec(1)],
        out_specs=_POST_OUT_SPECS + [_WD_OUT_SPEC],
        out_shape=_POST_OUT_SHAPES + [_WD_OUT_SHAPE],
        scratch_shapes=[pltpu.VMEM((pad_rows, D), F32), pltpu.VMEM((TM, D), F32),
                        pltpu.VMEM((D // LANES * SUBLANES, CONV_CHUNK + 2 * CONV_PAD - SUBLANES, LANES), F32)],
        compiler_params=_cparams(("arbitrary",)),
        name="conv",
    )(x1, moe0, mod4, g_norm1, w_pw1, b_pw1.reshape(1, 1, 2 * D), w_dw, b_dw.reshape(1, 1, D),
      ln_g.reshape(1, 1, D), ln_b.reshape(1, 1, D), w_pw2, b_pw2.reshape(1, 1, D),
      mod4, g_norm2, wr_t, br_c, moe_w_down)


def _final_kernel(x_ref, moe_ref, mod_ref, g_ref, yp_ref, ys_ref):
    i = pl.program_id(0)
    x = x_ref[...] + mod_ref[0, 0][5:6] * _load_token_major(moe_ref, TM, (0,))
    y = _rms(x) * g_ref[...]

    @pl.when(i < T_PROMPT // TM)
    def _():
        yp_ref[...] = y

    @pl.when(i >= T_PROMPT // TM)
    def _():
        ys_ref[...] = y


def _final(x, moe1, mod4, g_final):
    n_p = T_PROMPT // TM
    return pl.pallas_call(
        _final_kernel,
        grid=(T // TM,),
        in_specs=[pl.BlockSpec((TM, D), lambda i: (i, 0)),
                  _MOE_OUT_SPEC,
                  pl.BlockSpec((1, 1, 6, D), lambda i: (1, _cond_of_tile(i, TM), 0, 0)),
                  pl.BlockSpec((1, D), lambda i: (0, 0))],
        out_specs=[pl.BlockSpec((TM, D), lambda i: (jnp.minimum(i, n_p - 1), 0)),
                   pl.BlockSpec((TM, D), lambda i: (jnp.maximum(i - n_p, 0), 0))],
        out_shape=[jax.ShapeDtypeStruct((T_PROMPT, D), F32), jax.ShapeDtypeStruct((T_SAMPLE, D), F32)],
        compiler_params=_cparams(("arbitrary",)),
        name="final",
    )(x, moe1, mod4, g_final)


def kernel(x_prompt, x_sample, c, state_ret_fwd, state_ret_bwd, c_ctx, w_ada, b_ada, g_norm1, g_norm2,
           ret_w_in, ret_decay, ret_w_out, conv_w_pw1, conv_b_pw1, conv_w_dw, conv_b_dw, conv_ln_g,
           conv_ln_b, conv_w_pw2, conv_b_pw2, moe_w_router, moe_b_router, moe_w_up, moe_b_up,
           moe_w_down, moe_b_down, g_final):
    xp = x_prompt.reshape(T_PROMPT, D)
    xs = x_sample.reshape(T_SAMPLE, D)
    cond = jnp.concatenate([c_ctx[None, :], c, jnp.zeros((SUBLANES - 1 - N_SAMPLE, D), F32)], axis=0)
    mod = _ada(cond.T, w_ada, b_ada)
    mod4 = mod.reshape(mod.shape[0], SUBLANES, 6, D)
    wr_t = jnp.swapaxes(moe_w_router, 1, 2)
    br_c = moe_b_router[:, :, None]
    gn1 = g_norm1[:, None, :]
    gn2 = g_norm2[:, None, :]

    proj = _proj(xp, xs, g_norm1[0:1], mod4, ret_w_in[0].astype(BF16))
    og_p, new_f, new_b = _ret_prompt(ret_decay[0], proj)
    og_s = _ret_sample(ret_decay[0], proj, state_ret_fwd, state_ret_bwd)
    x1, h_tm, ti, tg, w_down_bf16 = _ret_out(og_p, og_s, ret_w_out[0].astype(BF16), xp, xs, mod4, gn2,
                                             wr_t, br_c, moe_w_down)
    moe0 = _moe_layer(0, ti, tg, h_tm, moe_w_up, moe_b_up, w_down_bf16, moe_b_down)

    x2, h_tm, ti, tg, w_down_bf16 = _conv_layer(
        x1, moe0, mod4, gn1, conv_w_pw1.astype(BF16), conv_b_pw1, conv_w_dw, conv_b_dw, conv_ln_g,
        conv_ln_b, conv_w_pw2.astype(BF16), conv_b_pw2, gn2, wr_t, br_c, moe_w_down)
    moe1 = _moe_layer(1, ti, tg, h_tm, moe_w_up, moe_b_up, w_down_bf16, moe_b_down)

    y_p, y_s = _final(x2, moe1, mod4, g_final[None, :])
    return (y_p.reshape(N_PROMPT, L_PROMPT, D), y_s.reshape(N_SAMPLE, L_SAMPLE, D), new_f, new_b)
```

```python
import functools

import numpy as np
import jax
import jax.numpy as jnp
from jax import lax
from jax.experimental import pallas as pl
from jax.experimental.pallas import tpu as pltpu

F32 = jnp.float32
BF16 = jnp.bfloat16

D = 1024
N_PROMPT, L_PROMPT = 32, 256
N_SAMPLE, L_SAMPLE = 2, 4096
T_PROMPT = N_PROMPT * L_PROMPT
T_SAMPLE = N_SAMPLE * L_SAMPLE
T = T_PROMPT + T_SAMPLE
GRID_W = 64
HEADS, DK, DV = 4, 256, 512
HK, HV = HEADS * DK, HEADS * DV
N_PROJ = 2 * HK + 2 * HV
ROPE_BASE = 10000.0
CONV_W = 31
CONV_PAD = 16
CONV_CHUNK = 64
N_EXP, TOP_K, D_EXP = 32, 4, 1024
SWIGLU_LIMIT, SWIGLU_ALPHA = 7.0, 1.702
EPS = 1e-6

LANES = 128
SUBLANES = 8
ROW_CHUNKS = D // LANES
VMEM_LIMIT = 56 * 1024 * 1024

TM = 512
MOE_TILE = 4096
MOE_BLK = 192
MOE_NBLK = 128
assert MOE_NBLK * MOE_BLK >= MOE_TILE * TOP_K + N_EXP * (MOE_BLK - 1) and (MOE_NBLK * MOE_BLK) % 1024 == 0
MOE_SLOTS = MOE_NBLK * MOE_BLK
MOE_SPLITS = 2
META_LANES = 2 * LANES
CODE_SHIFT = 16
N_MOE_TILES = T // MOE_TILE
RET_CHUNK = 256

NT_DIMS = (((1,), (1,)), ((), ()))
TN_DIMS = (((0,), (0,)), ((), ()))


def _cparams(sem):
    return pltpu.CompilerParams(dimension_semantics=sem, vmem_limit_bytes=VMEM_LIMIT)


def _cond_of_tile(i, tm):
    return jnp.maximum((i * tm) // L_SAMPLE - 1, 0)


def _sigmoid(x):
    return 0.5 * jnp.tanh(0.5 * x) + 0.5


def _rms(x):
    return x * lax.rsqrt(jnp.mean(x * x, axis=-1, keepdims=True) + EPS)


def _pick_x(i, tm, xp_ref, xs_ref):
    return jnp.where(i < T_PROMPT // tm, xp_ref[...], xs_ref[...])


def _x_specs(tm):
    n_p = T_PROMPT // tm
    return [pl.BlockSpec((tm, D), lambda i, *_: (jnp.minimum(i, n_p - 1), 0)),
            pl.BlockSpec((tm, D), lambda i, *_: (jnp.maximum(i - n_p, 0), 0))]


def _ada_kernel(ct_ref, w_ref, b_ref, o_ref):
    ct = ct_ref[...]
    s = ct * _sigmoid(ct)
    w = w_ref[0]
    rows = [jnp.sum(w * s[:, r:r + 1], axis=0, keepdims=True) for r in range(3)]
    rows.append(jnp.zeros((SUBLANES - 3, w.shape[1]), F32))
    o_ref[0] = jnp.concatenate(rows, axis=0) + b_ref[0]


def _ada(ct, w_ada, b_ada):
    depth, _, n = w_ada.shape
    tn = 1536
    return pl.pallas_call(
        _ada_kernel,
        grid=(depth, n // tn),
        in_specs=[pl.BlockSpec((D, SUBLANES), lambda l, j: (0, 0)),
                  pl.BlockSpec((1, D, tn), lambda l, j: (l, 0, j)),
                  pl.BlockSpec((1, 1, tn), lambda l, j: (l, 0, j))],
        out_specs=pl.BlockSpec((1, SUBLANES, tn), lambda l, j: (l, 0, j)),
        out_shape=jax.ShapeDtypeStruct((depth, SUBLANES, n), F32),
        compiler_params=_cparams(("arbitrary", "arbitrary")),
        name="ada",
    )(ct, w_ada, b_ada.reshape(depth, 1, n))


def _rope_tables():
    t = np.arange(L_SAMPLE)
    row = (t // GRID_W).astype(np.float32)
    col = (t % GRID_W).astype(np.float32)
    nf = DK // 4
    inv = (np.float32(ROPE_BASE) ** (-np.arange(nf, dtype=np.float32) / np.float32(nf))).astype(np.float32)
    cos, sin = [], []
    for pos in (row, col):
        ang = (pos[:, None] * inv[None, :]).astype(np.float32)
        c, s = np.cos(ang).astype(np.float32), np.sin(ang).astype(np.float32)
        cos += [c, c]
        sin += [-s, s]
    return np.concatenate(cos, axis=1), np.concatenate(sin, axis=1)


def _proj_kernel(xp_ref, xs_ref, g_ref, mod_ref, w_ref, cos_ref, sin_ref, o_ref):
    i = pl.program_id(0)
    m = mod_ref[0, 0]
    h = _rms(_pick_x(i, TM, xp_ref, xs_ref)) * (g_ref[...] * (1.0 + m[1:2]))
    h = (h + m[0:1]).astype(BF16)
    is_sample = i >= T_PROMPT // TM
    cos = jnp.where(is_sample, cos_ref[...], 1.0)
    sin = jnp.where(is_sample, sin_ref[...], 0.0)
    for col in range(N_PROJ // D):
        acc = jnp.dot(h, w_ref[:, col * D:(col + 1) * D], preferred_element_type=F32)
        if col == 1:
            acc = acc * DK ** -0.5
        if col < 2:
            for c in range(D // LANES):
                a = acc[:, c * LANES:(c + 1) * LANES]
                p = (c % 2) * LANES
                r = a * cos[:, p:p + LANES] + pltpu.roll(a, LANES // 2, 1) * sin[:, p:p + LANES]
                o_ref[:, col * D + c * LANES:col * D + (c + 1) * LANES] = r.astype(BF16)
        else:
            o_ref[:, col * D:(col + 1) * D] = acc.astype(BF16)


def _proj(xp, xs, g1, mod4, w_in):
    cos, sin = _rope_tables()
    n_p = T_PROMPT // TM
    n_s = L_SAMPLE // TM
    tab_spec = pl.BlockSpec((TM, DK), lambda i: (jnp.maximum(i - n_p, 0) % n_s, 0))
    return pl.pallas_call(
        _proj_kernel,
        grid=(T // TM,),
        in_specs=_x_specs(TM) + [
            pl.BlockSpec((1, D), lambda i: (0, 0)),
            pl.BlockSpec((1, 1, 6, D), lambda i: (0, _cond_of_tile(i, TM), 0, 0)),
            pl.BlockSpec((D, N_PROJ), lambda i: (0, 0)),
            tab_spec, tab_spec],
        out_specs=pl.BlockSpec((TM, N_PROJ), lambda i: (i, 0)),
        out_shape=jax.ShapeDtypeStruct((T, N_PROJ), BF16),
        compiler_params=_cparams(("arbitrary",)),
        name="proj",
    )(xp, xs, g1, mod4, w_in, jnp.asarray(cos), jnp.asarray(sin))


def _log_decays(dec_ref, head):
    out = []
    for direction in range(2):
        d = jnp.full((1, 1), dec_ref[direction, head], F32)
        out.append(jnp.minimum(d, 0.0) - jnp.log(1.0 + jnp.exp(-jnp.abs(d))))
    return out


def _decay_mask(lgf, lgb, c):
    ii = lax.broadcasted_iota(jnp.int32, (c, c), 0)
    jj = lax.broadcasted_iota(jnp.int32, (c, c), 1)
    diff = (ii - jj).astype(F32)
    fwd = jnp.where(diff >= 0, jnp.exp(lgf * jnp.maximum(diff, 0.0)), 0.0)
    bwd = jnp.where(diff <= 0, jnp.exp(lgb * jnp.maximum(-diff, 0.0)), 0.0)
    return fwd + bwd


def _norm_gate(o, g):
    g = g.astype(F32)
    return (_rms(o) * (g * _sigmoid(g))).astype(BF16)


def _ret_prompt_kernel(dec_ref, p_ref, o_ref, sf_ref, sb_ref):
    c = L_PROMPT
    pos = lax.broadcasted_iota(jnp.int32, (c, 1), 0).astype(F32)
    for head in range(HEADS):
        lgf, lgb = _log_decays(dec_ref, head)
        q = p_ref[:, head * DK:(head + 1) * DK]
        k = p_ref[:, HK + head * DK:HK + (head + 1) * DK]
        v = p_ref[:, 2 * HK + head * DV:2 * HK + (head + 1) * DV]
        g = p_ref[:, 2 * HK + HV + head * DV:2 * HK + HV + (head + 1) * DV]
        s = lax.dot_general(q, k, NT_DIMS, preferred_element_type=F32) * _decay_mask(lgf, lgb, c)
        o = jnp.dot(s.astype(BF16), v, preferred_element_type=F32)
        o_ref[:, head * DV:(head + 1) * DV] = _norm_gate(o, g)
        kf = k.astype(F32)
        k_fwd = (kf * jnp.exp(lgf * (c - 1.0 - pos))).astype(BF16)
        k_bwd = (kf * jnp.exp(lgb * pos)).astype(BF16)
        sf_ref[0, 0, head] = lax.dot_general(k_fwd, v, TN_DIMS, preferred_element_type=F32)
        sb_ref[0, 0, head] = lax.dot_general(k_bwd, v, TN_DIMS, preferred_element_type=F32)


def _ret_specs(seq_len, row0):
    r = row0 // seq_len
    return [pl.BlockSpec((seq_len, DK), lambda b, h: (r + b, h)),
            pl.BlockSpec((seq_len, DK), lambda b, h: (r + b, HK // DK + h)),
            pl.BlockSpec((seq_len, DV), lambda b, h: (r + b, 2 * HK // DV + h)),
            pl.BlockSpec((seq_len, DV), lambda b, h: (r + b, (2 * HK + HV) // DV + h))]


def _ret_prompt(decay, proj):
    state = jax.ShapeDtypeStruct((N_PROMPT, 1, HEADS, DK, DV), F32)
    state_spec = pl.BlockSpec((1, 1, HEADS, DK, DV), lambda b: (b, 0, 0, 0, 0))
    return pl.pallas_call(
        _ret_prompt_kernel,
        grid=(N_PROMPT,),
        in_specs=[pl.BlockSpec(memory_space=pltpu.SMEM),
                  pl.BlockSpec((L_PROMPT, N_PROJ), lambda b: (b, 0))],
        out_specs=[pl.BlockSpec((L_PROMPT, HV), lambda b: (b, 0)), state_spec, state_spec],
        out_shape=[jax.ShapeDtypeStruct((T_PROMPT, HV), BF16), state, state],
        compiler_params=_cparams(("arbitrary",)),
        name="ret_prompt",
    )(decay, proj)


def _ret_sample_kernel(dec_ref, q_ref, k_ref, v_ref, g_ref, s0f_ref, s0b_ref, o_ref,
                       of_scr, ob_scr, sf_scr, sb_scr, dm_scr):
    c = RET_CHUNK
    nc = L_SAMPLE // c
    lgf, lgb = _log_decays(dec_ref, pl.program_id(1))
    dm_scr[...] = _decay_mask(lgf, lgb, c)
    pos = lax.broadcasted_iota(jnp.int32, (c, 1), 0).astype(F32)

    def chunk(ci):
        rows = pl.ds(pl.multiple_of(ci * c, c), c)
        return rows, q_ref[rows, :], k_ref[rows, :], v_ref[rows, :]

    def state_update(s_scr, lg, write_pos, kc, vc):
        kw = (kc.astype(F32) * jnp.exp(lg * write_pos)).astype(BF16)
        s_scr[...] = s_scr[...] * jnp.exp(lg * c) + lax.dot_general(kw, vc, TN_DIMS, preferred_element_type=F32)

    def read_state(s_scr, lg, read_pos, qc):
        qr = (qc.astype(F32) * jnp.exp(lg * read_pos)).astype(BF16)
        return jnp.dot(qr, s_scr[...].astype(BF16), preferred_element_type=F32)

    sf_scr[...] = s0f_ref[0, 0, 0]
    sb_scr[...] = s0b_ref[0, 0, 0]

    def scan(step, carry):
        rows, qc, kc, vc = chunk(nc - 1 - step)
        ob_scr[rows, :] = read_state(sb_scr, lgb, c - pos, qc)
        state_update(sb_scr, lgb, pos, kc, vc)
        rows, qc, kc, vc = chunk(step)
        s = lax.dot_general(qc, kc, NT_DIMS, preferred_element_type=F32) * dm_scr[...]
        o = jnp.dot(s.astype(BF16), vc, preferred_element_type=F32)
        of_scr[rows, :] = o + read_state(sf_scr, lgf, pos + 1.0, qc)
        state_update(sf_scr, lgf, c - 1.0 - pos, kc, vc)
        return carry

    lax.fori_loop(0, nc, scan, 0)

    def finish(ci, carry):
        rows = pl.ds(pl.multiple_of(ci * c, c), c)
        o_ref[rows, :] = _norm_gate(of_scr[rows, :] + ob_scr[rows, :], g_ref[rows, :])
        return carry

    lax.fori_loop(0, nc, finish, 0)


def _ret_sample(decay, proj, s0f, s0b):
    state_spec = pl.BlockSpec((1, 1, 1, DK, DV), lambda b, h: (b, 0, h, 0, 0))
    return pl.pallas_call(
        _ret_sample_kernel,
        grid=(N_SAMPLE, HEADS),
        in_specs=[pl.BlockSpec(memory_space=pltpu.SMEM)] + _ret_specs(L_SAMPLE, T_PROMPT)
        + [state_spec, state_spec],
        out_specs=pl.BlockSpec((L_SAMPLE, DV), lambda b, h: (b, h)),
        out_shape=jax.ShapeDtypeStruct((T_SAMPLE, HV), BF16),
        scratch_shapes=[pltpu.VMEM((L_SAMPLE, DV), F32), pltpu.VMEM((L_SAMPLE, DV), F32),
                        pltpu.VMEM((DK, DV), F32), pltpu.VMEM((DK, DV), F32),
                        pltpu.VMEM((RET_CHUNK, RET_CHUNK), F32)],
        compiler_params=_cparams(("arbitrary", "arbitrary")),
        name="ret_sample",
    )(decay, proj, proj, proj, proj, s0f, s0b)


def _store_token_major(ref, val, tm):
    for s in range(ROW_CHUNKS):
        ref[pl.ds(s, tm, stride=ROW_CHUNKS), :] = val[:, s * LANES:(s + 1) * LANES]


def _load_token_major(ref, tm, lead=()):
    return jnp.concatenate([ref[lead + (pl.ds(s, tm, stride=ROW_CHUNKS), slice(None))]
                            for s in range(ROW_CHUNKS)], axis=1)


def _post_mixer(x, y, m, g2_ref, wr_ref, br_ref, x_out_ref, h_out_ref, ti_ref, tg_ref, tm):
    x1 = x + m[2:3] * y
    h = _rms(x1) * (g2_ref[0] * (1.0 + m[4:5])) + m[3:4]
    x_out_ref[...] = x1
    _store_token_major(h_out_ref, h, tm)
    w = wr_ref[...]
    w_hi = w.astype(BF16)
    w_lo = (w - w_hi.astype(F32)).astype(BF16)
    h_hi = h.astype(BF16)
    h_lo = (h - h_hi.astype(F32)).astype(BF16)
    dot = functools.partial(lax.dot_general, dimension_numbers=NT_DIMS, preferred_element_type=F32)
    cur = dot(w_hi, h_hi) + dot(w_hi, h_lo) + dot(w_lo, h_hi) + br_ref[...]
    ie = lax.broadcasted_iota(jnp.int32, (N_EXP, tm), 0).astype(F32)
    vals, idxs = [], []
    for _ in range(TOP_K):
        top = jnp.max(cur, axis=0, keepdims=True)
        idx = jnp.min(jnp.where(cur == top, ie, float(N_EXP)), axis=0, keepdims=True)
        vals.append(top)
        idxs.append(idx)
        cur = jnp.where(ie == idx, -jnp.inf, cur)
    ex = [jnp.exp(v - vals[0]) for v in vals]
    den = ex[0] + ex[1] + ex[2] + ex[3]
    pad = jnp.zeros((SUBLANES - TOP_K, tm), F32)
    ti_ref[...] = jnp.concatenate(idxs + [pad], axis=0).astype(jnp.int32)
    tg_ref[...] = jnp.concatenate([e / den for e in ex] + [pad], axis=0)


def _post_in_specs(layer):
    return [pl.BlockSpec((1, 1, 6, D), lambda i: (layer, _cond_of_tile(i, TM), 0, 0)),
            pl.BlockSpec((1, 1, D), lambda i: (layer, 0, 0)),
            pl.BlockSpec((1, N_EXP, D), lambda i: (layer, 0, 0)),
            pl.BlockSpec((1, N_EXP, 1), lambda i: (layer, 0, 0))]


_POST_OUT_SPECS = [pl.BlockSpec((TM, D), lambda i: (i, 0)),
                   pl.BlockSpec((TM * ROW_CHUNKS, LANES), lambda i: (i, 0)),
                   pl.BlockSpec((SUBLANES, TM), lambda i: (0, i)),
                   pl.BlockSpec((SUBLANES, TM), lambda i: (0, i))]
_POST_OUT_SHAPES = [jax.ShapeDtypeStruct((T, D), F32),
                    jax.ShapeDtypeStruct((T * ROW_CHUNKS, LANES), F32),
                    jax.ShapeDtypeStruct((SUBLANES, T), jnp.int32),
                    jax.ShapeDtypeStruct((SUBLANES, T), F32)]


def _ret_out_kernel(ogp_ref, ogs_ref, w_ref, xp_ref, xs_ref, mod_ref, g2_ref, wr_ref, br_ref, wd_ref,
                    x_out_ref, h_out_ref, ti_ref, tg_ref, wd_out_ref):
    wd_out_ref[...] = wd_ref[0].astype(BF16)
    i = pl.program_id(0)
    og = jnp.where(i < T_PROMPT // TM, ogp_ref[...], ogs_ref[...])
    y = jnp.dot(og, w_ref[...], preferred_element_type=F32)
    _post_mixer(_pick_x(i, TM, xp_ref, xs_ref), y, mod_ref[0, 0], g2_ref, wr_ref[0], br_ref[0],
                x_out_ref, h_out_ref, ti_ref, tg_ref, TM)


assert T // TM == N_EXP
_WD_OUT_SPEC = pl.BlockSpec((1, D_EXP, D), lambda i: (i, 0, 0))
_WD_OUT_SHAPE = jax.ShapeDtypeStruct((N_EXP, D_EXP, D), BF16)


def _wd_in_spec(layer):
    return pl.BlockSpec((1, 1, D_EXP, D), lambda i: (layer, i, 0, 0))


def _ret_out(og_p, og_s, w_out, xp, xs, mod4, g_norm2, wr_t, br_c, moe_w_down):
    n_p = T_PROMPT // TM
    return pl.pallas_call(
        _ret_out_kernel,
        grid=(T // TM,),
        in_specs=[pl.BlockSpec((TM, HV), lambda i: (jnp.minimum(i, n_p - 1), 0)),
                  pl.BlockSpec((TM, HV), lambda i: (jnp.maximum(i - n_p, 0), 0)),
                  pl.BlockSpec((HV, D), lambda i: (0, 0))]
        + _x_specs(TM) + _post_in_specs(0) + [_wd_in_spec(0)],
        out_specs=_POST_OUT_SPECS + [_WD_OUT_SPEC],
        out_shape=_POST_OUT_SHAPES + [_WD_OUT_SHAPE],
        compiler_params=_cparams(("arbitrary",)),
        name="ret_out",
    )(og_p, og_s, w_out, xp, xs, mod4, g_norm2, wr_t, br_c, moe_w_down)


def _route_a_kernel(ti_ref, dest_ref, meta_ref):
    tt, tm = MOE_TILE, MOE_BLK
    ti = ti_ref[...]
    ie = lax.broadcasted_iota(jnp.int32, (N_EXP, tt), 0)
    onehots = [(ie == ti[k:k + 1]).astype(F32) for k in range(TOP_K)]
    oh = onehots[0] + onehots[1] + onehots[2] + onehots[3]
    ch = 512
    upper = (lax.broadcasted_iota(jnp.int32, (ch, ch), 0)
             < lax.broadcasted_iota(jnp.int32, (ch, ch), 1)).astype(BF16)
    carry = jnp.zeros((N_EXP, 1), F32)
    cums = []
    for c in range(tt // ch):
        blk = oh[:, c * ch:(c + 1) * ch]
        cums.append(jnp.dot(blk.astype(BF16), upper, preferred_element_type=F32) + carry)
        carry = carry + jnp.sum(blk, axis=1, keepdims=True)
    cum = jnp.concatenate(cums, axis=1)
    cnt = carry
    nb = jnp.floor((cnt + (tm - 0.5)) * (1.0 / tm))
    lower = (lax.broadcasted_iota(jnp.int32, (N_EXP, N_EXP), 1)
             < lax.broadcasted_iota(jnp.int32, (N_EXP, N_EXP), 0)).astype(BF16)
    offb = jnp.dot(lower, jnp.broadcast_to(nb, (N_EXP, LANES)).astype(BF16),
                   preferred_element_type=F32)[:, :1]
    off = offb * tm
    base = off + cum
    dests = [jnp.sum(onehots[k] * base, axis=0, keepdims=True) for k in range(TOP_K)]
    dests.append(jnp.zeros((SUBLANES - TOP_K, tt), F32))
    dest_ref[0] = jnp.concatenate(dests, axis=0).astype(jnp.int32)
    nused = jnp.sum(nb, axis=0, keepdims=True)
    jl = lax.broadcasted_iota(jnp.int32, (N_EXP, META_LANES), 1).astype(F32)
    jc = jnp.minimum(jl, nused - 1.0)
    be = jnp.minimum(jnp.sum(((offb + nb) <= jc).astype(F32), axis=0, keepdims=True), N_EXP - 1.0)
    ief = lax.broadcasted_iota(jnp.int32, (N_EXP, META_LANES), 0).astype(F32)
    end_row = jnp.sum(jnp.where(ief == be, off + cnt, 0.0), axis=0, keepdims=True)
    nvalid = jnp.clip(end_row - jl[:1] * tm, 0.0, float(tm))
    nvalid = jnp.where(jl[:1] < nused, nvalid, 0.0)
    run_end = jnp.sum(jnp.where(ief == be, offb + nb, 0.0), axis=0, keepdims=True)
    nxt = jnp.sum(((offb + nb) <= run_end).astype(F32), axis=0, keepdims=True)
    nxt = jnp.where(run_end < nused, nxt, -1.0)
    meta = jnp.concatenate([be, nvalid, jnp.broadcast_to(nused, (1, META_LANES)), nxt,
                            jnp.zeros((SUBLANES - 4, META_LANES), F32)], axis=0)
    meta_ref[0] = meta.astype(jnp.int32)


def _route_a(ti):
    return pl.pallas_call(
        _route_a_kernel,
        grid=(N_MOE_TILES,),
        in_specs=[pl.BlockSpec((SUBLANES, MOE_TILE), lambda i: (0, i))],
        out_specs=[pl.BlockSpec((1, SUBLANES, MOE_TILE), lambda i: (i, 0, 0)),
                   pl.BlockSpec((1, SUBLANES, META_LANES), lambda i: (i, 0, 0))],
        out_shape=[jax.ShapeDtypeStruct((N_MOE_TILES, SUBLANES, MOE_TILE), jnp.int32),
                   jax.ShapeDtypeStruct((N_MOE_TILES, SUBLANES, META_LANES), jnp.int32)],
        compiler_params=_cparams(("arbitrary",)),
        name="route_a",
    )(ti)


CAST_EXPERTS = 2
ROUTE_STEPS = N_EXP // CAST_EXPERTS // N_MOE_TILES


def _cast_route_kernel(wu_ref, dest_ref, meta_ref, ou_ref, slot_ref):
    ou_ref[...] = wu_ref[0].astype(BF16)
    part = pl.program_id(0) % ROUTE_STEPS
    group = SUBLANES

    def per_block(j, carry):
        def pad(s, c2):
            slot_ref[j * MOE_BLK + s] = MOE_TILE * ROW_CHUNKS
            return c2

        def pad_group(gi, c2):
            slots = slot_ref.at[pl.ds(j * MOE_BLK + gi * group, group)]
            for u in range(group):
                slots[u] = MOE_TILE * ROW_CHUNKS
            return c2

        n_real = meta_ref[0, 1, j]
        first_group = (n_real + group - 1) // group
        lax.fori_loop(n_real, first_group * group, pad, 0)
        lax.fori_loop(first_group, MOE_BLK // group, pad_group, 0)
        return carry

    blocks = MOE_NBLK // ROUTE_STEPS
    lax.fori_loop(part * blocks, (part + 1) * blocks, per_block, 0)

    tokens = MOE_TILE // ROUTE_STEPS

    def per_group(tg, carry):
        t0 = part * tokens + tg * group
        rows = [dest_ref.at[0, k, pl.ds(t0, group)] for k in range(TOP_K)]
        tok_code = t0 * ((1 << CODE_SHIFT) + ROW_CHUNKS)
        for u0 in range(0, group, 2):
            loaded = [(k, u, rows[k][u]) for u in range(u0, u0 + 2) for k in range(TOP_K)]
            for k, u, d in loaded:
                slot_ref[d] = tok_code + ((k * MOE_TILE + u) << CODE_SHIFT) + u * ROW_CHUNKS
        return carry

    lax.fori_loop(0, tokens // group, per_group, 0)


def _cast_route(w_up, layer, dest, meta):
    tile = lambda e: e // ROUTE_STEPS
    return pl.pallas_call(
        _cast_route_kernel,
        grid=(N_EXP // CAST_EXPERTS,),
        in_specs=[pl.BlockSpec((1, CAST_EXPERTS) + w_up.shape[2:], lambda e: (layer, e, 0, 0)),
                  pl.BlockSpec((1, SUBLANES, MOE_TILE), lambda e: (tile(e), 0, 0), memory_space=pltpu.SMEM),
                  pl.BlockSpec((1, SUBLANES, META_LANES), lambda e: (tile(e), 0, 0),
                               memory_space=pltpu.SMEM)],
        out_specs=[pl.BlockSpec((CAST_EXPERTS,) + w_up.shape[2:], lambda e: (e, 0, 0)),
                   pl.BlockSpec((MOE_SLOTS,), lambda e: (tile(e),), memory_space=pltpu.SMEM)],
        out_shape=[jax.ShapeDtypeStruct(w_up.shape[1:], BF16),
                   jax.ShapeDtypeStruct((N_MOE_TILES * MOE_SLOTS,), jnp.int32)],
        compiler_params=_cparams(("arbitrary",)),
        name="cast_route",
    )(w_up, dest, meta)


def _moe_kernel(be_ref, nu_ref, nx_ref, slot_ref, gate_ref, bu_ref, bd_ref, h_hbm, wu_hbm, wd_hbm,
                out_hbm, g0_scr, g1_scr, y0_scr, y1_scr, wu_scr, wd_scr, h_scr, out_scr, sem, tile_sem):
    i = pl.program_id(0)
    tm = MOE_BLK
    nused = nu_ref[i]
    base = i * MOE_NBLK

    def weight_copies(e, buf):
        return (pltpu.make_async_copy(wu_hbm.at[e], wu_scr.at[buf], sem.at[0, buf]),
                pltpu.make_async_copy(wd_hbm.at[e], wd_scr.at[buf], sem.at[1, buf]))
    tile_rows = MOE_TILE * ROW_CHUNKS
    h_copy = pltpu.make_async_copy(h_hbm.at[pl.ds(i * tile_rows, tile_rows)], h_scr, tile_sem.at[0])
    out_copy = pltpu.make_async_copy(out_scr, out_hbm.at[i], tile_sem.at[1])

    def row_slice(ii):
        return slice(ii * ROW_CHUNKS, (ii + 1) * ROW_CHUNKS)

    def gather(blk, g_scr, lo, hi):
        for ii in range(lo, hi):
            off = pl.multiple_of(slot_ref[blk * tm + ii] & (tile_rows - 1), ROW_CHUNKS)
            g_scr[row_slice(ii), :] = h_scr[pl.ds(off, ROW_CHUNKS), :]

    def scatter(blk, y_scr, lo, hi):
        batch = 8
        for i0 in range(lo, hi, batch):
            pending = []
            for ii in range(i0, i0 + batch):
                code = slot_ref[blk * tm + ii]
                gate = gate_ref[code >> CODE_SHIFT]
                off = code & ((1 << CODE_SHIFT) - 1)
                rows = pl.ds(pl.multiple_of(off, ROW_CHUNKS), ROW_CHUNKS)
                pending.append((rows, out_scr[rows, :] + gate * y_scr[row_slice(ii), :]))
            for rows, val in pending:
                out_scr[rows, :] = val

    h_copy.start()
    for copy in weight_copies(be_ref[base], 0):
        copy.start()
    out_scr[...] = jnp.zeros(out_scr.shape, F32)
    y1_scr[...] = jnp.zeros(y1_scr.shape, F32)
    h_copy.wait()
    gather(0, g0_scr, 0, tm)

    def step(j, run, g_cur, g_nxt, y_cur, y_prv):
        e = be_ref[base + j]
        jp = jnp.maximum(j - 1, 0)
        jn = jnp.minimum(j + 1, nused - 1)
        first = jnp.logical_or(j == 0, be_ref[base + jp] != e)
        run = run + jnp.where(jnp.logical_and(first, j > 0), 1, 0)
        buf = run & 1

        @pl.when(first)
        def _():
            for copy in weight_copies(e, buf):
                copy.wait()
            nxt = nx_ref[base + j]

            @pl.when(nxt >= 0)
            def _():
                for copy in weight_copies(nxt, 1 - buf):
                    copy.start()

        def compute(wbuf):
            a = _load_token_major(g_cur, tm).astype(BF16)
            width = D_EXP // MOE_SPLITS
            ups = []
            for c in range(MOE_SPLITS):
                g_cols = slice(c * width, (c + 1) * width)
                l_cols = slice(D_EXP + c * width, D_EXP + (c + 1) * width)
                ups.append((jnp.dot(a, wu_scr[wbuf, :, g_cols], preferred_element_type=F32)
                            + bu_ref[e][:, g_cols],
                            jnp.dot(a, wu_scr[wbuf, :, l_cols], preferred_element_type=F32)
                            + bu_ref[e][:, l_cols]))
            scatter(jp, y_prv, 0, tm)
            gather(jn, g_nxt, 0, tm)
            y = bd_ref[e]
            for c, (glu, lin) in enumerate(ups):
                glu = jnp.minimum(glu, SWIGLU_LIMIT)
                lin = jnp.clip(lin, -SWIGLU_LIMIT, SWIGLU_LIMIT)
                act = glu * _sigmoid(SWIGLU_ALPHA * glu) * (lin + 1.0)
                y = y + jnp.dot(act.astype(BF16), wd_scr[wbuf, c * width:(c + 1) * width, :],
                                preferred_element_type=F32)
            _store_token_major(y_cur, y, tm)

        for wbuf in range(2):
            pl.when(buf == wbuf)(functools.partial(compute, wbuf))

        @pl.when(j == nused - 1)
        def _():
            scatter(j, y_cur, 0, tm)

        return run

    def pair(jj, run):
        run = step(2 * jj, run, g0_scr, g1_scr, y0_scr, y1_scr)
        return lax.cond(2 * jj + 1 < nused,
                        lambda r: step(2 * jj + 1, r, g1_scr, g0_scr, y1_scr, y0_scr),
                        lambda r: r, run)

    lax.fori_loop(0, (nused + 1) // 2, pair, jnp.int32(0))
    out_copy.start()
    out_copy.wait()


def _moe(be, nu, nx, slot, gates, h_tm, w_up, b_up, w_down, b_down):
    row_buf = pltpu.VMEM((MOE_BLK * ROW_CHUNKS, LANES), F32)
    grid_spec = pltpu.PrefetchScalarGridSpec(
        num_scalar_prefetch=3,
        grid=(N_MOE_TILES,),
        in_specs=[
            pl.BlockSpec((MOE_SLOTS,), lambda i, *_: (i,), memory_space=pltpu.SMEM),
            pl.BlockSpec((TOP_K * MOE_TILE,), lambda i, *_: (i,), memory_space=pltpu.SMEM),
            pl.BlockSpec((N_EXP, 1, 2 * D_EXP), lambda i, *_: (0, 0, 0)),
            pl.BlockSpec((N_EXP, 1, D), lambda i, *_: (0, 0, 0)),
            pl.BlockSpec(memory_space=pl.ANY),
            pl.BlockSpec(memory_space=pl.ANY),
            pl.BlockSpec(memory_space=pl.ANY),
        ],
        out_specs=pl.BlockSpec(memory_space=pl.ANY),
        scratch_shapes=[row_buf, row_buf, row_buf, row_buf,
                        pltpu.VMEM((2, D, 2 * D_EXP), BF16), pltpu.VMEM((2, D_EXP, D), BF16),
                        pltpu.VMEM((MOE_TILE * ROW_CHUNKS, LANES), F32),
                        pltpu.VMEM(((MOE_TILE + 1) * ROW_CHUNKS, LANES), F32),
                        pltpu.SemaphoreType.DMA((2, 2)), pltpu.SemaphoreType.DMA((2,))],
    )
    return pl.pallas_call(
        _moe_kernel,
        grid_spec=grid_spec,
        out_shape=jax.ShapeDtypeStruct((N_MOE_TILES, (MOE_TILE + 1) * ROW_CHUNKS, LANES), F32),
        compiler_params=_cparams(("arbitrary",)),
        name="moe",
    )(be, nu, nx, slot, gates, b_up.reshape(N_EXP, 1, 2 * D_EXP), b_down.reshape(N_EXP, 1, D),
      h_tm, w_up, w_down)


def _moe_layer(layer, ti, tg, h_tm, w_up, b_up, w_down_bf16, b_down):
    dest, meta = _route_a(ti)
    w_up_bf16, slot = _cast_route(w_up, layer, dest, meta)
    be, nx = (meta[:, r, :MOE_NBLK].reshape(-1) for r in (0, 3))
    nu = meta[:, 2, 0]
    gates = tg[:TOP_K].reshape(TOP_K, N_MOE_TILES, MOE_TILE).transpose(1, 0, 2).reshape(-1)
    return _moe(be, nu, nx, slot, gates, h_tm, w_up_bf16, b_up[layer], w_down_bf16, b_down[layer])


_MOE_OUT_SPEC = pl.BlockSpec((1, TM * ROW_CHUNKS, LANES),
                             lambda i: (i // (MOE_TILE // TM), i % (MOE_TILE // TM), 0))


def _conv_kernel(x_ref, moe_ref, modp_ref, g1_ref, w1_ref, b1_ref, wdw_ref, bdw_ref, lng_ref, lnb_ref,
                 w2_ref, b2_ref, mod_ref, g2_ref, wr_ref, br_ref, wd_ref,
                 x_out_ref, h_out_ref, ti_ref, tg_ref, wd_out_ref, pad_scr, conv_scr, shift_scr):
    wd_out_ref[...] = wd_ref[0].astype(BF16)
    i = pl.program_id(0)
    mp = modp_ref[0, 0]
    m = mod_ref[0, 0]
    x = x_ref[...] + mp[5:6] * _load_token_major(moe_ref, TM, (0,))
    h = (_rms(x) * (g1_ref[0] * (1.0 + m[1:2])) + m[0:1]).astype(BF16)
    ag = jnp.dot(h, w1_ref[0], preferred_element_type=F32) + b1_ref[0]
    u = ag[:, :D] * _sigmoid(ag[:, D:])
    is_prompt = i < T_PROMPT // TM

    def fill(seg):
        pitch = seg + CONV_PAD
        for s in range(TM // seg):
            pad_scr[s * pitch:s * pitch + CONV_PAD, :] = jnp.zeros((CONV_PAD, D), F32)
            pad_scr[s * pitch + CONV_PAD:(s + 1) * pitch, :] = u[s * seg:(s + 1) * seg, :]
        end = (TM // seg) * pitch
        pad_scr[end:end + CONV_PAD, :] = jnp.zeros((CONV_PAD, D), F32)

    @pl.when(is_prompt)
    def _():
        fill(L_PROMPT)

    @pl.when(jnp.logical_not(is_prompt))
    def _():
        fill(GRID_W)

    per_seq = L_PROMPT // CONV_CHUNK
    halo = CONV_CHUNK + 2 * CONV_PAD

    def conv_chunk(c, carry):
        base = jnp.where(is_prompt, (c // per_seq) * (L_PROMPT + CONV_PAD) + (c % per_seq) * CONV_CHUNK,
                         c * (GRID_W + CONV_PAD))
        base = pl.multiple_of(base, SUBLANES)
        out_rows = pl.ds(pl.multiple_of(c * CONV_CHUNK, CONV_CHUNK), CONV_CHUNK)
        for gl in range(D // LANES):
            lanes = slice(gl * LANES, (gl + 1) * LANES)
            blk = pad_scr[pl.ds(base, halo), lanes]
            span = halo - SUBLANES
            for r in range(SUBLANES):
                shift_scr[gl * SUBLANES + r] = blk[r:r + span, :]
            acc = jnp.zeros((CONV_CHUNK, LANES), F32)
            for tap in range(CONV_W):
                lo = CONV_PAD - CONV_W // 2 + tap
                al = lo // SUBLANES * SUBLANES
                acc = acc + (shift_scr[gl * SUBLANES + lo % SUBLANES, al:al + CONV_CHUNK, :]
                             * wdw_ref[0, tap:tap + 1, lanes])
            conv_scr[out_rows, lanes] = acc + bdw_ref[0, :, lanes]
        return carry

    lax.fori_loop(0, TM // CONV_CHUNK, conv_chunk, 0)
    uc = conv_scr[...]
    mu = jnp.mean(uc, axis=-1, keepdims=True)
    var = jnp.mean(jnp.square(uc - mu), axis=-1, keepdims=True)
    z = (uc - mu) * lax.rsqrt(var + EPS) * lng_ref[0] + lnb_ref[0]
    z = (z * _sigmoid(z)).astype(BF16)
    y = jnp.dot(z, w2_ref[0], preferred_element_type=F32) + b2_ref[0]
    _post_mixer(x, y, m, g2_ref, wr_ref[0], br_ref[0], x_out_ref, h_out_ref, ti_ref, tg_ref, TM)


def _conv_layer(x1, moe0, mod4, g_norm1, w_pw1, b_pw1, w_dw, b_dw, ln_g, ln_b, w_pw2, b_pw2,
                g_norm2, wr_t, br_c, moe_w_down):
    def full(shape):
        return pl.BlockSpec(shape, lambda i: (0,) * len(shape))

    pad_rows = (TM // GRID_W) * (GRID_W + CONV_PAD) + CONV_PAD
    return pl.pallas_call(
        _conv_kernel,
        grid=(T // TM,),
        in_specs=[pl.BlockSpec((TM, D), lambda i: (i, 0)),
                  _MOE_OUT_SPEC,
                  pl.BlockSpec((1, 1, 6, D), lambda i: (0, _cond_of_tile(i, TM), 0, 0)),
                  pl.BlockSpec((1, 1, D), lambda i: (1, 0, 0)),
                  full((1, D, 2 * D)), full((1, 1, 2 * D)), full((1, CONV_W, D)), full((1, 1, D)),
                  full((1, 1, D)), full((1, 1, D)), full((1, D, D)), full((1, 1, D))]
        + _post_in_specs(1) + [_wd_in_spec(1)],
        out_specs=_POST_OUT_SPECS + [_WD_OUT_SPEC],
        out_shape=_POST_OUT_SHAPES + [_WD_OUT_SHAPE],
        scratch_shapes=[pltpu.VMEM((pad_rows, D), F32), pltpu.VMEM((TM, D), F32),
                        pltpu.VMEM((D // LANES * SUBLANES, CONV_CHUNK + 2 * CONV_PAD - SUBLANES, LANES), F32)],
        compiler_params=_cparams(("arbitrary",)),
        name="conv",
    )(x1, moe0, mod4, g_norm1, w_pw1, b_pw1.reshape(1, 1, 2 * D), w_dw, b_dw.reshape(1, 1, D),
      ln_g.reshape(1, 1, D), ln_b.reshape(1, 1, D), w_pw2, b_pw2.reshape(1, 1, D),
      mod4, g_norm2, wr_t, br_c, moe_w_down)


def _final_kernel(x_ref, moe_ref, mod_ref, g_ref, yp_ref, ys_ref):
    i = pl.program_id(0)
    x = x_ref[...] + mod_ref[0, 0][5:6] * _load_token_major(moe_ref, TM, (0,))
    y = _rms(x) * g_ref[...]

    @pl.when(i < T_PROMPT // TM)
    def _():
        yp_ref[...] = y

    @pl.when(i >= T_PROMPT // TM)
    def _():
        ys_ref[...] = y


def _final(x, moe1, mod4, g_final):
    n_p = T_PROMPT // TM
    return pl.pallas_call(
        _final_kernel,
        grid=(T // TM,),
        in_specs=[pl.BlockSpec((TM, D), lambda i: (i, 0)),
                  _MOE_OUT_SPEC,
                  pl.BlockSpec((1, 1, 6, D), lambda i: (1, _cond_of_tile(i, TM), 0, 0)),
                  pl.BlockSpec((1, D), lambda i: (0, 0))],
        out_specs=[pl.BlockSpec((TM, D), lambda i: (jnp.minimum(i, n_p - 1), 0)),
                   pl.BlockSpec((TM, D), lambda i: (jnp.maximum(i - n_p, 0), 0))],
        out_shape=[jax.ShapeDtypeStruct((T_PROMPT, D), F32), jax.ShapeDtypeStruct((T_SAMPLE, D), F32)],
        compiler_params=_cparams(("arbitrary",)),
        name="final",
    )(x, moe1, mod4, g_final)


def kernel(x_prompt, x_sample, c, state_ret_fwd, state_ret_bwd, c_ctx, w_ada, b_ada, g_norm1, g_norm2,
           ret_w_in, ret_decay, ret_w_out, conv_w_pw1, conv_b_pw1, conv_w_dw, conv_b_dw, conv_ln_g,
           conv_ln_b, conv_w_pw2, conv_b_pw2, moe_w_router, moe_b_router, moe_w_up, moe_b_up,
           moe_w_down, moe_b_down, g_final):
    xp = x_prompt.reshape(T_PROMPT, D)
    xs = x_sample.reshape(T_SAMPLE, D)
    cond = jnp.concatenate([c_ctx[None, :], c, jnp.zeros((SUBLANES - 1 - N_SAMPLE, D), F32)], axis=0)
    mod = _ada(cond.T, w_ada, b_ada)
    mod4 = mod.reshape(mod.shape[0], SUBLANES, 6, D)
    wr_t = jnp.swapaxes(moe_w_router, 1, 2)
    br_c = moe_b_router[:, :, None]
    gn1 = g_norm1[:, None, :]
    gn2 = g_norm2[:, None, :]

    proj = _proj(xp, xs, g_norm1[0:1], mod4, ret_w_in[0].astype(BF16))
    og_p, new_f, new_b = _ret_prompt(ret_decay[0], proj)
    og_s = _ret_sample(ret_decay[0], proj, state_ret_fwd, state_ret_bwd)
    x1, h_tm, ti, tg, w_down_bf16 = _ret_out(og_p, og_s, ret_w_out[0].astype(BF16), xp, xs, mod4, gn2,
                                             wr_t, br_c, moe_w_down)
    moe0 = _moe_layer(0, ti, tg, h_tm, moe_w_up, moe_b_up, w_down_bf16, moe_b_down)

    x2, h_tm, ti, tg, w_down_bf16 = _conv_layer(
        x1, moe0, mod4, gn1, conv_w_pw1.astype(BF16), conv_b_pw1, conv_w_dw, conv_b_dw, conv_ln_g,
        conv_ln_b, conv_w_pw2.astype(BF16), conv_b_pw2, gn2, wr_t, br_c, moe_w_down)
    moe1 = _moe_layer(1, ti, tg, h_tm, moe_w_up, moe_b_up, w_down_bf16, moe_b_down)

    y_p, y_s = _final(x2, moe1, mod4, g_final[None, :])
    return (y_p.reshape(N_PROMPT, L_PROMPT, D), y_s.reshape(N_SAMPLE, L_SAMPLE, D), new_f, new_b)
```
